```python
import math
import jax, jax.numpy as jnp
from jax import lax
import numpy as np


D_MODEL = 2048
BATCH = 2
SEQ = 8192
DEPTH = 4

HEAD_DIM = 128
HEADS_PER_GROUP = 4
DILATION_GROUPS = ((128, 1), (512, 4), (2048, 16))
N_ATTN_HEADS = HEADS_PER_GROUP * len(DILATION_GROUPS)
ATTN_WIDTH = N_ATTN_HEADS * HEAD_DIM
ATTN_OUT = HEADS_PER_GROUP * HEAD_DIM
ROT_DIM = HEAD_DIM // 4
ROPE_THETA = 500000.0

SSM_WIDTH = D_MODEL // 2
SSM_GROUP = 16
SSM_GROUPS = SSM_WIDTH // SSM_GROUP
SSM_STATE = 64
SSM_DT_MIN = 0.001
SSM_DT_MAX = 0.1

N_BRANCHES = 2
N_IN = 3 * ATTN_WIDTH + SSM_WIDTH + N_BRANCHES * D_MODEL
IN_SPLITS = (ATTN_WIDTH, 2 * ATTN_WIDTH, 3 * ATTN_WIDTH, 3 * ATTN_WIDTH + SSM_WIDTH)

N_EXPERTS = 16
EXPERT_FF = D_MODEL // 2
CAPACITY_FACTOR = 2

PLE_DIM = 256

NORM_EPS = 1e-6
MASK_VALUE = -1e30

kernel_name = "hybrid_dilated_attn_s5_ec_moe_block"


def rms_norm(x, gain):
    xf = x.astype(jnp.float32)
    var = jnp.mean(xf * xf, axis=-1, keepdims=True)
    return (xf * lax.rsqrt(var + NORM_EPS) * gain.astype(jnp.float32)).astype(x.dtype)


def partial_rope(x, positions):
    half = ROT_DIM // 2
    inv_freq = jnp.power(ROPE_THETA, -jnp.arange(half, dtype=jnp.float32) * 2.0 / ROT_DIM)
    ang = positions.astype(jnp.float32)[..., None] * inv_freq
    cos = jnp.cos(ang)[:, :, None, :]
    sin = jnp.sin(ang)[:, :, None, :]
    xf = x.astype(jnp.float32)
    x1 = xf[..., :half]
    x2 = xf[..., half:ROT_DIM]
    out = jnp.concatenate([x1 * cos - x2 * sin, x2 * cos + x1 * sin, xf[..., ROT_DIM:]], axis=-1)
    return out.astype(x.dtype)


def dilated_window_attention(q, k, v, window, dilation):
    b, s, h, hd = q.shape
    n_side = (window // 2) // dilation
    blk = n_side
    sub_len = s // dilation
    nb = -(-sub_len // blk)
    pad_end = nb * blk - sub_len
    scale = HEAD_DIM ** -0.5

    def to_sub(t):
        return t.reshape(b, sub_len, dilation, h, hd).transpose(0, 2, 3, 1, 4)

    def key_windows(t):
        tp = jnp.pad(t, ((0, 0), (0, 0), (0, 0), (blk, pad_end + blk), (0, 0)))
        tp = tp.reshape(b, dilation, h, nb + 2, blk, hd)
        return jnp.concatenate([tp[:, :, :, :-2], tp[:, :, :, 1:-1], tp[:, :, :, 2:]], axis=4)

    qs = to_sub(q)
    qb = jnp.pad(qs, ((0, 0), (0, 0), (0, 0), (0, pad_end), (0, 0))).reshape(b, dilation, h, nb, blk, hd)
    kw = key_windows(to_sub(k))
    vw = key_windows(to_sub(v))

    q_pos = jnp.arange(nb)[:, None] * blk + jnp.arange(blk)[None, :]
    k_pos = jnp.arange(nb)[:, None] * blk - blk + jnp.arange(3 * blk)[None, :]
    dist = q_pos[:, :, None] - k_pos[:, None, :]
    valid = (jnp.abs(dist) <= n_side) & (k_pos[:, None, :] >= 0) & (k_pos[:, None, :] < sub_len)

    scores = jnp.einsum('bdhcqe,bdhcke->bdhcqk', qb, kw).astype(jnp.float32) * scale
    scores = jnp.where(valid, scores, MASK_VALUE)
    m = jnp.max(scores, axis=-1, keepdims=True)
    e = jnp.exp(scores - m)
    den = jnp.sum(e, axis=-1, keepdims=True)
    out = jnp.einsum('bdhcqk,bdhcke->bdhcqe', e / den, vw.astype(jnp.float32))
    lse = (m + jnp.log(den))[..., 0]

    out = out.reshape(b, dilation, h, nb * blk, hd)[:, :, :, :sub_len]
    out = out.transpose(0, 3, 1, 2, 4).reshape(b, s, h, hd)
    lse = lse.reshape(b, dilation, h, nb * blk)[:, :, :, :sub_len]
    lse = lse.transpose(0, 3, 1, 2).reshape(b, s, h)
    return out, lse


def attention_branch(q, k, v, q_gain, k_gain, positions):
    bsz, s = q.shape[0], q.shape[1]
    q = partial_rope(rms_norm(q, q_gain), positions)
    k = partial_rope(rms_norm(k, k_gain), positions)
    outs, lses = [], []
    for g, (window, dilation) in enumerate(DILATION_GROUPS):
        sl = slice(g * HEADS_PER_GROUP, (g + 1) * HEADS_PER_GROUP)
        o, l = dilated_window_attention(q[:, :, sl], k[:, :, sl], v[:, :, sl], window, dilation)
        outs.append(o)
        lses.append(l)
    w = jax.nn.softmax(jnp.stack(lses), axis=0)
    out = jnp.sum(w[..., None] * jnp.stack(outs), axis=0)
    return out.reshape(bsz, s, ATTN_OUT).astype(v.dtype)


def _complex_linear_combine(earlier, later):
    ar1, ai1, br1, bi1 = earlier
    ar2, ai2, br2, bi2 = later
    return (ar2 * ar1 - ai2 * ai1,
            ar2 * ai1 + ai2 * ar1,
            ar2 * br1 - ai2 * bi1 + br2,
            ar2 * bi1 + ai2 * br1 + bi2)


def s5_scan_direction(u, a_re, a_im, log_dt, b_re, b_im, c_re, c_im):
    f32 = jnp.float32
    a_re, a_im = a_re.astype(f32), a_im.astype(f32)
    b_re, b_im = b_re.astype(f32), b_im.astype(f32)
    dt = jnp.exp(log_dt.astype(f32))[:, None]
    mag = jnp.exp(a_re * dt)
    ang = a_im * dt
    lb_re = mag * jnp.cos(ang)
    lb_im = mag * jnp.sin(ang)
    den = a_re * a_re + a_im * a_im
    nr = lb_re - 1.0
    ni = lb_im
    coef_re = ((nr * a_re + ni * a_im) / den)[..., None]
    coef_im = ((ni * a_re - nr * a_im) / den)[..., None]
    bb_re = coef_re * b_re - coef_im * b_im
    bb_im = coef_re * b_im + coef_im * b_re
    bu_re = jnp.einsum('bsgh,gph->bsgp', u, bb_re)
    bu_im = jnp.einsum('bsgh,gph->bsgp', u, bb_im)
    ar = jnp.broadcast_to(lb_re, bu_re.shape)
    ai = jnp.broadcast_to(lb_im, bu_re.shape)
    _, _, x_re, x_im = lax.associative_scan(_complex_linear_combine, (ar, ai, bu_re, bu_im), axis=1)
    return (jnp.einsum('bsgp,ghp->bsgh', x_re, c_re.astype(f32))
            - jnp.einsum('bsgp,ghp->bsgh', x_im, c_im.astype(f32)))


def ssm_branch(u, d_skip, a_re, a_im, log_dt, b_re, b_im, c_re, c_im):
    bsz, s, _ = u.shape
    uf = u.astype(jnp.float32)
    ug = uf.reshape(bsz, s, SSM_GROUPS, SSM_GROUP)
    y_fwd = s5_scan_direction(ug, a_re[0], a_im[0], log_dt[0], b_re[0], b_im[0], c_re[0], c_im[0])
    y_bwd = jnp.flip(s5_scan_direction(jnp.flip(ug, axis=1), a_re[1], a_im[1], log_dt[1],
                                       b_re[1], b_im[1], c_re[1], c_im[1]), axis=1)
    return (y_fwd + y_bwd).reshape(bsz, s, SSM_WIDTH) + d_skip.astype(jnp.float32) * uf


def expert_choice_ffn(x, w_router, w_gate, w_up, w_down):
    bsz, s, d = x.shape
    capacity = CAPACITY_FACTOR * s // N_EXPERTS
    logits = jnp.einsum('bsd,de->bse', x, w_router).astype(jnp.float32)
    affinity = jax.nn.softmax(logits, axis=-1)
    gate, idx = lax.top_k(jnp.swapaxes(affinity, 1, 2), capacity)
    xg = jax.vmap(lambda xb, ib: xb[ib])(x, idx)
    hid = (jax.nn.silu(jnp.einsum('becd,edf->becf', xg, w_gate))
           * jnp.einsum('becd,edf->becf', xg, w_up))
    yg = jnp.einsum('becf,efd->becd', hid, w_down) * gate[..., None].astype(x.dtype)
    return jax.vmap(lambda yb, ib: jnp.zeros((s, d), x.dtype).at[ib.reshape(-1)].add(yb.reshape(-1, d)))(yg, idx)


def setup_inputs(seed: int = 0) -> dict:
    key = jax.random.key(seed)
    ks = jax.random.split(key, 26)
    f32 = jnp.float32
    L = DEPTH

    def normal(k, shape, scale):
        return jax.random.normal(k, shape, f32) * scale

    x = normal(ks[0], (BATCH, SEQ, D_MODEL), 1.0)
    p = normal(ks[1], (DEPTH, BATCH, SEQ, PLE_DIM), 1.0)
    positions = (jnp.arange(SEQ, dtype=jnp.int32)[None, :]
                 + jax.random.randint(ks[2], (BATCH, 1), 0, 1024, dtype=jnp.int32))
    norm_mix = 1.0 + normal(ks[3], (L, D_MODEL), 0.02)
    w_in = normal(ks[4], (L, D_MODEL, N_IN), D_MODEL ** -0.5)
    q_norm = 1.0 + normal(ks[5], (L, HEAD_DIM), 0.02)
    k_norm = 1.0 + normal(ks[6], (L, HEAD_DIM), 0.02)
    w_attn_br = normal(ks[7], (L, ATTN_OUT, D_MODEL), ATTN_OUT ** -0.5)
    n_idx = jnp.arange(SSM_STATE, dtype=f32)
    ssm_a_re = -0.5 + normal(ks[8], (L, 2, SSM_GROUPS, SSM_STATE), 0.01)
    ssm_a_im = jnp.pi * n_idx + normal(ks[9], (L, 2, SSM_GROUPS, SSM_STATE), 0.01)
    ssm_log_dt = jax.random.uniform(ks[10], (L, 2, SSM_GROUPS), f32,
                                    math.log(SSM_DT_MIN), math.log(SSM_DT_MAX))
    ssm_b_re = normal(ks[11], (L, 2, SSM_GROUPS, SSM_STATE, SSM_GROUP), (2 * SSM_GROUP) ** -0.5)
    ssm_b_im = normal(ks[12], (L, 2, SSM_GROUPS, SSM_STATE, SSM_GROUP), (2 * SSM_GROUP) ** -0.5)
    ssm_c_re = normal(ks[13], (L, 2, SSM_GROUPS, SSM_GROUP, SSM_STATE), 0.5)
    ssm_c_im = normal(ks[14], (L, 2, SSM_GROUPS, SSM_GROUP, SSM_STATE), 0.5)
    ssm_d = normal(ks[15], (L, SSM_WIDTH), 1.0)
    w_ssm_br = normal(ks[16], (L, SSM_WIDTH, 2 * D_MODEL), SSM_WIDTH ** -0.5)
    w_out = normal(ks[17], (L, D_MODEL, D_MODEL), D_MODEL ** -0.5)
    norm_ffn = 1.0 + normal(ks[18], (L, D_MODEL), 0.02)
    w_router = normal(ks[19], (L, D_MODEL, N_EXPERTS), D_MODEL ** -0.5)
    w_exp_gate = normal(ks[20], (L, N_EXPERTS, D_MODEL, EXPERT_FF), D_MODEL ** -0.5)
    w_exp_up = normal(ks[21], (L, N_EXPERTS, D_MODEL, EXPERT_FF), D_MODEL ** -0.5)
    w_exp_down = normal(ks[22], (L, N_EXPERTS, EXPERT_FF, D_MODEL), EXPERT_FF ** -0.5)
    norm_ple = 1.0 + normal(ks[23], (L, D_MODEL), 0.02)
    w_ple_gate = normal(ks[24], (L, D_MODEL, D_MODEL), D_MODEL ** -0.5)
    w_ple_proj = normal(ks[25], (L, PLE_DIM, D_MODEL), PLE_DIM ** -0.5)
    return {"x": x, "p": p, "positions": positions, "norm_mix": norm_mix, "w_in": w_in,
            "q_norm": q_norm, "k_norm": k_norm, "w_attn_br": w_attn_br,
            "ssm_a_re": ssm_a_re, "ssm_a_im": ssm_a_im, "ssm_log_dt": ssm_log_dt,
            "ssm_b_re": ssm_b_re, "ssm_b_im": ssm_b_im, "ssm_c_re": ssm_c_re, "ssm_c_im": ssm_c_im,
            "ssm_d": ssm_d, "w_ssm_br": w_ssm_br, "w_out": w_out, "norm_ffn": norm_ffn,
            "w_router": w_router, "w_exp_gate": w_exp_gate, "w_exp_up": w_exp_up,
            "w_exp_down": w_exp_down, "norm_ple": norm_ple, "w_ple_gate": w_ple_gate,
            "w_ple_proj": w_ple_proj}


def reference(x, p, positions, norm_mix, w_in, q_norm, k_norm, w_attn_br,
              ssm_a_re, ssm_a_im, ssm_log_dt, ssm_b_re, ssm_b_im, ssm_c_re, ssm_c_im,
              ssm_d, w_ssm_br, w_out, norm_ffn, w_router, w_exp_gate, w_exp_up,
              w_exp_down, norm_ple, w_ple_gate, w_ple_proj):
    bsz, s, d = x.shape
    h = x
    for l in range(DEPTH):
        xn = rms_norm(h, norm_mix[l])
        z = xn @ w_in[l]
        q, k, v, u, gate_logits = jnp.split(z, IN_SPLITS, axis=-1)
        q = q.reshape(bsz, s, N_ATTN_HEADS, HEAD_DIM)
        k = k.reshape(bsz, s, N_ATTN_HEADS, HEAD_DIM)
        v = v.reshape(bsz, s, N_ATTN_HEADS, HEAD_DIM)

        a_branch = attention_branch(q, k, v, q_norm[l], k_norm[l], positions) @ w_attn_br[l]

        y = ssm_branch(u, ssm_d[l], ssm_a_re[l], ssm_a_im[l], ssm_log_dt[l],
                       ssm_b_re[l], ssm_b_im[l], ssm_c_re[l], ssm_c_im[l])
        zs = jax.nn.gelu(y).astype(h.dtype) @ w_ssm_br[l]
        s_branch = zs[..., :d] * jax.nn.sigmoid(zs[..., d:])

        g_attn = jax.nn.sigmoid(gate_logits[..., :d])
        g_ssm = jax.nn.sigmoid(gate_logits[..., d:])
        h = h + (g_attn * a_branch + g_ssm * s_branch) @ w_out[l]

        h = h + expert_choice_ffn(rms_norm(h, norm_ffn[l]), w_router[l],
                                  w_exp_gate[l], w_exp_up[l], w_exp_down[l])

        ple = p[l] @ w_ple_proj[l]
        h = h + jax.nn.sigmoid(rms_norm(h, norm_ple[l]) @ w_ple_gate[l]) * ple
    return h
```

```python
import functools
import math

import jax
import jax.numpy as jnp
from jax import lax
from jax.experimental import pallas as pl
from jax.experimental.pallas import tpu as pltpu

F32 = jnp.float32
BF16 = jnp.bfloat16
I32 = jnp.int32

NORM_EPS = 1e-6
MASK_VALUE = -1e30
ROPE_THETA = 500000.0

HEAD_DIM = 128
HEADS_PER_GROUP = 4
GROUP_WIDTH = HEADS_PER_GROUP * HEAD_DIM
DILATIONS = (1, 4, 16)
N_SIDE = 64
ROT_HALF = 16

SSM_GROUP = 16
SSM_STATE = 64
SSM_CHUNK = 16

N_EXPERTS = 16
CAPACITY_FACTOR = 2
SLOT_CHUNK = 128

PERM_TILE = 1024
VMEM_LIMIT = 56 * 1024 * 1024


def _params(*sem):
    return pltpu.CompilerParams(dimension_semantics=sem, vmem_limit_bytes=VMEM_LIMIT)


def _rms(x, gain):
    var = jnp.mean(x * x, axis=-1, keepdims=True)
    return x * lax.rsqrt(var + NORM_EPS) * gain


def _qkv_body(h_ref, g_ref, w_ref, qkg_ref, cos_ref, sin_ref, o_ref, xn_ref, slab_ref):
    j = pl.program_id(1)
    tm = h_ref.shape[0]

    @pl.when(j == 0)
    def _():
        xn_ref[...] = _rms(h_ref[...], g_ref[...]).astype(BF16)

    acc = jnp.dot(xn_ref[...], w_ref[...], preferred_element_type=F32)
    for hs in range(HEADS_PER_GROUP):
        slab_ref[hs] = acc[:, hs * HEAD_DIM:(hs + 1) * HEAD_DIM]

    @pl.when(j % 3 < 2)
    def _():
        lane = lax.broadcasted_iota(I32, (tm, HEAD_DIM), 1)
        cos = cos_ref[...]
        sin = sin_ref[...]
        gain = qkg_ref[...]
        for hs in range(HEADS_PER_GROUP):
            a = _rms(slab_ref[hs], gain)
            rot = jnp.where(lane < ROT_HALF,
                            pltpu.roll(a, HEAD_DIM - ROT_HALF, 1), pltpu.roll(a, ROT_HALF, 1))
            slab_ref[hs] = a * cos + rot * sin

    for gi, d in enumerate(DILATIONS):
        @pl.when(j // 3 == gi)
        def _(d=d):
            n = tm // d
            for hs in range(HEADS_PER_GROUP):
                for r in range(d):
                    v = slab_ref[hs] if d == 1 else slab_ref[hs, pl.ds(r, n, stride=d), :]
                    o_ref[r * n:(r + 1) * n, hs * HEAD_DIM:(hs + 1) * HEAD_DIM] = v.astype(BF16)


def _qkv_proj(h, gain, w_qkv, qk_gain, cos, sin):
    t, dm = h.shape
    tm, tn = PERM_TILE, GROUP_WIDTH
    nj = w_qkv.shape[1] // tn
    return pl.pallas_call(
        _qkv_body,
        grid=(t // tm, nj),
        in_specs=[
            pl.BlockSpec((tm, dm), lambda i, j: (i, 0)),
            pl.BlockSpec((1, dm), lambda i, j: (0, 0)),
            pl.BlockSpec((dm, tn), lambda i, j: (0, j)),
            pl.BlockSpec((None, 1, HEAD_DIM), lambda i, j: (j % 3, 0, 0)),
            pl.BlockSpec((tm, HEAD_DIM), lambda i, j: (i, 0)),
            pl.BlockSpec((tm, HEAD_DIM), lambda i, j: (i, 0)),
        ],
        out_specs=pl.BlockSpec((tm, tn), lambda i, j: (i, j)),
        out_shape=jax.ShapeDtypeStruct((t, nj * tn), BF16),
        scratch_shapes=[pltpu.VMEM((tm, dm), BF16),
                        pltpu.VMEM((HEADS_PER_GROUP, tm, HEAD_DIM), F32)],
        compiler_params=_params("parallel", "arbitrary"),
        name="qkv_proj",
    )(h, gain, w_qkv, qk_gain, cos, sin)


def _norm_mm_body(h_ref, g_ref, w_ref, o_ref, xn_ref, *, act):
    @pl.when(pl.program_id(1) == 0)
    def _():
        xn_ref[...] = _rms(h_ref[...], g_ref[...]).astype(BF16)

    acc = jnp.dot(xn_ref[...], w_ref[...], preferred_element_type=F32)
    if act == "sigmoid":
        acc = jax.nn.sigmoid(acc)
    o_ref[...] = acc.astype(o_ref.dtype)


def _norm_matmul(h, gain, w, *, out_dtype, act=None, tn=512, name):
    t, dm = h.shape
    tm = 1024
    nj = w.shape[1] // tn
    return pl.pallas_call(
        functools.partial(_norm_mm_body, act=act),
        grid=(t // tm, nj),
        in_specs=[
            pl.BlockSpec((tm, dm), lambda i, j: (i, 0)),
            pl.BlockSpec((1, dm), lambda i, j: (0, 0)),
            pl.BlockSpec((dm, tn), lambda i, j: (0, j)),
        ],
        out_specs=pl.BlockSpec((tm, tn), lambda i, j: (i, j)),
        out_shape=jax.ShapeDtypeStruct((t, nj * tn), out_dtype),
        scratch_shapes=[pltpu.VMEM((tm, dm), BF16)],
        compiler_params=_params("parallel", "arbitrary"),
        name=name,
    )(h, gain, w)


def _attn_body(q_ref, kp_ref, km_ref, kn_ref, vp_ref, vm_ref, vn_ref, o_ref, l_ref,
               kw_ref, vw_ref, os_ref, ls_ref, *, tq, sub_len):
    c = pl.program_id(2)
    kw_ref[0:N_SIDE] = kp_ref[...]
    kw_ref[N_SIDE:N_SIDE + tq] = km_ref[...].reshape(tq, GROUP_WIDTH)
    kw_ref[N_SIDE + tq:] = kn_ref[...]
    vw_ref[0:N_SIDE] = vp_ref[...]
    vw_ref[N_SIDE:N_SIDE + tq] = vm_ref[...].reshape(tq, GROUP_WIDTH)
    vw_ref[N_SIDE + tq:] = vn_ref[...]
    q = q_ref[...].reshape(tq, GROUP_WIDTH)

    sb = min(128, tq)
    nk = sb + 2 * N_SIDE
    scale = HEAD_DIM ** -0.5
    row = lax.broadcasted_iota(I32, (sb, nk), 0)
    col = lax.broadcasted_iota(I32, (sb, nk), 1)
    band = jnp.abs(col - row - N_SIDE) <= N_SIDE
    for i in range(tq // sb):
        kpos = c * tq + (i * sb - N_SIDE) + col
        valid = band & (kpos >= 0) & (kpos < sub_len)
        for hs in range(HEADS_PER_GROUP):
            lanes = slice(hs * HEAD_DIM, (hs + 1) * HEAD_DIM)
            qh = q[i * sb:(i + 1) * sb, lanes]
            kh = kw_ref[i * sb:i * sb + nk, lanes]
            vh = vw_ref[i * sb:i * sb + nk, lanes]
            s = lax.dot_general(qh, kh, (((1,), (1,)), ((), ())), preferred_element_type=F32) * scale
            s = jnp.where(valid, s, MASK_VALUE)
            m = jnp.max(s, axis=-1, keepdims=True)
            e = jnp.exp(s - m)
            den = jnp.sum(e, axis=-1, keepdims=True)
            o = jnp.dot((e / den).astype(BF16), vh, preferred_element_type=F32)
            os_ref[i * sb:(i + 1) * sb, lanes] = o
            ls_ref[i * sb:(i + 1) * sb, lanes] = jnp.broadcast_to(m + jnp.log(den), (sb, HEAD_DIM))
    o_ref[...] = os_ref[...].reshape(o_ref.shape)
    l_ref[...] = ls_ref[...].reshape(l_ref.shape)


def _attention(qkv, gi, bsz, seq):
    d = DILATIONS[gi]
    t = bsz * seq
    sub_len = seq // d
    nbt = PERM_TILE // (N_SIDE * d)
    ntile = seq // PERM_TILE
    tq = min(512, sub_len)
    nbq = tq // N_SIDE
    nblk = sub_len // N_SIDE
    view = (bsz, ntile, d, nbt, N_SIDE, qkv.shape[1])
    oview = (bsz, ntile, d, nbt, N_SIDE, GROUP_WIDTH)
    gw = GROUP_WIDTH

    if nbt >= nbq:
        per = nbt // nbq
        main_shape = (None, None, None, nbq, N_SIDE, gw)

        def main_idx(col):
            return lambda b, r, c: (b, c // per, r, c % per, 0, col)
    else:
        main_shape = (None, nbq // nbt, None, nbt, N_SIDE, gw)

        def main_idx(col):
            return lambda b, r, c: (b, c, r, 0, 0, col)

    halo_shape = (None, None, None, None, N_SIDE, gw)

    def prev_idx(col):
        def f(b, r, c):
            n = jnp.maximum(c * nbq - 1, 0)
            return (b, n // nbt, r, n % nbt, 0, col)
        return f

    def next_idx(col):
        def f(b, r, c):
            n = jnp.minimum((c + 1) * nbq, nblk - 1)
            return (b, n // nbt, r, n % nbt, 0, col)
        return f

    qc, kc, vc = 3 * gi, 3 * gi + 1, 3 * gi + 2
    x = qkv.reshape(view)
    o, l = pl.pallas_call(
        functools.partial(_attn_body, tq=tq, sub_len=sub_len),
        grid=(bsz, d, sub_len // tq),
        in_specs=[
            pl.BlockSpec(main_shape, main_idx(qc)),
            pl.BlockSpec(halo_shape, prev_idx(kc)),
            pl.BlockSpec(main_shape, main_idx(kc)),
            pl.BlockSpec(halo_shape, next_idx(kc)),
            pl.BlockSpec(halo_shape, prev_idx(vc)),
            pl.BlockSpec(main_shape, main_idx(vc)),
            pl.BlockSpec(halo_shape, next_idx(vc)),
        ],
        out_specs=[pl.BlockSpec(main_shape, main_idx(0)), pl.BlockSpec(main_shape, main_idx(0))],
        out_shape=[jax.ShapeDtypeStruct(oview, F32), jax.ShapeDtypeStruct(oview, F32)],
        scratch_shapes=[pltpu.VMEM((tq + 2 * N_SIDE, gw), BF16), pltpu.VMEM((tq + 2 * N_SIDE, gw), BF16),
                        pltpu.VMEM((tq, gw), F32), pltpu.VMEM((tq, gw), F32)],
        compiler_params=_params("parallel", "parallel", "arbitrary"),
        name=f"dilated_attn_d{d}",
    )(x, x, x, x, x, x, x)
    return o.reshape(t, gw), l.reshape(t, gw)


def _attn_merge_body(o0, l0, o1, l1, o2, l2, w_ref, g_ref, out_ref, comb_ref, so_ref, sl_ref):
    tm = out_ref.shape[0]

    @pl.when(pl.program_id(1) == 0)
    def _():
        for gi, (o_ref, l_ref) in enumerate(((o0, l0), (o1, l1), (o2, l2))):
            d = DILATIONS[gi]
            n = tm // d
            for hs in range(HEADS_PER_GROUP):
                lanes = slice(hs * HEAD_DIM, (hs + 1) * HEAD_DIM)
                for r in range(d):
                    ov = o_ref[r * n:(r + 1) * n, lanes]
                    lv = l_ref[r * n:(r + 1) * n, lanes]
                    if d == 1:
                        so_ref[gi, hs] = ov
                        sl_ref[gi, hs] = lv
                    else:
                        so_ref[gi, hs, pl.ds(r, n, stride=d), :] = ov
                        sl_ref[gi, hs, pl.ds(r, n, stride=d), :] = lv
        for hs in range(HEADS_PER_GROUP):
            ls = [sl_ref[gi, hs] for gi in range(3)]
            mx = jnp.maximum(jnp.maximum(ls[0], ls[1]), ls[2])
            ws = [jnp.exp(l - mx) for l in ls]
            num = ws[0] * so_ref[0, hs] + ws[1] * so_ref[1, hs] + ws[2] * so_ref[2, hs]
            comb = num / (ws[0] + ws[1] + ws[2])
            comb_ref[:, hs * HEAD_DIM:(hs + 1) * HEAD_DIM] = comb.astype(BF16)

    acc = jnp.dot(comb_ref[...], w_ref[...], preferred_element_type=F32)
    out_ref[...] = (acc * g_ref[...].astype(F32)).astype(out_ref.dtype)


def _attn_merge_proj(outs, lses, w_attn, gates):
    t = outs[0].shape[0]
    dm = w_attn.shape[1]
    tm, tn = PERM_TILE, 512
    row = pl.BlockSpec((tm, GROUP_WIDTH), lambda i, j: (i, 0))
    return pl.pallas_call(
        _attn_merge_body,
        grid=(t // tm, dm // tn),
        in_specs=[row, row, row, row, row, row,
                  pl.BlockSpec((GROUP_WIDTH, tn), lambda i, j: (0, j)),
                  pl.BlockSpec((tm, tn), lambda i, j: (i, j))],
        out_specs=pl.BlockSpec((tm, tn), lambda i, j: (i, j)),
        out_shape=jax.ShapeDtypeStruct((t, dm), BF16),
        scratch_shapes=[pltpu.VMEM((tm, GROUP_WIDTH), BF16),
                        pltpu.VMEM((3, HEADS_PER_GROUP, tm, HEAD_DIM), F32),
                        pltpu.VMEM((3, HEADS_PER_GROUP, tm, HEAD_DIM), F32)],
        compiler_params=_params("parallel", "arbitrary"),
        name="attn_merge_proj",
    )(outs[0], lses[0], outs[1], lses[1], outs[2], lses[2], w_attn, gates)


def _ssm_tables(a_re, a_im, log_dt, b_re, b_im, c_re, c_im, n_steps):
    hi = lax.Precision.HIGHEST
    lc = SSM_CHUNK
    a_re, a_im = a_re.astype(F32), a_im.astype(F32)
    dt = jnp.exp(log_dt.astype(F32))[..., None]

    def lam_pow(n):
        n = n.astype(F32)[:, None, None, None]
        mag = jnp.exp(a_re * dt * n)
        ang = a_im * dt * n
        return mag * jnp.cos(ang), mag * jnp.sin(ang)

    one_re, one_im = lam_pow(jnp.ones((1,)))
    lb_re, lb_im = one_re[0], one_im[0]
    den = a_re * a_re + a_im * a_im
    nr, ni = lb_re - 1.0, lb_im
    coef_re = ((nr * a_re + ni * a_im) / den)[..., None]
    coef_im = ((ni * a_re - nr * a_im) / den)[..., None]
    b_re, b_im = b_re.astype(F32), b_im.astype(F32)
    bb_re = coef_re * b_re - coef_im * b_im
    bb_im = coef_re * b_im + coef_im * b_re
    c_re, c_im = c_re.astype(F32), c_im.astype(F32)

    pw_re, pw_im = lam_pow(jnp.arange(lc + 1))
    e_re = pw_re[..., None] * bb_re - pw_im[..., None] * bb_im
    e_im = pw_re[..., None] * bb_im + pw_im[..., None] * bb_re
    kern = (jnp.einsum('dgop,tdgpi->tdgoi', c_re, e_re, precision=hi)
            - jnp.einsum('dgop,tdgpi->tdgoi', c_im, e_im, precision=hi))
    s_idx = jnp.arange(lc)[:, None]
    t_idx = jnp.arange(lc)[None, :]
    lag_f = jnp.clip(t_idx - s_idx, 0, lc)
    lag_b = jnp.clip(s_idx - t_idx, 0, lc)
    kf = jnp.where((t_idx >= s_idx)[..., None, None, None], kern[lag_f, 0], 0.0)
    kb = jnp.where((s_idx >= t_idx)[..., None, None, None], kern[lag_b, 1], 0.0)
    toep = jnp.transpose(kf + kb, (2, 0, 4, 1, 3))
    g = toep.shape[0]
    toep = toep.reshape(g, lc * SSM_GROUP, lc * SSM_GROUP)

    def state_in(e, direction, taus):
        return jnp.transpose(e[taus, direction], (1, 0, 3, 2)).reshape(g, lc * SSM_GROUP, SSM_STATE)

    tau_f = lc - 1 - jnp.arange(lc)
    tau_b = jnp.arange(lc)
    q_in = jnp.concatenate([state_in(e_re, 0, tau_f), state_in(e_im, 0, tau_f),
                            state_in(e_re, 1, tau_b), state_in(e_im, 1, tau_b)], axis=-1)
    tq = jnp.concatenate([toep, q_in], axis=-1)

    def state_out(direction, taus):
        lr = pw_re[taus, direction][:, :, None, :]
        li = pw_im[taus, direction][:, :, None, :]
        cr, ci = c_re[direction][None], c_im[direction][None]
        mr = cr * lr - ci * li
        mi = cr * li + ci * lr
        to_rows = lambda m: jnp.transpose(m, (1, 3, 0, 2)).reshape(g, SSM_STATE, lc * SSM_GROUP)
        return jnp.concatenate([to_rows(mr), to_rows(-mi)], axis=1)

    pm = jnp.concatenate([state_out(0, jnp.arange(lc) + 1), state_out(1, lc - jnp.arange(lc))], axis=1)

    sc_re, sc_im = lam_pow(lc * (2 ** jnp.arange(n_steps)))
    a1 = jnp.concatenate([sc_re, sc_re], axis=-1)
    a2 = jnp.concatenate([-sc_im, sc_im], axis=-1)
    tab = jnp.stack([a1, a2], axis=2)
    tab = jnp.transpose(tab, (3, 1, 0, 2, 4)).reshape(g, 4 * n_steps, 2 * SSM_STATE)
    return tq.astype(BF16), pm.astype(BF16), tab


def _ssm_body(uc_ref, tq_ref, pm_ref, tab_ref, y_ref, *, n_steps):
    gb, nc, kw = uc_ref.shape
    half = 2 * SSM_STATE
    row = lax.broadcasted_iota(I32, (nc, half), 0)

    def shifted(x, sh, direction):
        if direction == 0:
            return jnp.where(row >= sh, pltpu.roll(x, sh, 0), 0.0)
        return jnp.where(row < nc - sh, pltpu.roll(x, nc - sh, 0), 0.0)

    def one_group(g, carry):
        r = jnp.dot(uc_ref[g], tq_ref[g], preferred_element_type=F32)
        states = []
        for direction in range(2):
            x = r[:, kw + half * direction:kw + half * (direction + 1)]
            for k in range(n_steps):
                base = (direction * n_steps + k) * 2
                a1 = tab_ref[g, pl.ds(base, 1), :]
                a2 = tab_ref[g, pl.ds(base + 1, 1), :]
                xs = shifted(x, 1 << k, direction)
                x = x + a1 * xs + a2 * pltpu.roll(xs, SSM_STATE, 1)
            states.append(shifted(x, 1, direction))
        xin = jnp.concatenate(states, axis=1).astype(BF16)
        y_ref[g] = r[:, :kw] + jnp.dot(xin, pm_ref[g], preferred_element_type=F32)
        return carry

    lax.fori_loop(0, gb, one_group, 0)


def _ssm_scan(uc, tq, pm, tab):
    bsz, g, nc, kw = uc.shape
    gb = 8
    n_steps = int(math.log2(nc))
    return pl.pallas_call(
        functools.partial(_ssm_body, n_steps=n_steps),
        grid=(bsz, g // gb),
        in_specs=[
            pl.BlockSpec((None, gb, nc, kw), lambda b, j: (b, j, 0, 0)),
            pl.BlockSpec((gb, kw, 2 * kw), lambda b, j: (j, 0, 0)),
            pl.BlockSpec((gb, kw, kw), lambda b, j: (j, 0, 0)),
            pl.BlockSpec((gb, 4 * n_steps, 2 * SSM_STATE), lambda b, j: (j, 0, 0)),
        ],
        out_specs=pl.BlockSpec((None, gb, nc, kw), lambda b, j: (b, j, 0, 0)),
        out_shape=jax.ShapeDtypeStruct((bsz, g, nc, kw), F32),
        compiler_params=_params("parallel", "arbitrary"),
        name="s5_chunk_scan",
    )(uc, tq, pm, tab)


def _ssm_glu_body(y_ref, u_ref, d_ref, wa_ref, wb_ref, ga_ref, gs_ref, o_ref, act_ref):
    @pl.when(pl.program_id(1) == 0)
    def _():
        act_ref[...] = jax.nn.gelu(y_ref[...] + d_ref[...] * u_ref[...]).astype(BF16)

    act = act_ref[...]
    za = jnp.dot(act, wa_ref[...], preferred_element_type=F32)
    zb = jnp.dot(act, wb_ref[...], preferred_element_type=F32)
    s_branch = za * jax.nn.sigmoid(zb)
    o_ref[...] = (ga_ref[...].astype(F32) + gs_ref[...].astype(F32) * s_branch).astype(o_ref.dtype)


def _ssm_glu_merge(y, u, d_skip, w_ssm, gated_attn, gates):
    t, width = y.shape
    dm = gated_attn.shape[1]
    tm, tn = 1024, 512
    nj = dm // tn
    return pl.pallas_call(
        _ssm_glu_body,
        grid=(t // tm, nj),
        in_specs=[
            pl.BlockSpec((tm, width), lambda i, j: (i, 0)),
            pl.BlockSpec((tm, width), lambda i, j: (i, 0)),
            pl.BlockSpec((1, width), lambda i, j: (0, 0)),
            pl.BlockSpec((width, tn), lambda i, j: (0, j)),
            pl.BlockSpec((width, tn), lambda i, j: (0, j + nj)),
            pl.BlockSpec((tm, tn), lambda i, j: (i, j)),
            pl.BlockSpec((tm, tn), lambda i, j: (i, j + nj)),
        ],
        out_specs=pl.BlockSpec((tm, tn), lambda i, j: (i, j)),
        out_shape=jax.ShapeDtypeStruct((t, dm), BF16),
        scratch_shapes=[pltpu.VMEM((tm, width), BF16)],
        compiler_params=_params("parallel", "arbitrary"),
        name="s5_glu_merge",
    )(y, u, d_skip, w_ssm, w_ssm, gated_attn, gates)


def _resid_mm_body(x_ref, w_ref, h_ref, o_ref):
    o_ref[...] = h_ref[...] + jnp.dot(x_ref[...], w_ref[...], preferred_element_type=F32)


def _resid_matmul(x, w, h):
    t, k = x.shape
    dm = w.shape[1]
    tm, tn = 1024, 512
    return pl.pallas_call(
        _resid_mm_body,
        grid=(t // tm, dm // tn),
        in_specs=[
            pl.BlockSpec((tm, k), lambda i, j: (i, 0)),
            pl.BlockSpec((k, tn), lambda i, j: (0, j)),
            pl.BlockSpec((tm, tn), lambda i, j: (i, j)),
        ],
        out_specs=pl.BlockSpec((tm, tn), lambda i, j: (i, j)),
        out_shape=jax.ShapeDtypeStruct((t, dm), F32),
        compiler_params=_params("parallel", "arbitrary"),
        name="out_proj",
    )(x, w, h)


def _router_body(h_ref, g_ref, wr_ref, xn_ref, lg_ref):
    xn = _rms(h_ref[...], g_ref[...])
    xn_ref[...] = xn.astype(BF16)
    lg_ref[...] = lax.dot_general(wr_ref[...], xn, (((1,), (1,)), ((), ())),
                                  precision=lax.Precision.HIGHEST, preferred_element_type=F32)


def _router_logits(h, gain, w_router_t):
    t, dm = h.shape
    tm = 512
    return pl.pallas_call(
        _router_body,
        grid=(t // tm,),
        in_specs=[
            pl.BlockSpec((tm, dm), lambda i: (i, 0)),
            pl.BlockSpec((1, dm), lambda i: (0, 0)),
            pl.BlockSpec((N_EXPERTS, dm), lambda i: (0, 0)),
        ],
        out_specs=[pl.BlockSpec((tm, dm), lambda i: (i, 0)),
                   pl.BlockSpec((N_EXPERTS, tm), lambda i: (0, i))],
        out_shape=[jax.ShapeDtypeStruct((t, dm), BF16), jax.ShapeDtypeStruct((N_EXPERTS, t), F32)],
        compiler_params=_params("parallel"),
        name="router_logits",
    )(h, gain, w_router_t)


def _select_body(lg_ref, gate_ref, pos_ref, sel_ref, *, cap):
    lg = lg_ref[...]
    ne, s = lg.shape
    m = jnp.max(lg, axis=0, keepdims=True)
    e = jnp.exp(lg - m)
    aff = e / jnp.sum(e, axis=0, keepdims=True)
    bits = lax.bitcast_convert_type(aff, I32)

    def count(ind):
        return jnp.sum(ind, axis=1, keepdims=True)

    def value_bit(k, thr):
        cand = thr | jnp.left_shift(jnp.int32(1), 30 - k)
        return jnp.where(count(jnp.where(bits >= cand, 1.0, 0.0)) >= cap, cand, thr)

    thr = lax.fori_loop(0, 31, value_bit, jnp.zeros((ne, 1), I32))
    above = jnp.where(bits > thr, 1.0, 0.0)
    tie = jnp.where(bits == thr, 1.0, 0.0)
    need = cap - count(above)
    idx = lax.broadcasted_iota(I32, (ne, s), 1)
    n_bits = int(math.log2(s)) + 1

    def index_bit(k, bound):
        cand = bound | jnp.left_shift(jnp.int32(1), n_bits - 1 - k)
        below = count(jnp.where(idx < cand, tie, 0.0))
        return jnp.where(below < need, cand, bound)

    bound = lax.fori_loop(0, n_bits, index_bit, jnp.zeros((ne, 1), I32))
    sel = above + jnp.where(idx <= bound, tie, 0.0)
    gate_ref[...] = sel * aff
    sel_ref[...] = sel.astype(I32)

    blk = 256
    tri = jnp.where(lax.broadcasted_iota(I32, (blk, blk), 0) <= lax.broadcasted_iota(I32, (blk, blk), 1),
                    1.0, 0.0).astype(BF16)
    carry = jnp.zeros((ne, 1), F32)
    for j in range(s // blk):
        seg = sel[:, j * blk:(j + 1) * blk]
        inc = jnp.dot(seg.astype(BF16), tri, preferred_element_type=F32)
        pos_ref[:, j * blk:(j + 1) * blk] = (inc - seg + carry).astype(I32)
        carry = carry + inc[:, blk - 1:blk]


def _select_tokens(logits_t, bsz, seq):
    cap = CAPACITY_FACTOR * seq // N_EXPERTS
    t = bsz * seq
    spec = pl.BlockSpec((N_EXPERTS, seq), lambda b: (0, b))
    return pl.pallas_call(
        functools.partial(_select_body, cap=cap),
        grid=(bsz,),
        in_specs=[spec],
        out_specs=[spec, spec, spec],
        out_shape=[jax.ShapeDtypeStruct((N_EXPERTS, t), F32), jax.ShapeDtypeStruct((N_EXPERTS, t), I32),
                   jax.ShapeDtypeStruct((N_EXPERTS, t), I32)],
        compiler_params=_params("parallel"),
        name="expert_choice_select",
    )(logits_t)


def _dispatch_body(base_ref, x_ref, slot_ref, out_ref, *, nt):
    b = pl.program_id(0)
    i = pl.program_id(2)
    ts = x_ref.shape[0]

    @pl.when(i == 0)
    def _():
        out_ref[...] = jnp.zeros_like(out_ref)

    x = x_ref[...]
    wio = lax.broadcasted_iota(I32, (SLOT_CHUNK, ts), 0)
    for e in range(N_EXPERTS):
        k = (b * N_EXPERTS + e) * (nt + 1) + i
        lo = base_ref[k]
        hi = base_ref[k + 1]
        srow = slot_ref[e:e + 1, :]

        def chunk(q, carry, e=e, srow=srow):
            onehot = jnp.where(srow == q * SLOT_CHUNK + wio, 1.0, 0.0).astype(x.dtype)
            res = jnp.dot(onehot, x, preferred_element_type=F32)
            r0 = pl.multiple_of(q * SLOT_CHUNK, SLOT_CHUNK)
            out_ref[e, pl.ds(r0, SLOT_CHUNK), :] += res.astype(out_ref.dtype)
            return carry

        lax.fori_loop(lo // SLOT_CHUNK, (hi + SLOT_CHUNK - 1) // SLOT_CHUNK, chunk, 0)


def _dispatch(base, x, slot_t, bsz, seq, cap, *, cw, out_dtype, name):
    t, width = x.shape
    ts = 256
    nt = seq // ts
    grid_spec = pltpu.PrefetchScalarGridSpec(
        num_scalar_prefetch=1,
        grid=(bsz, width // cw, nt),
        in_specs=[
            pl.BlockSpec((ts, cw), lambda b, c, i, base: (b * nt + i, c)),
            pl.BlockSpec((N_EXPERTS, ts), lambda b, c, i, base: (0, b * nt + i)),
        ],
        out_specs=pl.BlockSpec((None, N_EXPERTS, cap, cw), lambda b, c, i, base: (b, 0, 0, c)),
    )
    return pl.pallas_call(
        functools.partial(_dispatch_body, nt=nt),
        grid_spec=grid_spec,
        out_shape=jax.ShapeDtypeStruct((bsz, N_EXPERTS, cap, width), out_dtype),
        compiler_params=_params("parallel", "parallel", "arbitrary"),
        name=name,
    )(base, x, slot_t)


def _expert_body(x_ref, wg_ref, wu_ref, wd_ref, gs_ref, y_ref, acc_ref):
    e = pl.program_id(0)
    f = pl.program_id(2)
    x = x_ref[...]
    hid = (jax.nn.silu(jnp.dot(x, wg_ref[...], preferred_element_type=F32))
           * jnp.dot(x, wu_ref[...], preferred_element_type=F32)).astype(BF16)
    part = jnp.dot(hid, wd_ref[...], preferred_element_type=F32)

    @pl.when(f == 0)
    def _():
        acc_ref[...] = part

    @pl.when(f > 0)
    def _():
        acc_ref[...] += part

    @pl.when(f == pl.num_programs(2) - 1)
    def _():
        pieces = gs_ref[...]
        lane = lax.broadcasted_iota(I32, pieces.shape, 1)
        mine = (lane >= 3 * e) & (lane < 3 * e + 3)
        gate = jnp.sum(jnp.where(mine, pieces, 0.0), axis=1, keepdims=True)
        y_ref[...] = (acc_ref[...] * gate).astype(y_ref.dtype)


def _expert_ffn(xg, w_gate, w_up, w_down, gate_slots):
    bsz, ne, cap, dm = xg.shape
    ff = w_gate.shape[2]
    tf = 512
    return pl.pallas_call(
        _expert_body,
        grid=(ne, bsz, ff // tf),
        in_specs=[
            pl.BlockSpec((None, None, cap, dm), lambda e, b, f: (b, e, 0, 0)),
            pl.BlockSpec((None, dm, tf), lambda e, b, f: (e, 0, f)),
            pl.BlockSpec((None, dm, tf), lambda e, b, f: (e, 0, f)),
            pl.BlockSpec((None, tf, dm), lambda e, b, f: (e, f, 0)),
            pl.BlockSpec((None, None, cap, 128), lambda e, b, f: (b, e, 0, 0)),
        ],
        out_specs=pl.BlockSpec((None, None, cap, dm), lambda e, b, f: (b, e, 0, 0)),
        out_shape=jax.ShapeDtypeStruct((bsz, ne, cap, dm), BF16),
        scratch_shapes=[pltpu.VMEM((cap, dm), F32)],
        compiler_params=_params("parallel", "parallel", "arbitrary"),
        name="expert_swiglu",
    )(xg, w_gate, w_up, w_down, gate_slots)


def _combine_body(base_ref, yg_ref, h_ref, slot_ref, out_ref, pst_ref, wst_ref, *, nt, kbc):
    b = pl.program_id(0)
    i = pl.program_id(2)
    ts = h_ref.shape[0]

    @pl.when((b == 0) & (pl.program_id(1) == 0) & (i == 0))
    def _():
        wst_ref[...] = jnp.zeros_like(wst_ref)

    lane = lax.broadcasted_iota(I32, (ts, SLOT_CHUNK), 1)
    n_used = jnp.int32(0)
    for e in range(N_EXPERTS):
        k = (b * N_EXPERTS + e) * (nt + 1) + i
        lo = base_ref[k]
        hi = base_ref[k + 1]
        scol = slot_ref[:, e:e + 1]

        def chunk(q, n, e=e, scol=scol):
            pst_ref[n] = jnp.where(scol == q * SLOT_CHUNK + lane, 1.0, 0.0).astype(BF16)
            r0 = pl.multiple_of(q * SLOT_CHUNK, SLOT_CHUNK)
            wst_ref[n] = yg_ref[e, pl.ds(r0, SLOT_CHUNK), :]
            return n + 1

        n_used = lax.fori_loop(lo // SLOT_CHUNK, (hi + SLOT_CHUNK - 1) // SLOT_CHUNK, chunk, n_used)

    n_blocks = (n_used + kbc - 1) // kbc

    def pad(n, carry):
        pst_ref[n] = jnp.zeros((ts, SLOT_CHUNK), BF16)
        return carry

    lax.fori_loop(n_used, n_blocks * kbc, pad, 0)
    out_ref[...] = h_ref[...]

    def block(kb, carry):
        acc = jnp.dot(pst_ref[kb * kbc], wst_ref[kb * kbc], preferred_element_type=F32)
        for u in range(1, kbc):
            acc += jnp.dot(pst_ref[kb * kbc + u], wst_ref[kb * kbc + u], preferred_element_type=F32)
        out_ref[...] += acc
        return carry

    lax.fori_loop(0, n_blocks, block, 0)


def _combine(base, yg, h, slot_tok, bsz, seq):
    t, dm = h.shape
    cap = yg.shape[2]
    ts, cw, kbc = 256, 256, 4
    nt = seq // ts
    max_chunks = N_EXPERTS * (ts // SLOT_CHUNK + 1)
    max_chunks = -(-max_chunks // kbc) * kbc
    grid_spec = pltpu.PrefetchScalarGridSpec(
        num_scalar_prefetch=1,
        grid=(bsz, dm // cw, nt),
        in_specs=[
            pl.BlockSpec((None, N_EXPERTS, cap, cw), lambda b, c, i, base: (b, 0, 0, c)),
            pl.BlockSpec((ts, cw), lambda b, c, i, base: (b * nt + i, c)),
            pl.BlockSpec((ts, N_EXPERTS), lambda b, c, i, base: (b * nt + i, 0)),
        ],
        out_specs=pl.BlockSpec((ts, cw), lambda b, c, i, base: (b * nt + i, c)),
        scratch_shapes=[pltpu.VMEM((max_chunks, ts, SLOT_CHUNK), BF16),
                        pltpu.VMEM((max_chunks, SLOT_CHUNK, cw), BF16)],
    )
    return pl.pallas_call(
        functools.partial(_combine_body, nt=nt, kbc=kbc),
        grid_spec=grid_spec,
        out_shape=jax.ShapeDtypeStruct((t, dm), F32),
        compiler_params=_params("arbitrary", "arbitrary", "arbitrary"),
        name="moe_combine",
    )(base, yg, h, slot_tok)


def _moe(h, gain, w_router_t, w_gate, w_up, w_down, bsz, seq):
    cap = CAPACITY_FACTOR * seq // N_EXPERTS
    xn, logits_t = _router_logits(h, gain, w_router_t)
    gate_t, pos_t, sel_t = _select_tokens(logits_t, bsz, seq)

    slot_t = jnp.where(sel_t > 0, pos_t, -1)
    ts = 256
    nt = seq // ts
    starts = pos_t.reshape(N_EXPERTS, bsz, nt, ts)[..., 0]
    base = jnp.concatenate([jnp.transpose(starts, (1, 0, 2)),
                            jnp.full((bsz, N_EXPERTS, 1), cap, I32)], axis=-1).reshape(-1)
    gate_tok = gate_t.T
    g_hi = gate_tok.astype(BF16)
    r1 = gate_tok - g_hi.astype(F32)
    g_mid = r1.astype(BF16)
    g_lo = (r1 - g_mid.astype(F32)).astype(BF16)
    pieces = jnp.stack([g_hi, g_mid, g_lo], axis=-1).reshape(-1, 3 * N_EXPERTS)
    pieces = jnp.pad(pieces, ((0, 0), (0, 128 - 3 * N_EXPERTS)))

    xg = _dispatch(base, xn, slot_t, bsz, seq, cap, cw=512, out_dtype=BF16, name="moe_dispatch")
    gate_slots = _dispatch(base, pieces, slot_t, bsz, seq, cap, cw=128, out_dtype=F32, name="moe_gate_dispatch")
    yg = _expert_ffn(xg, w_gate, w_up, w_down, gate_slots)
    return _combine(base, yg, h, slot_t.T, bsz, seq)


def _ple_body(h_ref, g_ref, wg_ref, p_ref, wp_ref, hres_ref, o_ref, xn_ref):
    @pl.when(pl.program_id(1) == 0)
    def _():
        xn_ref[...] = _rms(h_ref[...], g_ref[...]).astype(BF16)

    gate = jax.nn.sigmoid(jnp.dot(xn_ref[...], wg_ref[...], preferred_element_type=F32))
    ple = jnp.dot(p_ref[...].astype(BF16), wp_ref[...], preferred_element_type=F32)
    o_ref[...] = hres_ref[...] + gate * ple


def _ple(h, gain, w_gate, p, w_proj):
    t, dm = h.shape
    tm, tn = 1024, 512
    pd = p.shape[1]
    return pl.pallas_call(
        _ple_body,
        grid=(t // tm, dm // tn),
        in_specs=[
            pl.BlockSpec((tm, dm), lambda i, j: (i, 0)),
            pl.BlockSpec((1, dm), lambda i, j: (0, 0)),
            pl.BlockSpec((dm, tn), lambda i, j: (0, j)),
            pl.BlockSpec((tm, pd), lambda i, j: (i, 0)),
            pl.BlockSpec((pd, tn), lambda i, j: (0, j)),
            pl.BlockSpec((tm, tn), lambda i, j: (i, j)),
        ],
        out_specs=pl.BlockSpec((tm, tn), lambda i, j: (i, j)),
        out_shape=jax.ShapeDtypeStruct((t, dm), F32),
        scratch_shapes=[pltpu.VMEM((tm, dm), BF16)],
        compiler_params=_params("parallel", "arbitrary"),
        name="ple_gate",
    )(h, gain, w_gate, p, w_proj, h)


def _rope_tables(positions):
    inv_freq = jnp.power(ROPE_THETA, -jnp.arange(ROT_HALF, dtype=F32) * 2.0 / (2 * ROT_HALF))
    ang = positions.astype(F32)[..., None] * inv_freq
    cos, sin = jnp.cos(ang), jnp.sin(ang)
    rest = HEAD_DIM - 2 * ROT_HALF
    cos_t = jnp.concatenate([cos, cos, jnp.ones(ang.shape[:-1] + (rest,), F32)], axis=-1)
    sin_t = jnp.concatenate([-sin, sin, jnp.zeros(ang.shape[:-1] + (rest,), F32)], axis=-1)
    return cos_t.reshape(-1, HEAD_DIM), sin_t.reshape(-1, HEAD_DIM)


def _qkv_columns():
    n_heads = HEADS_PER_GROUP * len(DILATIONS)
    cols = []
    for gi in range(len(DILATIONS)):
        for part in range(3):
            start = part * n_heads * HEAD_DIM + gi * GROUP_WIDTH
            cols.append(jnp.arange(start, start + GROUP_WIDTH))
    return jnp.concatenate(cols)


def kernel(x, p, positions, norm_mix, w_in, q_norm, k_norm, w_attn_br, ssm_a_re, ssm_a_im, ssm_log_dt,
           ssm_b_re, ssm_b_im, ssm_c_re, ssm_c_im, ssm_d, w_ssm_br, w_out, norm_ffn, w_router,
           w_exp_gate, w_exp_up, w_exp_down, norm_ple, w_ple_gate, w_ple_proj):
    bsz, seq, dm = x.shape
    depth = w_in.shape[0]
    t = bsz * seq
    n_attn = 3 * HEADS_PER_GROUP * len(DILATIONS) * HEAD_DIM
    ssm_width = ssm_d.shape[1]
    n_groups = ssm_width // SSM_GROUP
    nc = seq // SSM_CHUNK
    n_steps = int(math.log2(nc))
    assert seq % PERM_TILE == 0 and nc == 1 << n_steps

    cos_t, sin_t = _rope_tables(positions)
    qkv_cols = _qkv_columns()
    h = x.reshape(t, dm)

    for l in range(depth):
        w_in_l = w_in[l]
        w_qkv = w_in_l[:, :n_attn][:, qkv_cols].astype(BF16)
        w_u = w_in_l[:, n_attn:n_attn + ssm_width].astype(BF16)
        w_g = w_in_l[:, n_attn + ssm_width:].astype(BF16)
        gain = norm_mix[l][None]
        qk_gain = jnp.stack([q_norm[l], k_norm[l], jnp.ones_like(q_norm[l])])[:, None, :]

        qkv = _qkv_proj(h, gain, w_qkv, qk_gain, cos_t, sin_t)
        u = _norm_matmul(h, gain, w_u, out_dtype=F32, name="u_proj")
        gates = _norm_matmul(h, gain, w_g, out_dtype=BF16, act="sigmoid", name="gate_proj")

        outs, lses = zip(*[_attention(qkv, gi, bsz, seq) for gi in range(len(DILATIONS))])
        gated_attn = _attn_merge_proj(outs, lses, w_attn_br[l].astype(BF16), gates)

        tq, pm, tab = _ssm_tables(ssm_a_re[l], ssm_a_im[l], ssm_log_dt[l], ssm_b_re[l], ssm_b_im[l],
                                  ssm_c_re[l], ssm_c_im[l], n_steps)
        uc = u.reshape(bsz, nc, SSM_CHUNK, n_groups, SSM_GROUP).transpose(0, 3, 1, 2, 4)
        uc = uc.reshape(bsz, n_groups, nc, SSM_CHUNK * SSM_GROUP).astype(BF16)
        yc = _ssm_scan(uc, tq, pm, tab)
        y = yc.reshape(bsz, n_groups, nc, SSM_CHUNK, SSM_GROUP).transpose(0, 2, 3, 1, 4).reshape(t, ssm_width)
        merged = _ssm_glu_merge(y, u, ssm_d[l][None], w_ssm_br[l].astype(BF16), gated_attn, gates)
        h = _resid_matmul(merged, w_out[l].astype(BF16), h)

        h = _moe(h, norm_ffn[l][None], w_router[l].T, w_exp_gate[l].astype(BF16),
                 w_exp_up[l].astype(BF16), w_exp_down[l].astype(BF16), bsz, seq)

        h = _ple(h, norm_ple[l][None], w_ple_gate[l].astype(BF16), p[l].reshape(t, -1),
                 w_ple_proj[l].astype(BF16))
    return h.reshape(bsz, seq, dm)
```

```python
import functools
import math

import jax
import jax.numpy as jnp
from jax import lax
from jax.experimental import pallas as pl
from jax.experimental.pallas import tpu as pltpu

F32 = jnp.float32
BF16 = jnp.bfloat16
I32 = jnp.int32

NORM_EPS = 1e-6
MASK_VALUE = -1e30
ROPE_THETA = 500000.0

LANES = 128
HEAD_DIM = 128
HEADS_PER_GROUP = 4
GROUP_WIDTH = HEADS_PER_GROUP * HEAD_DIM
DILATIONS = (1, 4, 16)
N_SIDE = 64
ROT_HALF = 16

SSM_GROUP = 16
SSM_STATE = 64
SSM_CHUNK = 16

N_EXPERTS = 16
CAPACITY_FACTOR = 2
SLOT_WIN = 128
SLOT_ALIGN = 16
MOE_TILE = 512

PERM_TILE = 1024
VMEM_LIMIT = 56 * 1024 * 1024


def _params(*sem):
    return pltpu.CompilerParams(dimension_semantics=sem, vmem_limit_bytes=VMEM_LIMIT)


def _rms(x, gain):
    var = jnp.mean(x * x, axis=-1, keepdims=True)
    return x * lax.rsqrt(var + NORM_EPS) * gain


def _qkv_body(h_ref, g_ref, w_ref, qkg_ref, cos_ref, sin_ref, o_ref, xn_ref, slab_ref):
    j = pl.program_id(1)
    tm = h_ref.shape[0]

    @pl.when(j == 0)
    def _():
        xn_ref[...] = _rms(h_ref[...], g_ref[...]).astype(BF16)

    acc = jnp.dot(xn_ref[...], w_ref[...], preferred_element_type=F32)
    for hs in range(HEADS_PER_GROUP):
        slab_ref[hs] = acc[:, hs * HEAD_DIM:(hs + 1) * HEAD_DIM]

    @pl.when(j % 3 < 2)
    def _():
        lane = lax.broadcasted_iota(I32, (tm, HEAD_DIM), 1)
        cos = cos_ref[...]
        sin = sin_ref[...]
        gain = qkg_ref[...]
        for hs in range(HEADS_PER_GROUP):
            a = _rms(slab_ref[hs], gain)
            rot = jnp.where(lane < ROT_HALF,
                            pltpu.roll(a, HEAD_DIM - ROT_HALF, 1), pltpu.roll(a, ROT_HALF, 1))
            slab_ref[hs] = a * cos + rot * sin

    for gi, d in enumerate(DILATIONS):
        @pl.when(j // 3 == gi)
        def _(d=d):
            n = tm // d
            for hs in range(HEADS_PER_GROUP):
                for r in range(d):
                    v = slab_ref[hs] if d == 1 else slab_ref[hs, pl.ds(r, n, stride=d), :]
                    o_ref[r * n:(r + 1) * n, hs * HEAD_DIM:(hs + 1) * HEAD_DIM] = v.astype(BF16)


def _qkv_proj(h, gain, w_in, layer, qk_gain, cos, sin):
    t, dm = h.shape
    tm, tn = PERM_TILE, GROUP_WIDTH
    ng = len(DILATIONS)
    nj = 3 * ng
    return pl.pallas_call(
        _qkv_body,
        grid=(t // tm, nj),
        in_specs=[
            pl.BlockSpec((tm, dm), lambda i, j: (i, 0)),
            pl.BlockSpec((1, dm), lambda i, j: (0, 0)),
            pl.BlockSpec((None, dm, tn), lambda i, j: (layer, 0, (j % 3) * ng + j // 3)),
            pl.BlockSpec((None, 1, HEAD_DIM), lambda i, j: (j % 3, 0, 0)),
            pl.BlockSpec((tm, HEAD_DIM), lambda i, j: (i, 0)),
            pl.BlockSpec((tm, HEAD_DIM), lambda i, j: (i, 0)),
        ],
        out_specs=pl.BlockSpec((tm, tn), lambda i, j: (i, j)),
        out_shape=jax.ShapeDtypeStruct((t, nj * tn), BF16),
        scratch_shapes=[pltpu.VMEM((tm, dm), BF16),
                        pltpu.VMEM((HEADS_PER_GROUP, tm, HEAD_DIM), F32)],
        compiler_params=_params("parallel", "arbitrary"),
        name="qkv_proj",
    )(h, gain, w_in, qk_gain, cos, sin)


def _norm_mm_body(h_ref, g_ref, w_ref, o_ref, xn_ref, *, act):
    @pl.when(pl.program_id(1) == 0)
    def _():
        xn_ref[...] = _rms(h_ref[...], g_ref[...]).astype(BF16)

    acc = jnp.dot(xn_ref[...], w_ref[...], preferred_element_type=F32)
    if act == "sigmoid":
        acc = jax.nn.sigmoid(acc)
    o_ref[...] = acc.astype(o_ref.dtype)


def _norm_matmul(h, gain, w, layer, col0, ncols, *, out_dtype, act=None, tn=512, name):
    t, dm = h.shape
    tm = 1024
    nj = ncols // tn
    blk0 = col0 // tn
    return pl.pallas_call(
        functools.partial(_norm_mm_body, act=act),
        grid=(t // tm, nj),
        in_specs=[
            pl.BlockSpec((tm, dm), lambda i, j: (i, 0)),
            pl.BlockSpec((1, dm), lambda i, j: (0, 0)),
            pl.BlockSpec((None, dm, tn), lambda i, j: (layer, 0, blk0 + j)),
        ],
        out_specs=pl.BlockSpec((tm, tn), lambda i, j: (i, j)),
        out_shape=jax.ShapeDtypeStruct((t, nj * tn), out_dtype),
        scratch_shapes=[pltpu.VMEM((tm, dm), BF16)],
        compiler_params=_params("parallel", "arbitrary"),
        name=name,
    )(h, gain, w)


def _attn_body(q_ref, kp_ref, km_ref, kn_ref, vp_ref, vm_ref, vn_ref, o_ref, l_ref,
               kw_ref, vw_ref, os_ref, ls_ref, *, tq, sub_len):
    c = pl.program_id(2)
    kw_ref[0:N_SIDE] = kp_ref[...]
    kw_ref[N_SIDE:N_SIDE + tq] = km_ref[...].reshape(tq, GROUP_WIDTH)
    kw_ref[N_SIDE + tq:] = kn_ref[...]
    vw_ref[0:N_SIDE] = vp_ref[...]
    vw_ref[N_SIDE:N_SIDE + tq] = vm_ref[...].reshape(tq, GROUP_WIDTH)
    vw_ref[N_SIDE + tq:] = vn_ref[...]
    q = q_ref[...].reshape(tq, GROUP_WIDTH)

    sb = min(128, tq)
    nk = sb + 2 * N_SIDE
    scale = HEAD_DIM ** -0.5
    row = lax.broadcasted_iota(I32, (sb, nk), 0)
    col = lax.broadcasted_iota(I32, (sb, nk), 1)
    band = jnp.abs(col - row - N_SIDE) <= N_SIDE
    for i in range(tq // sb):
        kpos = c * tq + (i * sb - N_SIDE) + col
        valid = band & (kpos >= 0) & (kpos < sub_len)
        for hs in range(HEADS_PER_GROUP):
            lanes = slice(hs * HEAD_DIM, (hs + 1) * HEAD_DIM)
            qh = q[i * sb:(i + 1) * sb, lanes]
            kh = kw_ref[i * sb:i * sb + nk, lanes]
            vh = vw_ref[i * sb:i * sb + nk, lanes]
            s = lax.dot_general(qh, kh, (((1,), (1,)), ((), ())), preferred_element_type=F32) * scale
            s = jnp.where(valid, s, MASK_VALUE)
            m = jnp.max(s, axis=-1, keepdims=True)
            e = jnp.exp(s - m)
            den = jnp.sum(e, axis=-1, keepdims=True)
            o = jnp.dot((e / den).astype(BF16), vh, preferred_element_type=F32)
            os_ref[i * sb:(i + 1) * sb, lanes] = o
            ls_ref[i * sb:(i + 1) * sb, lanes] = jnp.broadcast_to(m + jnp.log(den), (sb, HEAD_DIM))
    o_ref[...] = os_ref[...].reshape(o_ref.shape)
    l_ref[...] = ls_ref[...].reshape(l_ref.shape)


def _attention(qkv, gi, bsz, seq):
    d = DILATIONS[gi]
    t = bsz * seq
    sub_len = seq // d
    nbt = PERM_TILE // (N_SIDE * d)
    ntile = seq // PERM_TILE
    tq = min(512, sub_len)
    nbq = tq // N_SIDE
    nblk = sub_len // N_SIDE
    view = (bsz, ntile, d, nbt, N_SIDE, qkv.shape[1])
    oview = (bsz, ntile, d, nbt, N_SIDE, GROUP_WIDTH)
    gw = GROUP_WIDTH

    if nbt >= nbq:
        per = nbt // nbq
        main_shape = (None, None, None, nbq, N_SIDE, gw)

        def main_idx(col):
            return lambda b, r, c: (b, c // per, r, c % per, 0, col)
    else:
        main_shape = (None, nbq // nbt, None, nbt, N_SIDE, gw)

        def main_idx(col):
            return lambda b, r, c: (b, c, r, 0, 0, col)

    halo_shape = (None, None, None, None, N_SIDE, gw)

    def prev_idx(col):
        def f(b, r, c):
            n = jnp.maximum(c * nbq - 1, 0)
            return (b, n // nbt, r, n % nbt, 0, col)
        return f

    def next_idx(col):
        def f(b, r, c):
            n = jnp.minimum((c + 1) * nbq, nblk - 1)
            return (b, n // nbt, r, n % nbt, 0, col)
        return f

    qc, kc, vc = 3 * gi, 3 * gi + 1, 3 * gi + 2
    x = qkv.reshape(view)
    o, l = pl.pallas_call(
        functools.partial(_attn_body, tq=tq, sub_len=sub_len),
        grid=(bsz, d, sub_len // tq),
        in_specs=[
            pl.BlockSpec(main_shape, main_idx(qc)),
            pl.BlockSpec(halo_shape, prev_idx(kc)),
            pl.BlockSpec(main_shape, main_idx(kc)),
            pl.BlockSpec(halo_shape, next_idx(kc)),
            pl.BlockSpec(halo_shape, prev_idx(vc)),
            pl.BlockSpec(main_shape, main_idx(vc)),
            pl.BlockSpec(halo_shape, next_idx(vc)),
        ],
        out_specs=[pl.BlockSpec(main_shape, main_idx(0)), pl.BlockSpec(main_shape, main_idx(0))],
        out_shape=[jax.ShapeDtypeStruct(oview, F32), jax.ShapeDtypeStruct(oview, F32)],
        scratch_shapes=[pltpu.VMEM((tq + 2 * N_SIDE, gw), BF16), pltpu.VMEM((tq + 2 * N_SIDE, gw), BF16),
                        pltpu.VMEM((tq, gw), F32), pltpu.VMEM((tq, gw), F32)],
        compiler_params=_params("parallel", "parallel", "arbitrary"),
        name=f"dilated_attn_d{d}",
    )(x, x, x, x, x, x, x)
    return o.reshape(t, gw), l.reshape(t, gw)


def _attn_merge_body(o0, l0, o1, l1, o2, l2, w_ref, g_ref, out_ref, comb_ref, so_ref, sl_ref):
    tm = out_ref.shape[0]

    @pl.when(pl.program_id(1) == 0)
    def _():
        for gi, (o_ref, l_ref) in enumerate(((o0, l0), (o1, l1), (o2, l2))):
            d = DILATIONS[gi]
            n = tm // d
            for hs in range(HEADS_PER_GROUP):
                lanes = slice(hs * HEAD_DIM, (hs + 1) * HEAD_DIM)
                for r in range(d):
                    ov = o_ref[r * n:(r + 1) * n, lanes]
                    lv = l_ref[r * n:(r + 1) * n, lanes]
                    if d == 1:
                        so_ref[gi, hs] = ov
                        sl_ref[gi, hs] = lv
                    else:
                        so_ref[gi, hs, pl.ds(r, n, stride=d), :] = ov
                        sl_ref[gi, hs, pl.ds(r, n, stride=d), :] = lv
        for hs in range(HEADS_PER_GROUP):
            ls = [sl_ref[gi, hs] for gi in range(3)]
            mx = jnp.maximum(jnp.maximum(ls[0], ls[1]), ls[2])
            ws = [jnp.exp(l - mx) for l in ls]
            num = ws[0] * so_ref[0, hs] + ws[1] * so_ref[1, hs] + ws[2] * so_ref[2, hs]
            comb = num / (ws[0] + ws[1] + ws[2])
            comb_ref[:, hs * HEAD_DIM:(hs + 1) * HEAD_DIM] = comb.astype(BF16)

    acc = jnp.dot(comb_ref[...], w_ref[...], preferred_element_type=F32)
    out_ref[...] = (acc * g_ref[...].astype(F32)).astype(out_ref.dtype)


def _attn_merge_proj(outs, lses, w_attn, layer, gates):
    t = outs[0].shape[0]
    dm = w_attn.shape[2]
    tm, tn = PERM_TILE, 512
    row = pl.BlockSpec((tm, GROUP_WIDTH), lambda i, j: (i, 0))
    return pl.pallas_call(
        _attn_merge_body,
        grid=(t // tm, dm // tn),
        in_specs=[row, row, row, row, row, row,
                  pl.BlockSpec((None, GROUP_WIDTH, tn), lambda i, j: (layer, 0, j)),
                  pl.BlockSpec((tm, tn), lambda i, j: (i, j))],
        out_specs=pl.BlockSpec((tm, tn), lambda i, j: (i, j)),
        out_shape=jax.ShapeDtypeStruct((t, dm), BF16),
        scratch_shapes=[pltpu.VMEM((tm, GROUP_WIDTH), BF16),
                        pltpu.VMEM((3, HEADS_PER_GROUP, tm, HEAD_DIM), F32),
                        pltpu.VMEM((3, HEADS_PER_GROUP, tm, HEAD_DIM), F32)],
        compiler_params=_params("parallel", "arbitrary"),
        name="attn_merge_proj",
    )(outs[0], lses[0], outs[1], lses[1], outs[2], lses[2], w_attn, gates)


def _ssm_tables(a_re, a_im, log_dt, b_re, b_im, c_re, c_im, n_steps):
    hi = lax.Precision.HIGHEST
    lc = SSM_CHUNK
    a_re, a_im = a_re.astype(F32), a_im.astype(F32)
    dt = jnp.exp(log_dt.astype(F32))[..., None]

    def lam_pow(n):
        n = n.astype(F32)[:, None, None, None]
        mag = jnp.exp(a_re * dt * n)
        ang = a_im * dt * n
        return mag * jnp.cos(ang), mag * jnp.sin(ang)

    one_re, one_im = lam_pow(jnp.ones((1,)))
    lb_re, lb_im = one_re[0], one_im[0]
    den = a_re * a_re + a_im * a_im
    nr, ni = lb_re - 1.0, lb_im
    coef_re = ((nr * a_re + ni * a_im) / den)[..., None]
    coef_im = ((ni * a_re - nr * a_im) / den)[..., None]
    b_re, b_im = b_re.astype(F32), b_im.astype(F32)
    bb_re = coef_re * b_re - coef_im * b_im
    bb_im = coef_re * b_im + coef_im * b_re
    c_re, c_im = c_re.astype(F32), c_im.astype(F32)

    pw_re, pw_im = lam_pow(jnp.arange(lc + 1))
    e_re = pw_re[..., None] * bb_re - pw_im[..., None] * bb_im
    e_im = pw_re[..., None] * bb_im + pw_im[..., None] * bb_re
    kern = (jnp.einsum('dgop,tdgpi->tdgoi', c_re, e_re, precision=hi)
            - jnp.einsum('dgop,tdgpi->tdgoi', c_im, e_im, precision=hi))
    s_idx = jnp.arange(lc)[:, None]
    t_idx = jnp.arange(lc)[None, :]
    lag_f = jnp.clip(t_idx - s_idx, 0, lc)
    lag_b = jnp.clip(s_idx - t_idx, 0, lc)
    kf = jnp.where((t_idx >= s_idx)[..., None, None, None], kern[lag_f, 0], 0.0)
    kb = jnp.where((s_idx >= t_idx)[..., None, None, None], kern[lag_b, 1], 0.0)
    toep = jnp.transpose(kf + kb, (2, 0, 4, 1, 3))
    g = toep.shape[0]
    toep = toep.reshape(g, lc * SSM_GROUP, lc * SSM_GROUP)

    def state_in(e, direction, taus):
        return jnp.transpose(e[taus, direction], (1, 0, 3, 2)).reshape(g, lc * SSM_GROUP, SSM_STATE)

    tau_f = lc - 1 - jnp.arange(lc)
    tau_b = jnp.arange(lc)
    q_in = jnp.concatenate([state_in(e_re, 0, tau_f), state_in(e_im, 0, tau_f),
                            state_in(e_re, 1, tau_b), state_in(e_im, 1, tau_b)], axis=-1)
    tq = jnp.concatenate([toep, q_in], axis=-1)

    def state_out(direction, taus):
        lr = pw_re[taus, direction][:, :, None, :]
        li = pw_im[taus, direction][:, :, None, :]
        cr, ci = c_re[direction][None], c_im[direction][None]
        mr = cr * lr - ci * li
        mi = cr * li + ci * lr
        to_rows = lambda m: jnp.transpose(m, (1, 3, 0, 2)).reshape(g, SSM_STATE, lc * SSM_GROUP)
        return jnp.concatenate([to_rows(mr), to_rows(-mi)], axis=1)

    pm = jnp.concatenate([state_out(0, jnp.arange(lc) + 1), state_out(1, lc - jnp.arange(lc))], axis=1)

    sc_re, sc_im = lam_pow(lc * (2 ** jnp.arange(n_steps)))
    a1 = jnp.concatenate([sc_re, sc_re], axis=-1)
    a2 = jnp.concatenate([-sc_im, sc_im], axis=-1)
    tab = jnp.stack([a1, a2], axis=2)
    tab = jnp.transpose(tab, (3, 1, 0, 2, 4)).reshape(g, 4 * n_steps, 2 * SSM_STATE)
    return tq.astype(BF16), pm.astype(BF16), tab


def _ssm_expand(tq, pm, tab):
    g = tq.shape[0]
    gps = LANES // SSM_GROUP
    ns = g // gps
    lc, hg = SSM_CHUNK, SSM_GROUP
    kw = lc * hg
    side = lc * LANES
    eye = jnp.eye(gps, dtype=tq.dtype)
    t6 = tq[..., :kw].reshape(ns, gps, lc, hg, lc, hg)
    toe = jnp.einsum('Ggshto,gk->Gsghtko', t6, eye).reshape(ns, side, side)
    q6 = tq[..., kw:].reshape(ns, gps, lc, hg, 2, 2 * SSM_STATE)
    sin = jnp.einsum('Ggshdl,gk->Gsghdkl', q6, eye).reshape(ns, side, 2 * gps * 2 * SSM_STATE)
    p6 = pm.reshape(ns, gps, 2, 2 * SSM_STATE, lc, hg)
    sout = jnp.einsum('Ggdlto,gk->Gdgltko', p6, eye).reshape(ns, 2 * gps * 2 * SSM_STATE, side)
    rows = tab.shape[1] // 2
    tabs = tab.reshape(ns, gps, 2, rows, 2 * SSM_STATE).transpose(0, 2, 1, 3, 4)
    return toe, sin, sout, tabs.reshape(ns, 2 * gps * rows, 2 * SSM_STATE)


def _ssm_body(u_ref, toe_ref, sin_ref, sout_ref, tab_ref, y_ref, lhs_ref, inj_ref, st_ref, *, n_steps):
    seq = u_ref.shape[0]
    lc = SSM_CHUNK
    nc = seq // lc
    n_slabs = inj_ref.shape[0]
    per_dir = n_slabs // 2
    row = lax.broadcasted_iota(I32, (nc, LANES), 0)

    for s in range(lc):
        lhs_ref[:, s * LANES:(s + 1) * LANES] = u_ref[pl.ds(s, nc, stride=lc), :].astype(BF16)
    lhs = lhs_ref[...]

    for j in range(0, n_slabs, 2):
        r = jnp.dot(lhs, sin_ref[:, j * LANES:(j + 2) * LANES], preferred_element_type=F32)
        inj_ref[j] = r[:, :LANES]
        inj_ref[j + 1] = r[:, LANES:]

    def shifted(x, sh, direction):
        if direction == 0:
            return jnp.where(row >= sh, pltpu.roll(x, sh, 0), 0.0)
        return jnp.where(row < nc - sh, pltpu.roll(x, nc - sh, 0), 0.0)

    for direction in range(2):
        def scan_slab(j, carry, direction=direction):
            x = inj_ref[j]
            for k in range(n_steps):
                base = (j * n_steps + k) * 2
                a1 = tab_ref[pl.ds(base, 1), :]
                a2 = tab_ref[pl.ds(base + 1, 1), :]
                xs = shifted(x, 1 << k, direction)
                x = x + a1 * xs + a2 * pltpu.roll(xs, SSM_STATE, 1)
            st_ref[j] = shifted(x, 1, direction).astype(BF16)
            return carry

        lax.fori_loop(direction * per_dir, (direction + 1) * per_dir, scan_slab, 0)

    states = jnp.concatenate([st_ref[j] for j in range(n_slabs)], axis=1)
    for t in range(0, lc, 2):
        cols = slice(t * LANES, (t + 2) * LANES)
        r = (jnp.dot(lhs, toe_ref[:, cols], preferred_element_type=F32)
             + jnp.dot(states, sout_ref[:, cols], preferred_element_type=F32))
        y_ref[pl.ds(t, nc, stride=lc), :] = r[:, :LANES]
        y_ref[pl.ds(t + 1, nc, stride=lc), :] = r[:, LANES:]


def _ssm_scan(u, toe, sin, sout, tab, layer, bsz, seq):
    t, width = u.shape
    ns = width // LANES
    nc = seq // SSM_CHUNK
    n_steps = int(math.log2(nc))
    side = SSM_CHUNK * LANES
    n_slabs = sin.shape[-1] // LANES
    once = pl.Buffered(1)
    return pl.pallas_call(
        functools.partial(_ssm_body, n_steps=n_steps),
        grid=(ns, bsz),
        in_specs=[
            pl.BlockSpec((seq, LANES), lambda g, b: (b, g)),
            pl.BlockSpec((None, None, side, side), lambda g, b: (layer, g, 0, 0), pipeline_mode=once),
            pl.BlockSpec((None, None, side, sin.shape[-1]), lambda g, b: (layer, g, 0, 0), pipeline_mode=once),
            pl.BlockSpec((None, None, sout.shape[-2], side), lambda g, b: (layer, g, 0, 0), pipeline_mode=once),
            pl.BlockSpec((None, None, tab.shape[-2], LANES), lambda g, b: (layer, g, 0, 0)),
        ],
        out_specs=pl.BlockSpec((seq, LANES), lambda g, b: (b, g)),
        out_shape=jax.ShapeDtypeStruct((t, width), F32),
        scratch_shapes=[pltpu.VMEM((nc, side), BF16),
                        pltpu.VMEM((n_slabs, nc, LANES), F32),
                        pltpu.VMEM((n_slabs, nc, LANES), BF16)],
        compiler_params=_params("arbitrary", "arbitrary"),
        name="s5_chunk_scan",
    )(u, toe, sin, sout, tab)


def _ssm_glu_body(y_ref, u_ref, d_ref, wa_ref, wb_ref, ga_ref, gs_ref, o_ref, act_ref):
    @pl.when(pl.program_id(1) == 0)
    def _():
        act_ref[...] = jax.nn.gelu(y_ref[...] + d_ref[...] * u_ref[...]).astype(BF16)

    act = act_ref[...]
    za = jnp.dot(act, wa_ref[...], preferred_element_type=F32)
    zb = jnp.dot(act, wb_ref[...], preferred_element_type=F32)
    s_branch = za * jax.nn.sigmoid(zb)
    o_ref[...] = (ga_ref[...].astype(F32) + gs_ref[...].astype(F32) * s_branch).astype(o_ref.dtype)


def _ssm_glu_merge(y, u, d_skip, w_ssm, layer, gated_attn, gates):
    t, width = y.shape
    dm = gated_attn.shape[1]
    tm, tn = 1024, 512
    nj = dm // tn
    return pl.pallas_call(
        _ssm_glu_body,
        grid=(t // tm, nj),
        in_specs=[
            pl.BlockSpec((tm, width), lambda i, j: (i, 0)),
            pl.BlockSpec((tm, width), lambda i, j: (i, 0)),
            pl.BlockSpec((1, width), lambda i, j: (0, 0)),
            pl.BlockSpec((None, width, tn), lambda i, j: (layer, 0, j)),
            pl.BlockSpec((None, width, tn), lambda i, j: (layer, 0, j + nj)),
            pl.BlockSpec((tm, tn), lambda i, j: (i, j)),
            pl.BlockSpec((tm, tn), lambda i, j: (i, j + nj)),
        ],
        out_specs=pl.BlockSpec((tm, tn), lambda i, j: (i, j)),
        out_shape=jax.ShapeDtypeStruct((t, dm), BF16),
        scratch_shapes=[pltpu.VMEM((tm, width), BF16)],
        compiler_params=_params("parallel", "arbitrary"),
        name="s5_glu_merge",
    )(y, u, d_skip, w_ssm, w_ssm, gated_attn, gates)


def _resid_mm_body(x_ref, w_ref, h_ref, o_ref):
    o_ref[...] = h_ref[...] + jnp.dot(x_ref[...], w_ref[...], preferred_element_type=F32)


def _resid_matmul(x, w, layer, h):
    t, k = x.shape
    dm = w.shape[2]
    tm, tn = 1024, 512
    return pl.pallas_call(
        _resid_mm_body,
        grid=(t // tm, dm // tn),
        in_specs=[
            pl.BlockSpec((tm, k), lambda i, j: (i, 0)),
            pl.BlockSpec((None, k, tn), lambda i, j: (layer, 0, j)),
            pl.BlockSpec((tm, tn), lambda i, j: (i, j)),
        ],
        out_specs=pl.BlockSpec((tm, tn), lambda i, j: (i, j)),
        out_shape=jax.ShapeDtypeStruct((t, dm), F32),
        compiler_params=_params("parallel", "arbitrary"),
        name="out_proj",
    )(x, w, h)


def _router_body(h_ref, g_ref, wr_ref, xn_ref, lg_ref):
    xn = _rms(h_ref[...], g_ref[...])
    xn_ref[...] = xn.astype(BF16)
    lg_ref[...] = lax.dot_general(wr_ref[...], xn, (((1,), (1,)), ((), ())),
                                  precision=lax.Precision.HIGHEST, preferred_element_type=F32)


def _router_logits(h, gain, w_router_t):
    t, dm = h.shape
    tm = 512
    return pl.pallas_call(
        _router_body,
        grid=(t // tm,),
        in_specs=[
            pl.BlockSpec((tm, dm), lambda i: (i, 0)),
            pl.BlockSpec((1, dm), lambda i: (0, 0)),
            pl.BlockSpec((N_EXPERTS, dm), lambda i: (0, 0)),
        ],
        out_specs=[pl.BlockSpec((tm, dm), lambda i: (i, 0)),
                   pl.BlockSpec((N_EXPERTS, tm), lambda i: (0, i))],
        out_shape=[jax.ShapeDtypeStruct((t, dm), BF16), jax.ShapeDtypeStruct((N_EXPERTS, t), F32)],
        compiler_params=_params("parallel"),
        name="router_logits",
    )(h, gain, w_router_t)


def _select_body(lg_ref, gate_ref, pos_ref, sel_ref, *, cap):
    lg = lg_ref[...]
    ne, s = lg.shape
    m = jnp.max(lg, axis=0, keepdims=True)
    e = jnp.exp(lg - m)
    aff = e / jnp.sum(e, axis=0, keepdims=True)
    bits = lax.bitcast_convert_type(aff, I32)

    def count(ind):
        return jnp.sum(ind, axis=1, keepdims=True)

    def value_bit(k, thr):
        cand = thr | jnp.left_shift(jnp.int32(1), 30 - k)
        return jnp.where(count(jnp.where(bits >= cand, 1.0, 0.0)) >= cap, cand, thr)

    thr = lax.fori_loop(0, 31, value_bit, jnp.zeros((ne, 1), I32))
    above = jnp.where(bits > thr, 1.0, 0.0)
    tie = jnp.where(bits == thr, 1.0, 0.0)
    need = cap - count(above)
    idx = lax.broadcasted_iota(I32, (ne, s), 1)
    n_bits = int(math.log2(s)) + 1

    def index_bit(k, bound):
        cand = bound | jnp.left_shift(jnp.int32(1), n_bits - 1 - k)
        below = count(jnp.where(idx < cand, tie, 0.0))
        return jnp.where(below < need, cand, bound)

    bound = lax.fori_loop(0, n_bits, index_bit, jnp.zeros((ne, 1), I32))
    sel = above + jnp.where(idx <= bound, tie, 0.0)
    gate_ref[...] = sel * aff
    sel_ref[...] = sel.astype(I32)

    blk = 256
    tri = jnp.where(lax.broadcasted_iota(I32, (blk, blk), 0) <= lax.broadcasted_iota(I32, (blk, blk), 1),
                    1.0, 0.0).astype(BF16)
    carry = jnp.zeros((ne, 1), F32)
    for j in range(s // blk):
        seg = sel[:, j * blk:(j + 1) * blk]
        inc = jnp.dot(seg.astype(BF16), tri, preferred_element_type=F32)
        pos_ref[:, j * blk:(j + 1) * blk] = (inc - seg + carry).astype(I32)
        carry = carry + inc[:, blk - 1:blk]


def _select_tokens(logits_t, bsz, seq):
    cap = CAPACITY_FACTOR * seq // N_EXPERTS
    t = bsz * seq
    spec = pl.BlockSpec((N_EXPERTS, seq), lambda b: (0, b))
    return pl.pallas_call(
        functools.partial(_select_body, cap=cap),
        grid=(bsz,),
        in_specs=[spec],
        out_specs=[spec, spec, spec],
        out_shape=[jax.ShapeDtypeStruct((N_EXPERTS, t), F32), jax.ShapeDtypeStruct((N_EXPERTS, t), I32),
                   jax.ShapeDtypeStruct((N_EXPERTS, t), I32)],
        compiler_params=_params("parallel"),
        name="expert_choice_select",
    )(logits_t)


def _slot_window(base_ref, b, e, i, nt, cap):
    k = (b * N_EXPERTS + e) * (nt + 1) + i
    lo = base_ref[k]
    start = jnp.minimum((lo // SLOT_ALIGN) * SLOT_ALIGN, cap - SLOT_WIN)
    return pl.multiple_of(start, SLOT_ALIGN), base_ref[k + 1]


def _n_extra_windows(start, hi):
    return (jnp.maximum(hi - (start + SLOT_WIN), 0) + SLOT_WIN - 1) // SLOT_WIN


def _extra_window(start, k, cap):
    first = start + SLOT_WIN * k
    return first, pl.multiple_of(jnp.minimum(first, cap - SLOT_WIN), SLOT_ALIGN)


def _dispatch_body(base_ref, x_ref, slot_ref, out_ref, *, nt, cap):
    b = pl.program_id(0)
    i = pl.program_id(2)
    ts = x_ref.shape[0]

    @pl.when(i == 0)
    def _():
        out_ref[...] = jnp.zeros_like(out_ref)

    x = x_ref[...]
    wio = lax.broadcasted_iota(I32, (SLOT_WIN, ts), 0)
    wins = [_slot_window(base_ref, b, e, i, nt, cap) for e in range(N_EXPERTS)]
    onehot = jnp.concatenate(
        [jnp.where(slot_ref[e:e + 1, :] - wins[e][0] == wio, 1.0, 0.0).astype(x.dtype) for e in range(N_EXPERTS)],
        axis=0)
    res = jnp.dot(onehot, x, preferred_element_type=F32)
    for e in range(N_EXPERTS):
        out_ref[e, pl.ds(wins[e][0], SLOT_WIN), :] += res[e * SLOT_WIN:(e + 1) * SLOT_WIN].astype(out_ref.dtype)

    for e in range(N_EXPERTS):
        start, hi = wins[e]

        def extra(k, carry, e=e, start=start):
            first, st = _extra_window(start, k, cap)
            srow = slot_ref[e:e + 1, :]
            srow = jnp.where(srow >= first, srow, -1)
            oh = jnp.where(srow - st == wio, 1.0, 0.0).astype(x.dtype)
            out_ref[e, pl.ds(st, SLOT_WIN), :] += jnp.dot(oh, x, preferred_element_type=F32).astype(out_ref.dtype)
            return carry

        lax.fori_loop(1, 1 + _n_extra_windows(start, hi), extra, 0)


def _dispatch(base, x, slot_t, bsz, seq, cap, *, cw, out_dtype, name):
    t, width = x.shape
    ts = MOE_TILE
    nt = seq // ts
    grid_spec = pltpu.PrefetchScalarGridSpec(
        num_scalar_prefetch=1,
        grid=(bsz, width // cw, nt),
        in_specs=[
            pl.BlockSpec((ts, cw), lambda b, c, i, base: (b * nt + i, c)),
            pl.BlockSpec((N_EXPERTS, ts), lambda b, c, i, base: (0, b * nt + i)),
        ],
        out_specs=pl.BlockSpec((None, N_EXPERTS, cap, cw), lambda b, c, i, base: (b, 0, 0, c)),
    )
    return pl.pallas_call(
        functools.partial(_dispatch_body, nt=nt, cap=cap),
        grid_spec=grid_spec,
        out_shape=jax.ShapeDtypeStruct((bsz, N_EXPERTS, cap, width), out_dtype),
        compiler_params=_params("parallel", "parallel", "arbitrary"),
        name=name,
    )(base, x, slot_t)


def _expert_body(x_ref, wg_ref, wu_ref, wd_ref, gs_ref, y_ref, acc_ref):
    e = pl.program_id(0)
    f = pl.program_id(2)
    x = x_ref[...]
    hid = (jax.nn.silu(jnp.dot(x, wg_ref[...], preferred_element_type=F32))
           * jnp.dot(x, wu_ref[...], preferred_element_type=F32)).astype(BF16)
    part = jnp.dot(hid, wd_ref[...], preferred_element_type=F32)

    @pl.when(f == 0)
    def _():
        acc_ref[...] = part

    @pl.when(f > 0)
    def _():
        acc_ref[...] += part

    @pl.when(f == pl.num_programs(2) - 1)
    def _():
        pieces = gs_ref[...]
        lane = lax.broadcasted_iota(I32, pieces.shape, 1)
        mine = (lane >= 3 * e) & (lane < 3 * e + 3)
        gate = jnp.sum(jnp.where(mine, pieces, 0.0), axis=1, keepdims=True)
        y_ref[...] = (acc_ref[...] * gate).astype(y_ref.dtype)


def _expert_ffn(xg, w_gate, w_up, w_down, layer, gate_slots):
    bsz, ne, cap, dm = xg.shape
    ff = w_gate.shape[3]
    tf = 512
    return pl.pallas_call(
        _expert_body,
        grid=(ne, bsz, ff // tf),
        in_specs=[
            pl.BlockSpec((None, None, cap, dm), lambda e, b, f: (b, e, 0, 0)),
            pl.BlockSpec((None, None, dm, tf), lambda e, b, f: (layer, e, 0, f)),
            pl.BlockSpec((None, None, dm, tf), lambda e, b, f: (layer, e, 0, f)),
            pl.BlockSpec((None, None, tf, dm), lambda e, b, f: (layer, e, f, 0)),
            pl.BlockSpec((None, None, cap, 128), lambda e, b, f: (b, e, 0, 0)),
        ],
        out_specs=pl.BlockSpec((None, None, cap, dm), lambda e, b, f: (b, e, 0, 0)),
        out_shape=jax.ShapeDtypeStruct((bsz, ne, cap, dm), BF16),
        scratch_shapes=[pltpu.VMEM((cap, dm), F32)],
        compiler_params=_params("parallel", "parallel", "arbitrary"),
        name="expert_swiglu",
    )(xg, w_gate, w_up, w_down, gate_slots)


def _combine_body(base_ref, yg_hbm, h_ref, slot_ref, out_ref, wbuf, xbuf, wsem, xsem, *, nt, cap, n_steps):
    n = pl.program_id(0)
    ts = h_ref.shape[0]

    def window_copies(step, half):
        b = step // nt
        i = step % nt
        copies = []
        for e in range(N_EXPERTS):
            start, _ = _slot_window(base_ref, b, e, i, nt, cap)
            copies.append(pltpu.make_async_copy(yg_hbm.at[b, e, pl.ds(start, SLOT_WIN), :],
                                                wbuf.at[half, e], wsem.at[half]))
        return copies

    @pl.when(n == 0)
    def _():
        for c in window_copies(0, 0):
            c.start()

    @pl.when(n + 1 < n_steps)
    def _():
        for c in window_copies(n + 1, (n + 1) % 2):
            c.start()

    b = n // nt
    i = n % nt
    half = n % 2
    lane = lax.broadcasted_iota(I32, (ts, SLOT_WIN), 1)
    wins = [_slot_window(base_ref, b, e, i, nt, cap) for e in range(N_EXPERTS)]
    onehot = jnp.concatenate(
        [jnp.where(slot_ref[:, e:e + 1] - wins[e][0] == lane, 1.0, 0.0).astype(BF16) for e in range(N_EXPERTS)],
        axis=1)
    for c in window_copies(n, half):
        c.wait()
    rows = wbuf[half].reshape(N_EXPERTS * SLOT_WIN, wbuf.shape[-1])
    out_ref[...] = h_ref[...] + jnp.dot(onehot, rows, preferred_element_type=F32)

    for e in range(N_EXPERTS):
        start, hi = wins[e]

        def extra(k, carry, e=e, start=start):
            first, st = _extra_window(start, k, cap)
            copy = pltpu.make_async_copy(yg_hbm.at[b, e, pl.ds(st, SLOT_WIN), :], xbuf, xsem)
            copy.start()
            scol = slot_ref[:, e:e + 1]
            scol = jnp.where(scol >= first, scol, -1)
            oh = jnp.where(scol - st == lane, 1.0, 0.0).astype(BF16)
            copy.wait()
            out_ref[...] += jnp.dot(oh, xbuf[...], preferred_element_type=F32)
            return carry

        lax.fori_loop(1, 1 + _n_extra_windows(start, hi), extra, 0)


def _combine(base, yg, h, slot_tok, bsz, seq):
    t, dm = h.shape
    cap = yg.shape[2]
    ts = MOE_TILE
    nt = seq // ts
    n_steps = bsz * nt
    grid_spec = pltpu.PrefetchScalarGridSpec(
        num_scalar_prefetch=1,
        grid=(n_steps,),
        in_specs=[
            pl.BlockSpec(memory_space=pl.ANY),
            pl.BlockSpec((ts, dm), lambda n, base: (n, 0)),
            pl.BlockSpec((ts, N_EXPERTS), lambda n, base: (n, 0)),
        ],
        out_specs=pl.BlockSpec((ts, dm), lambda n, base: (n, 0)),
        scratch_shapes=[pltpu.VMEM((2, N_EXPERTS, SLOT_WIN, dm), BF16),
                        pltpu.VMEM((SLOT_WIN, dm), BF16),
                        pltpu.SemaphoreType.DMA((2,)),
                        pltpu.SemaphoreType.DMA(())],
    )
    return pl.pallas_call(
        functools.partial(_combine_body, nt=nt, cap=cap, n_steps=n_steps),
        grid_spec=grid_spec,
        out_shape=jax.ShapeDtypeStruct((t, dm), F32),
        compiler_params=_params("arbitrary"),
        name="moe_combine",
    )(base, yg, h, slot_tok)


def _moe(h, gain, w_router_t, w_gate, w_up, w_down, layer, bsz, seq):
    cap = CAPACITY_FACTOR * seq // N_EXPERTS
    xn, logits_t = _router_logits(h, gain, w_router_t)
    gate_t, pos_t, sel_t = _select_tokens(logits_t, bsz, seq)

    slot_t = jnp.where(sel_t > 0, pos_t, -1)
    ts = MOE_TILE
    nt = seq // ts
    starts = pos_t.reshape(N_EXPERTS, bsz, nt, ts)[..., 0]
    base = jnp.concatenate([jnp.transpose(starts, (1, 0, 2)),
                            jnp.full((bsz, N_EXPERTS, 1), cap, I32)], axis=-1).reshape(-1)
    gate_tok = gate_t.T
    g_hi = gate_tok.astype(BF16)
    r1 = gate_tok - g_hi.astype(F32)
    g_mid = r1.astype(BF16)
    g_lo = (r1 - g_mid.astype(F32)).astype(BF16)
    pieces = jnp.stack([g_hi, g_mid, g_lo], axis=-1).reshape(-1, 3 * N_EXPERTS)
    pieces = jnp.pad(pieces, ((0, 0), (0, 128 - 3 * N_EXPERTS)))

    xg = _dispatch(base, xn, slot_t, bsz, seq, cap, cw=512, out_dtype=BF16, name="moe_dispatch")
    gate_slots = _dispatch(base, pieces, slot_t, bsz, seq, cap, cw=128, out_dtype=F32, name="moe_gate_dispatch")
    yg = _expert_ffn(xg, w_gate, w_up, w_down, layer, gate_slots)
    return _combine(base, yg, h, slot_t.T, bsz, seq)


def _ple_body(h_ref, g_ref, wg_ref, p_ref, wp_ref, hres_ref, o_ref, xn_ref):
    @pl.when(pl.program_id(1) == 0)
    def _():
        xn_ref[...] = _rms(h_ref[...], g_ref[...]).astype(BF16)

    gate = jax.nn.sigmoid(jnp.dot(xn_ref[...], wg_ref[...], preferred_element_type=F32))
    ple = jnp.dot(p_ref[...].astype(BF16), wp_ref[...], preferred_element_type=F32)
    o_ref[...] = hres_ref[...] + gate * ple


def _ple(h, gain, w_gate, p, w_proj, layer):
    t, dm = h.shape
    tm, tn = 1024, 512
    pd = p.shape[2]
    return pl.pallas_call(
        _ple_body,
        grid=(t // tm, dm // tn),
        in_specs=[
            pl.BlockSpec((tm, dm), lambda i, j: (i, 0)),
            pl.BlockSpec((1, dm), lambda i, j: (0, 0)),
            pl.BlockSpec((None, dm, tn), lambda i, j: (layer, 0, j)),
            pl.BlockSpec((None, tm, pd), lambda i, j: (layer, i, 0)),
            pl.BlockSpec((None, pd, tn), lambda i, j: (layer, 0, j)),
            pl.BlockSpec((tm, tn), lambda i, j: (i, j)),
        ],
        out_specs=pl.BlockSpec((tm, tn), lambda i, j: (i, j)),
        out_shape=jax.ShapeDtypeStruct((t, dm), F32),
        scratch_shapes=[pltpu.VMEM((tm, dm), BF16)],
        compiler_params=_params("parallel", "arbitrary"),
        name="ple_gate",
    )(h, gain, w_gate, p, w_proj, h)


def _rope_tables(positions):
    inv_freq = jnp.power(ROPE_THETA, -jnp.arange(ROT_HALF, dtype=F32) * 2.0 / (2 * ROT_HALF))
    ang = positions.astype(F32)[..., None] * inv_freq
    cos, sin = jnp.cos(ang), jnp.sin(ang)
    rest = HEAD_DIM - 2 * ROT_HALF
    cos_t = jnp.concatenate([cos, cos, jnp.ones(ang.shape[:-1] + (rest,), F32)], axis=-1)
    sin_t = jnp.concatenate([-sin, sin, jnp.zeros(ang.shape[:-1] + (rest,), F32)], axis=-1)
    return cos_t.reshape(-1, HEAD_DIM), sin_t.reshape(-1, HEAD_DIM)


def kernel(x, p, positions, norm_mix, w_in, q_norm, k_norm, w_attn_br, ssm_a_re, ssm_a_im, ssm_log_dt,
           ssm_b_re, ssm_b_im, ssm_c_re, ssm_c_im, ssm_d, w_ssm_br, w_out, norm_ffn, w_router,
           w_exp_gate, w_exp_up, w_exp_down, norm_ple, w_ple_gate, w_ple_proj):
    bsz, seq, dm = x.shape
    depth = w_in.shape[0]
    t = bsz * seq
    n_attn = 3 * HEADS_PER_GROUP * len(DILATIONS) * HEAD_DIM
    ssm_width = ssm_d.shape[1]
    nc = seq // SSM_CHUNK
    n_steps = int(math.log2(nc))
    assert seq % PERM_TILE == 0 and nc == 1 << n_steps

    cos_t, sin_t = _rope_tables(positions)
    w_in_b, w_attn_b, w_ssm_b, w_out_b = (w.astype(BF16) for w in (w_in, w_attn_br, w_ssm_br, w_out))
    w_eg_b, w_eu_b, w_ed_b = (w.astype(BF16) for w in (w_exp_gate, w_exp_up, w_exp_down))
    w_pg_b, w_pp_b = w_ple_gate.astype(BF16), w_ple_proj.astype(BF16)
    tq, pm, tab = jax.vmap(functools.partial(_ssm_tables, n_steps=n_steps))(
        ssm_a_re, ssm_a_im, ssm_log_dt, ssm_b_re, ssm_b_im, ssm_c_re, ssm_c_im)
    toe, sin_w, sout_w, tabs = jax.vmap(_ssm_expand)(tq, pm, tab)
    p_rows = p.reshape(depth, t, p.shape[-1])
    h = x.reshape(t, dm)

    for l in range(depth):
        gain = norm_mix[l][None]
        qk_gain = jnp.stack([q_norm[l], k_norm[l], jnp.ones_like(q_norm[l])])[:, None, :]

        qkv = _qkv_proj(h, gain, w_in_b, l, qk_gain, cos_t, sin_t)
        u = _norm_matmul(h, gain, w_in_b, l, n_attn, ssm_width, out_dtype=F32, name="u_proj")
        gates = _norm_matmul(h, gain, w_in_b, l, n_attn + ssm_width, 2 * dm, out_dtype=BF16, act="sigmoid",
                             name="gate_proj")

        outs, lses = zip(*[_attention(qkv, gi, bsz, seq) for gi in range(len(DILATIONS))])
        gated_attn = _attn_merge_proj(outs, lses, w_attn_b, l, gates)

        y = _ssm_scan(u, toe, sin_w, sout_w, tabs, l, bsz, seq)
        merged = _ssm_glu_merge(y, u, ssm_d[l][None], w_ssm_b, l, gated_attn, gates)
        h = _resid_matmul(merged, w_out_b, l, h)

        h = _moe(h, norm_ffn[l][None], w_router[l].T, w_eg_b, w_eu_b, w_ed_b, l, bsz, seq)
        h = _ple(h, norm_ple[l][None], w_pg_b, p_rows, w_pp_b, l)
    return h.reshape(bsz, seq, dm)
```

```python
import functools
import math

import jax
import jax.numpy as jnp
from jax import lax
from jax.experimental import pallas as pl
from jax.experimental.pallas import tpu as pltpu

F32 = jnp.float32
BF16 = jnp.bfloat16
I32 = jnp.int32

NORM_EPS = 1e-6
MASK_VALUE = -1e30
ROPE_THETA = 500000.0

LANES = 128
HEAD_DIM = 128
HEADS_PER_GROUP = 4
GROUP_WIDTH = HEADS_PER_GROUP * HEAD_DIM
DILATIONS = (1, 4, 16)
N_SIDE = 64
ROT_HALF = 16

SSM_GROUP = 16
SSM_STATE = 64
SSM_CHUNK = 16

N_EXPERTS = 16
CAPACITY_FACTOR = 2
SLOT_WIN = 128
SLOT_ALIGN = 16
MOE_TILE = 512

PERM_TILE = 1024
VMEM_LIMIT = 56 * 1024 * 1024


def _params(*sem):
    return pltpu.CompilerParams(dimension_semantics=sem, vmem_limit_bytes=VMEM_LIMIT)


def _rms(x, gain):
    var = jnp.mean(x * x, axis=-1, keepdims=True)
    return x * lax.rsqrt(var + NORM_EPS) * gain


def _qkv_body(h_ref, g_ref, w_ref, qkg_ref, cos_ref, sin_ref, o_ref, xn_ref, slab_ref):
    j = pl.program_id(1)
    tm = h_ref.shape[0]

    @pl.when(j == 0)
    def _():
        xn_ref[...] = _rms(h_ref[...], g_ref[...]).astype(BF16)

    acc = jnp.dot(xn_ref[...], w_ref[...], preferred_element_type=F32)
    for hs in range(HEADS_PER_GROUP):
        slab_ref[hs] = acc[:, hs * HEAD_DIM:(hs + 1) * HEAD_DIM]

    @pl.when(j % 3 < 2)
    def _():
        lane = lax.broadcasted_iota(I32, (tm, HEAD_DIM), 1)
        cos = cos_ref[...]
        sin = sin_ref[...]
        gain = qkg_ref[...]
        for hs in range(HEADS_PER_GROUP):
            a = _rms(slab_ref[hs], gain)
            rot = jnp.where(lane < ROT_HALF,
                            pltpu.roll(a, HEAD_DIM - ROT_HALF, 1), pltpu.roll(a, ROT_HALF, 1))
            slab_ref[hs] = a * cos + rot * sin

    for gi, d in enumerate(DILATIONS):
        @pl.when(j // 3 == gi)
        def _(d=d):
            n = tm // d
            for hs in range(HEADS_PER_GROUP):
                for r in range(d):
                    v = slab_ref[hs] if d == 1 else slab_ref[hs, pl.ds(r, n, stride=d), :]
                    o_ref[r * n:(r + 1) * n, hs * HEAD_DIM:(hs + 1) * HEAD_DIM] = v.astype(BF16)


def _qkv_proj(h, gain, w_in, layer, qk_gain, cos, sin):
    t, dm = h.shape
    tm, tn = PERM_TILE, GROUP_WIDTH
    ng = len(DILATIONS)
    nj = 3 * ng
    return pl.pallas_call(
        _qkv_body,
        grid=(t // tm, nj),
        in_specs=[
            pl.BlockSpec((tm, dm), lambda i, j: (i, 0)),
            pl.BlockSpec((1, dm), lambda i, j: (0, 0)),
            pl.BlockSpec((None, dm, tn), lambda i, j: (layer, 0, (j % 3) * ng + j // 3)),
            pl.BlockSpec((None, 1, HEAD_DIM), lambda i, j: (j % 3, 0, 0)),
            pl.BlockSpec((tm, HEAD_DIM), lambda i, j: (i, 0)),
            pl.BlockSpec((tm, HEAD_DIM), lambda i, j: (i, 0)),
        ],
        out_specs=pl.BlockSpec((tm, tn), lambda i, j: (i, j)),
        out_shape=jax.ShapeDtypeStruct((t, nj * tn), BF16),
        scratch_shapes=[pltpu.VMEM((tm, dm), BF16),
                        pltpu.VMEM((HEADS_PER_GROUP, tm, HEAD_DIM), F32)],
        compiler_params=_params("parallel", "arbitrary"),
        name="qkv_proj",
    )(h, gain, w_in, qk_gain, cos, sin)


def _norm_mm_body(h_ref, g_ref, w_ref, o_ref, xn_ref, *, act):
    @pl.when(pl.program_id(1) == 0)
    def _():
        xn_ref[...] = _rms(h_ref[...], g_ref[...]).astype(BF16)

    acc = jnp.dot(xn_ref[...], w_ref[...], preferred_element_type=F32)
    if act == "sigmoid":
        acc = jax.nn.sigmoid(acc)
    o_ref[...] = acc.astype(o_ref.dtype)


def _norm_matmul(h, gain, w, layer, col0, ncols, *, out_dtype, act=None, tn=512, name):
    t, dm = h.shape
    tm = 1024
    nj = ncols // tn
    blk0 = col0 // tn
    return pl.pallas_call(
        functools.partial(_norm_mm_body, act=act),
        grid=(t // tm, nj),
        in_specs=[
            pl.BlockSpec((tm, dm), lambda i, j: (i, 0)),
            pl.BlockSpec((1, dm), lambda i, j: (0, 0)),
            pl.BlockSpec((None, dm, tn), lambda i, j: (layer, 0, blk0 + j)),
        ],
        out_specs=pl.BlockSpec((tm, tn), lambda i, j: (i, j)),
        out_shape=jax.ShapeDtypeStruct((t, nj * tn), out_dtype),
        scratch_shapes=[pltpu.VMEM((tm, dm), BF16)],
        compiler_params=_params("parallel", "arbitrary"),
        name=name,
    )(h, gain, w)


def _attn_body(q_ref, kp_ref, km_ref, kn_ref, vp_ref, vm_ref, vn_ref, o_ref, l_ref,
               kw_ref, vw_ref, os_ref, ls_ref, *, tq, sub_len):
    c = pl.program_id(2)
    kw_ref[0:N_SIDE] = kp_ref[...]
    kw_ref[N_SIDE:N_SIDE + tq] = km_ref[...].reshape(tq, GROUP_WIDTH)
    kw_ref[N_SIDE + tq:] = kn_ref[...]
    vw_ref[0:N_SIDE] = vp_ref[...]
    vw_ref[N_SIDE:N_SIDE + tq] = vm_ref[...].reshape(tq, GROUP_WIDTH)
    vw_ref[N_SIDE + tq:] = vn_ref[...]
    q = q_ref[...].reshape(tq, GROUP_WIDTH)

    sb = min(128, tq)
    nk = sb + 2 * N_SIDE
    scale = HEAD_DIM ** -0.5
    row = lax.broadcasted_iota(I32, (sb, nk), 0)
    col = lax.broadcasted_iota(I32, (sb, nk), 1)
    band = jnp.abs(col - row - N_SIDE) <= N_SIDE
    for i in range(tq // sb):
        kpos = c * tq + (i * sb - N_SIDE) + col
        valid = band & (kpos >= 0) & (kpos < sub_len)
        for hs in range(HEADS_PER_GROUP):
            lanes = slice(hs * HEAD_DIM, (hs + 1) * HEAD_DIM)
            qh = q[i * sb:(i + 1) * sb, lanes]
            kh = kw_ref[i * sb:i * sb + nk, lanes]
            vh = vw_ref[i * sb:i * sb + nk, lanes]
            s = lax.dot_general(qh, kh, (((1,), (1,)), ((), ())), preferred_element_type=F32) * scale
            s = jnp.where(valid, s, MASK_VALUE)
            m = jnp.max(s, axis=-1, keepdims=True)
            e = jnp.exp(s - m)
            den = jnp.sum(e, axis=-1, keepdims=True)
            o = jnp.dot((e / den).astype(BF16), vh, preferred_element_type=F32)
            os_ref[i * sb:(i + 1) * sb, lanes] = o
            ls_ref[i * sb:(i + 1) * sb, lanes] = jnp.broadcast_to(m + jnp.log(den), (sb, HEAD_DIM))
    o_ref[...] = os_ref[...].reshape(o_ref.shape)
    l_ref[...] = ls_ref[...].reshape(l_ref.shape)


def _attention(qkv, gi, bsz, seq):
    d = DILATIONS[gi]
    t = bsz * seq
    sub_len = seq // d
    nbt = PERM_TILE // (N_SIDE * d)
    ntile = seq // PERM_TILE
    tq = min(512, sub_len)
    nbq = tq // N_SIDE
    nblk = sub_len // N_SIDE
    view = (bsz, ntile, d, nbt, N_SIDE, qkv.shape[1])
    oview = (bsz, ntile, d, nbt, N_SIDE, GROUP_WIDTH)
    gw = GROUP_WIDTH

    if nbt >= nbq:
        per = nbt // nbq
        main_shape = (None, None, None, nbq, N_SIDE, gw)

        def main_idx(col):
            return lambda b, r, c: (b, c // per, r, c % per, 0, col)
    else:
        main_shape = (None, nbq // nbt, None, nbt, N_SIDE, gw)

        def main_idx(col):
            return lambda b, r, c: (b, c, r, 0, 0, col)

    halo_shape = (None, None, None, None, N_SIDE, gw)

    def prev_idx(col):
        def f(b, r, c):
            n = jnp.maximum(c * nbq - 1, 0)
            return (b, n // nbt, r, n % nbt, 0, col)
        return f

    def next_idx(col):
        def f(b, r, c):
            n = jnp.minimum((c + 1) * nbq, nblk - 1)
            return (b, n // nbt, r, n % nbt, 0, col)
        return f

    qc, kc, vc = 3 * gi, 3 * gi + 1, 3 * gi + 2
    x = qkv.reshape(view)
    o, l = pl.pallas_call(
        functools.partial(_attn_body, tq=tq, sub_len=sub_len),
        grid=(bsz, d, sub_len // tq),
        in_specs=[
            pl.BlockSpec(main_shape, main_idx(qc)),
            pl.BlockSpec(halo_shape, prev_idx(kc)),
            pl.BlockSpec(main_shape, main_idx(kc)),
            pl.BlockSpec(halo_shape, next_idx(kc)),
            pl.BlockSpec(halo_shape, prev_idx(vc)),
            pl.BlockSpec(main_shape, main_idx(vc)),
            pl.BlockSpec(halo_shape, next_idx(vc)),
        ],
        out_specs=[pl.BlockSpec(main_shape, main_idx(0)), pl.BlockSpec(main_shape, main_idx(0))],
        out_shape=[jax.ShapeDtypeStruct(oview, F32), jax.ShapeDtypeStruct(oview, F32)],
        scratch_shapes=[pltpu.VMEM((tq + 2 * N_SIDE, gw), BF16), pltpu.VMEM((tq + 2 * N_SIDE, gw), BF16),
                        pltpu.VMEM((tq, gw), F32), pltpu.VMEM((tq, gw), F32)],
        compiler_params=_params("parallel", "parallel", "arbitrary"),
        name=f"dilated_attn_d{d}",
    )(x, x, x, x, x, x, x)
    return o.reshape(t, gw), l.reshape(t, gw)


def _attn_merge_body(o0, l0, o1, l1, o2, l2, w_ref, g_ref, out_ref, comb_ref, so_ref, sl_ref):
    tm = out_ref.shape[0]

    @pl.when(pl.program_id(1) == 0)
    def _():
        for gi, (o_ref, l_ref) in enumerate(((o0, l0), (o1, l1), (o2, l2))):
            d = DILATIONS[gi]
            n = tm // d
            for hs in range(HEADS_PER_GROUP):
                lanes = slice(hs * HEAD_DIM, (hs + 1) * HEAD_DIM)
                for r in range(d):
                    ov = o_ref[r * n:(r + 1) * n, lanes]
                    lv = l_ref[r * n:(r + 1) * n, lanes]
                    if d == 1:
                        so_ref[gi, hs] = ov
                        sl_ref[gi, hs] = lv
                    else:
                        so_ref[gi, hs, pl.ds(r, n, stride=d), :] = ov
                        sl_ref[gi, hs, pl.ds(r, n, stride=d), :] = lv
        for hs in range(HEADS_PER_GROUP):
            ls = [sl_ref[gi, hs] for gi in range(3)]
            mx = jnp.maximum(jnp.maximum(ls[0], ls[1]), ls[2])
            ws = [jnp.exp(l - mx) for l in ls]
            num = ws[0] * so_ref[0, hs] + ws[1] * so_ref[1, hs] + ws[2] * so_ref[2, hs]
            comb = num / (ws[0] + ws[1] + ws[2])
            comb_ref[:, hs * HEAD_DIM:(hs + 1) * HEAD_DIM] = comb.astype(BF16)

    acc = jnp.dot(comb_ref[...], w_ref[...], preferred_element_type=F32)
    out_ref[...] = (acc * g_ref[...].astype(F32)).astype(out_ref.dtype)


def _attn_merge_proj(outs, lses, w_attn, layer, gates):
    t = outs[0].shape[0]
    dm = w_attn.shape[2]
    tm, tn = PERM_TILE, 512
    row = pl.BlockSpec((tm, GROUP_WIDTH), lambda i, j: (i, 0))
    return pl.pallas_call(
        _attn_merge_body,
        grid=(t // tm, dm // tn),
        in_specs=[row, row, row, row, row, row,
                  pl.BlockSpec((None, GROUP_WIDTH, tn), lambda i, j: (layer, 0, j)),
                  pl.BlockSpec((tm, tn), lambda i, j: (i, j))],
        out_specs=pl.BlockSpec((tm, tn), lambda i, j: (i, j)),
        out_shape=jax.ShapeDtypeStruct((t, dm), BF16),
        scratch_shapes=[pltpu.VMEM((tm, GROUP_WIDTH), BF16),
                        pltpu.VMEM((3, HEADS_PER_GROUP, tm, HEAD_DIM), F32),
                        pltpu.VMEM((3, HEADS_PER_GROUP, tm, HEAD_DIM), F32)],
        compiler_params=_params("parallel", "arbitrary"),
        name="attn_merge_proj",
    )(outs[0], lses[0], outs[1], lses[1], outs[2], lses[2], w_attn, gates)


def _ssm_tables(a_re, a_im, log_dt, b_re, b_im, c_re, c_im, n_steps):
    hi = lax.Precision.HIGHEST
    lc = SSM_CHUNK
    a_re, a_im = a_re.astype(F32), a_im.astype(F32)
    dt = jnp.exp(log_dt.astype(F32))[..., None]

    def lam_pow(n):
        n = n.astype(F32)[:, None, None, None]
        mag = jnp.exp(a_re * dt * n)
        ang = a_im * dt * n
        return mag * jnp.cos(ang), mag * jnp.sin(ang)

    one_re, one_im = lam_pow(jnp.ones((1,)))
    lb_re, lb_im = one_re[0], one_im[0]
    den = a_re * a_re + a_im * a_im
    nr, ni = lb_re - 1.0, lb_im
    coef_re = ((nr * a_re + ni * a_im) / den)[..., None]
    coef_im = ((ni * a_re - nr * a_im) / den)[..., None]
    b_re, b_im = b_re.astype(F32), b_im.astype(F32)
    bb_re = coef_re * b_re - coef_im * b_im
    bb_im = coef_re * b_im + coef_im * b_re
    c_re, c_im = c_re.astype(F32), c_im.astype(F32)

    pw_re, pw_im = lam_pow(jnp.arange(lc + 1))
    e_re = pw_re[..., None] * bb_re - pw_im[..., None] * bb_im
    e_im = pw_re[..., None] * bb_im + pw_im[..., None] * bb_re
    kern = (jnp.einsum('dgop,tdgpi->tdgoi', c_re, e_re, precision=hi)
            - jnp.einsum('dgop,tdgpi->tdgoi', c_im, e_im, precision=hi))
    s_idx = jnp.arange(lc)[:, None]
    t_idx = jnp.arange(lc)[None, :]
    lag_f = jnp.clip(t_idx - s_idx, 0, lc)
    lag_b = jnp.clip(s_idx - t_idx, 0, lc)
    kf = jnp.where((t_idx >= s_idx)[..., None, None, None], kern[lag_f, 0], 0.0)
    kb = jnp.where((s_idx >= t_idx)[..., None, None, None], kern[lag_b, 1], 0.0)
    toep = jnp.transpose(kf + kb, (2, 0, 4, 1, 3))
    g = toep.shape[0]
    toep = toep.reshape(g, lc * SSM_GROUP, lc * SSM_GROUP)

    def state_in(e, direction, taus):
        return jnp.transpose(e[taus, direction], (1, 0, 3, 2)).reshape(g, lc * SSM_GROUP, SSM_STATE)

    tau_f = lc - 1 - jnp.arange(lc)
    tau_b = jnp.arange(lc)
    q_in = jnp.concatenate([state_in(e_re, 0, tau_f), state_in(e_im, 0, tau_f),
                            state_in(e_re, 1, tau_b), state_in(e_im, 1, tau_b)], axis=-1)
    tq = jnp.concatenate([toep, q_in], axis=-1)

    def state_out(direction, taus):
        lr = pw_re[taus, direction][:, :, None, :]
        li = pw_im[taus, direction][:, :, None, :]
        cr, ci = c_re[direction][None], c_im[direction][None]
        mr = cr * lr - ci * li
        mi = cr * li + ci * lr
        to_rows = lambda m: jnp.transpose(m, (1, 3, 0, 2)).reshape(g, SSM_STATE, lc * SSM_GROUP)
        return jnp.concatenate([to_rows(mr), to_rows(-mi)], axis=1)

    pm = jnp.concatenate([state_out(0, jnp.arange(lc) + 1), state_out(1, lc - jnp.arange(lc))], axis=1)

    sc_re, sc_im = lam_pow(lc * (2 ** jnp.arange(n_steps)))
    return tq.astype(BF16), pm.astype(BF16), jnp.stack([sc_re, sc_im], axis=2)


def _ssm_compact(tq, pm, sc):
    g = tq.shape[0]
    gps = LANES // SSM_GROUP
    ns, npair = g // gps, gps // 2
    lc, hg, p = SSM_CHUNK, SSM_GROUP, SSM_STATE
    kw = lc * hg
    side = lc * LANES

    def rows_sgh(m):
        return m.reshape(ns, gps, lc, hg, m.shape[-1]).transpose(0, 2, 1, 3, 4).reshape(ns, side, m.shape[-1])

    c_sout = pm.reshape(ns, npair, 2, 2, 2, p, kw).transpose(0, 3, 1, 4, 2, 5, 6).reshape(ns, side, kw)
    comp = jnp.stack([rows_sgh(tq[..., :kw]), rows_sgh(tq[..., kw:]), c_sout], axis=1)
    n_steps = sc.shape[0]
    tab = sc.reshape(n_steps, 2, 2, ns, npair, 2, p).transpose(3, 1, 4, 0, 2, 5, 6)
    return comp, tab.reshape(ns, 2 * npair * n_steps * 2, 2 * p)


def _ssm_expand_constants():
    lc, hg, p = SSM_CHUNK, SSM_GROUP, SSM_STATE
    r = jnp.arange(lc * hg)[:, None]
    c = jnp.arange(lc * LANES)[None, :]
    e_time = (r // hg == c // LANES) & (r % hg == c % hg)
    e_state = (r // (2 * p) == c // (lc * LANES // 2)) & ((r // p) % 2 == (c // LANES) % 2) & (r % p == c % p)
    rr = jnp.arange(lc * LANES)[:, None]
    lane_group = lambda i: (i % LANES) // hg
    state_group = lambda i: 2 * ((i // (2 * LANES)) % (LANES // hg // 2)) + (i % LANES) // p
    masks = [lane_group(rr) == lane_group(c), lane_group(rr) == state_group(c), state_group(rr) == lane_group(c)]
    return (jnp.stack([e_time, e_state, e_time]).astype(BF16), jnp.stack(masks).astype(BF16))


def _ssm_expand_body(c_ref, e_ref, m_ref, o_ref):
    rows = 512
    for r in range(0, o_ref.shape[0], rows):
        acc = jnp.dot(c_ref[r:r + rows], e_ref[...], preferred_element_type=F32)
        o_ref[r:r + rows] = jnp.where(m_ref[r:r + rows] > 0, acc, 0.0).astype(o_ref.dtype)


def _ssm_expand(comp):
    depth, ns, three, side, kw = comp.shape
    e, mask = _ssm_expand_constants()
    return pl.pallas_call(
        _ssm_expand_body,
        grid=(three, depth * ns),
        in_specs=[
            pl.BlockSpec((None, None, None, side, kw), lambda m, n: (n // ns, n % ns, m, 0, 0)),
            pl.BlockSpec((None, kw, side), lambda m, n: (m, 0, 0)),
            pl.BlockSpec((None, side, side), lambda m, n: (m, 0, 0)),
        ],
        out_specs=pl.BlockSpec((None, None, None, side, side), lambda m, n: (n // ns, n % ns, m, 0, 0)),
        out_shape=jax.ShapeDtypeStruct((depth, ns, three, side, side), BF16),
        compiler_params=_params("arbitrary", "arbitrary"),
        name="s5_expand_weights",
    )(comp, e, mask)


def _ssm_body(u_ref, toe_ref, sin_ref, sout_ref, tab_ref, y_ref, lhs_ref, inj_ref, st_ref, *, n_steps):
    seq = u_ref.shape[0]
    lc = SSM_CHUNK
    nc = seq // lc
    n_slabs = inj_ref.shape[0]
    per_dir = n_slabs // 4
    row = lax.broadcasted_iota(I32, (nc, LANES), 0)

    for s in range(lc):
        lhs_ref[:, s * LANES:(s + 1) * LANES] = u_ref[pl.ds(s, nc, stride=lc), :].astype(BF16)
    lhs = lhs_ref[...]

    for j in range(0, n_slabs, 2):
        r = jnp.dot(lhs, sin_ref[:, j * LANES:(j + 2) * LANES], preferred_element_type=F32)
        inj_ref[j] = r[:, :LANES]
        inj_ref[j + 1] = r[:, LANES:]

    def shifted(x, sh, direction):
        if sh % 8 == 0:
            zeros = jnp.zeros((sh, LANES), x.dtype)
            if direction == 0:
                return jnp.concatenate([zeros, x[:nc - sh]], axis=0)
            return jnp.concatenate([x[sh:], zeros], axis=0)
        if direction == 0:
            return jnp.where(row >= sh, pltpu.roll(x, sh, 0), 0.0)
        return jnp.where(row < nc - sh, pltpu.roll(x, nc - sh, 0), 0.0)

    for direction in range(2):
        def scan_pair(jq, carry, direction=direction):
            xr = inj_ref[2 * jq]
            xi = inj_ref[2 * jq + 1]
            for k in range(n_steps):
                base = (jq * n_steps + k) * 2
                ar = tab_ref[pl.ds(base, 1), :]
                ai = tab_ref[pl.ds(base + 1, 1), :]
                sr = shifted(xr, 1 << k, direction)
                si = shifted(xi, 1 << k, direction)
                xr, xi = xr + ar * sr - ai * si, xi + ar * si + ai * sr
            st_ref[2 * jq] = shifted(xr, 1, direction).astype(BF16)
            st_ref[2 * jq + 1] = shifted(xi, 1, direction).astype(BF16)
            return carry

        lax.fori_loop(direction * per_dir, (direction + 1) * per_dir, scan_pair, 0)

    states = jnp.concatenate([st_ref[j] for j in range(n_slabs)], axis=1)
    for t in range(0, lc, 2):
        cols = slice(t * LANES, (t + 2) * LANES)
        r = (jnp.dot(lhs, toe_ref[:, cols], preferred_element_type=F32)
             + jnp.dot(states, sout_ref[:, cols], preferred_element_type=F32))
        y_ref[pl.ds(t, nc, stride=lc), :] = r[:, :LANES]
        y_ref[pl.ds(t + 1, nc, stride=lc), :] = r[:, LANES:]


def _ssm_scan(u, w_slab, tab, layer, bsz, seq):
    t, width = u.shape
    ns = width // LANES
    nc = seq // SSM_CHUNK
    n_steps = int(math.log2(nc))
    side = SSM_CHUNK * LANES
    n_slabs = side // LANES
    once = pl.Buffered(1)

    def weight(m):
        return pl.BlockSpec((None, None, None, side, side), lambda g, b: (layer, g, m, 0, 0), pipeline_mode=once)

    return pl.pallas_call(
        functools.partial(_ssm_body, n_steps=n_steps),
        grid=(ns, bsz),
        in_specs=[
            pl.BlockSpec((seq, LANES), lambda g, b: (b, g)),
            weight(0), weight(1), weight(2),
            pl.BlockSpec((None, None, tab.shape[-2], LANES), lambda g, b: (layer, g, 0, 0)),
        ],
        out_specs=pl.BlockSpec((seq, LANES), lambda g, b: (b, g)),
        out_shape=jax.ShapeDtypeStruct((t, width), F32),
        scratch_shapes=[pltpu.VMEM((nc, side), BF16),
                        pltpu.VMEM((n_slabs, nc, LANES), F32),
                        pltpu.VMEM((n_slabs, nc, LANES), BF16)],
        compiler_params=_params("arbitrary", "arbitrary"),
        name="s5_chunk_scan",
    )(u, w_slab, w_slab, w_slab, tab)


def _ssm_glu_body(y_ref, u_ref, d_ref, wa_ref, wb_ref, ga_ref, gs_ref, o_ref, act_ref):
    @pl.when(pl.program_id(1) == 0)
    def _():
        act_ref[...] = jax.nn.gelu(y_ref[...] + d_ref[...] * u_ref[...]).astype(BF16)

    act = act_ref[...]
    za = jnp.dot(act, wa_ref[...], preferred_element_type=F32)
    zb = jnp.dot(act, wb_ref[...], preferred_element_type=F32)
    s_branch = za * jax.nn.sigmoid(zb)
    o_ref[...] = (ga_ref[...].astype(F32) + gs_ref[...].astype(F32) * s_branch).astype(o_ref.dtype)


def _ssm_glu_merge(y, u, d_skip, w_ssm, layer, gated_attn, gates):
    t, width = y.shape
    dm = gated_attn.shape[1]
    tm, tn = 1024, 512
    nj = dm // tn
    return pl.pallas_call(
        _ssm_glu_body,
        grid=(t // tm, nj),
        in_specs=[
            pl.BlockSpec((tm, width), lambda i, j: (i, 0)),
            pl.BlockSpec((tm, width), lambda i, j: (i, 0)),
            pl.BlockSpec((1, width), lambda i, j: (0, 0)),
            pl.BlockSpec((None, width, tn), lambda i, j: (layer, 0, j)),
            pl.BlockSpec((None, width, tn), lambda i, j: (layer, 0, j + nj)),
            pl.BlockSpec((tm, tn), lambda i, j: (i, j)),
            pl.BlockSpec((tm, tn), lambda i, j: (i, j + nj)),
        ],
        out_specs=pl.BlockSpec((tm, tn), lambda i, j: (i, j)),
        out_shape=jax.ShapeDtypeStruct((t, dm), BF16),
        scratch_shapes=[pltpu.VMEM((tm, width), BF16)],
        compiler_params=_params("parallel", "arbitrary"),
        name="s5_glu_merge",
    )(y, u, d_skip, w_ssm, w_ssm, gated_attn, gates)


def _resid_mm_body(x_ref, w_ref, h_ref, o_ref):
    o_ref[...] = h_ref[...] + jnp.dot(x_ref[...], w_ref[...], preferred_element_type=F32)


def _resid_matmul(x, w, layer, h):
    t, k = x.shape
    dm = w.shape[2]
    tm, tn = 1024, 512
    return pl.pallas_call(
        _resid_mm_body,
        grid=(t // tm, dm // tn),
        in_specs=[
            pl.BlockSpec((tm, k), lambda i, j: (i, 0)),
            pl.BlockSpec((None, k, tn), lambda i, j: (layer, 0, j)),
            pl.BlockSpec((tm, tn), lambda i, j: (i, j)),
        ],
        out_specs=pl.BlockSpec((tm, tn), lambda i, j: (i, j)),
        out_shape=jax.ShapeDtypeStruct((t, dm), F32),
        compiler_params=_params("parallel", "arbitrary"),
        name="out_proj",
    )(x, w, h)


def _router_body(h_ref, g_ref, wr_ref, xn_ref, lg_ref):
    xn = _rms(h_ref[...], g_ref[...])
    xn_ref[...] = xn.astype(BF16)
    lg_ref[...] = lax.dot_general(wr_ref[...], xn, (((1,), (1,)), ((), ())),
                                  precision=lax.Precision.HIGHEST, preferred_element_type=F32)


def _router_logits(h, gain, w_router_t):
    t, dm = h.shape
    tm = 512
    return pl.pallas_call(
        _router_body,
        grid=(t // tm,),
        in_specs=[
            pl.BlockSpec((tm, dm), lambda i: (i, 0)),
            pl.BlockSpec((1, dm), lambda i: (0, 0)),
            pl.BlockSpec((N_EXPERTS, dm), lambda i: (0, 0)),
        ],
        out_specs=[pl.BlockSpec((tm, dm), lambda i: (i, 0)),
                   pl.BlockSpec((N_EXPERTS, tm), lambda i: (0, i))],
        out_shape=[jax.ShapeDtypeStruct((t, dm), BF16), jax.ShapeDtypeStruct((N_EXPERTS, t), F32)],
        compiler_params=_params("parallel"),
        name="router_logits",
    )(h, gain, w_router_t)


def _select_body(lg_ref, gate_ref, pos_ref, sel_ref, *, cap):
    lg = lg_ref[...]
    ne, s = lg.shape
    m = jnp.max(lg, axis=0, keepdims=True)
    e = jnp.exp(lg - m)
    aff = e / jnp.sum(e, axis=0, keepdims=True)
    bits = lax.bitcast_convert_type(aff, I32)

    def count(ind):
        return jnp.sum(ind, axis=1, keepdims=True)

    def value_bit(k, thr):
        cand = thr | jnp.left_shift(jnp.int32(1), 30 - k)
        return jnp.where(count(jnp.where(bits >= cand, 1.0, 0.0)) >= cap, cand, thr)

    thr = lax.fori_loop(0, 31, value_bit, jnp.zeros((ne, 1), I32))
    above = jnp.where(bits > thr, 1.0, 0.0)
    tie = jnp.where(bits == thr, 1.0, 0.0)
    need = cap - count(above)
    idx = lax.broadcasted_iota(I32, (ne, s), 1)
    n_bits = int(math.log2(s)) + 1

    def index_bit(k, bound):
        cand = bound | jnp.left_shift(jnp.int32(1), n_bits - 1 - k)
        below = count(jnp.where(idx < cand, tie, 0.0))
        return jnp.where(below < need, cand, bound)

    bound = lax.fori_loop(0, n_bits, index_bit, jnp.zeros((ne, 1), I32))
    sel = above + jnp.where(idx <= bound, tie, 0.0)
    gate_ref[...] = sel * aff
    sel_ref[...] = sel.astype(I32)

    blk = 256
    tri = jnp.where(lax.broadcasted_iota(I32, (blk, blk), 0) <= lax.broadcasted_iota(I32, (blk, blk), 1),
                    1.0, 0.0).astype(BF16)
    carry = jnp.zeros((ne, 1), F32)
    for j in range(s // blk):
        seg = sel[:, j * blk:(j + 1) * blk]
        inc = jnp.dot(seg.astype(BF16), tri, preferred_element_type=F32)
        pos_ref[:, j * blk:(j + 1) * blk] = (inc - seg + carry).astype(I32)
        carry = carry + inc[:, blk - 1:blk]


def _select_tokens(logits_t, bsz, seq):
    cap = CAPACITY_FACTOR * seq // N_EXPERTS
    t = bsz * seq
    spec = pl.BlockSpec((N_EXPERTS, seq), lambda b: (0, b))
    return pl.pallas_call(
        functools.partial(_select_body, cap=cap),
        grid=(bsz,),
        in_specs=[spec],
        out_specs=[spec, spec, spec],
        out_shape=[jax.ShapeDtypeStruct((N_EXPERTS, t), F32), jax.ShapeDtypeStruct((N_EXPERTS, t), I32),
                   jax.ShapeDtypeStruct((N_EXPERTS, t), I32)],
        compiler_params=_params("parallel"),
        name="expert_choice_select",
    )(logits_t)


def _slot_window(base_ref, b, e, i, nt, cap):
    k = (b * N_EXPERTS + e) * (nt + 1) + i
    lo = base_ref[k]
    start = jnp.minimum((lo // SLOT_ALIGN) * SLOT_ALIGN, cap - SLOT_WIN)
    return pl.multiple_of(start, SLOT_ALIGN), base_ref[k + 1]


def _n_extra_windows(start, hi):
    return (jnp.maximum(hi - (start + SLOT_WIN), 0) + SLOT_WIN - 1) // SLOT_WIN


def _extra_window(start, k, cap):
    first = start + SLOT_WIN * k
    return first, pl.multiple_of(jnp.minimum(first, cap - SLOT_WIN), SLOT_ALIGN)


def _dispatch_body(base_ref, x_ref, slot_ref, out_ref, *, nt, cap):
    b = pl.program_id(0)
    i = pl.program_id(2)
    ts = x_ref.shape[0]

    @pl.when(i == 0)
    def _():
        out_ref[...] = jnp.zeros_like(out_ref)

    x = x_ref[...]
    wio = lax.broadcasted_iota(I32, (SLOT_WIN, ts), 0)
    wins = [_slot_window(base_ref, b, e, i, nt, cap) for e in range(N_EXPERTS)]
    onehot = jnp.concatenate(
        [jnp.where(slot_ref[e:e + 1, :] - wins[e][0] == wio, 1.0, 0.0).astype(x.dtype) for e in range(N_EXPERTS)],
        axis=0)
    res = jnp.dot(onehot, x, preferred_element_type=F32)
    for e in range(N_EXPERTS):
        out_ref[e, pl.ds(wins[e][0], SLOT_WIN), :] += res[e * SLOT_WIN:(e + 1) * SLOT_WIN].astype(out_ref.dtype)

    for e in range(N_EXPERTS):
        start, hi = wins[e]

        def extra(k, carry, e=e, start=start):
            first, st = _extra_window(start, k, cap)
            srow = slot_ref[e:e + 1, :]
            srow = jnp.where(srow >= first, srow, -1)
            oh = jnp.where(srow - st == wio, 1.0, 0.0).astype(x.dtype)
            out_ref[e, pl.ds(st, SLOT_WIN), :] += jnp.dot(oh, x, preferred_element_type=F32).astype(out_ref.dtype)
            return carry

        lax.fori_loop(1, 1 + _n_extra_windows(start, hi), extra, 0)


def _dispatch(base, x, slot_t, bsz, seq, cap, *, cw, out_dtype, name):
    t, width = x.shape
    ts = MOE_TILE
    nt = seq // ts
    grid_spec = pltpu.PrefetchScalarGridSpec(
        num_scalar_prefetch=1,
        grid=(bsz, width // cw, nt),
        in_specs=[
            pl.BlockSpec((ts, cw), lambda b, c, i, base: (b * nt + i, c)),
            pl.BlockSpec((N_EXPERTS, ts), lambda b, c, i, base: (0, b * nt + i)),
        ],
        out_specs=pl.BlockSpec((None, N_EXPERTS, cap, cw), lambda b, c, i, base: (b, 0, 0, c)),
    )
    return pl.pallas_call(
        functools.partial(_dispatch_body, nt=nt, cap=cap),
        grid_spec=grid_spec,
        out_shape=jax.ShapeDtypeStruct((bsz, N_EXPERTS, cap, width), out_dtype),
        compiler_params=_params("parallel", "parallel", "arbitrary"),
        name=name,
    )(base, x, slot_t)


def _expert_body(x_ref, wg_ref, wu_ref, wd_ref, gs_ref, y_ref, acc_ref):
    e = pl.program_id(0)
    f = pl.program_id(2)
    x = x_ref[...]
    hid = (jax.nn.silu(jnp.dot(x, wg_ref[...], preferred_element_type=F32))
           * jnp.dot(x, wu_ref[...], preferred_element_type=F32)).astype(BF16)
    part = jnp.dot(hid, wd_ref[...], preferred_element_type=F32)

    @pl.when(f == 0)
    def _():
        acc_ref[...] = part

    @pl.when(f > 0)
    def _():
        acc_ref[...] += part

    @pl.when(f == pl.num_programs(2) - 1)
    def _():
        pieces = gs_ref[...]
        lane = lax.broadcasted_iota(I32, pieces.shape, 1)
        mine = (lane >= 3 * e) & (lane < 3 * e + 3)
        gate = jnp.sum(jnp.where(mine, pieces, 0.0), axis=1, keepdims=True)
        y_ref[...] = (acc_ref[...] * gate).astype(y_ref.dtype)


def _expert_ffn(xg, w_gate, w_up, w_down, layer, gate_slots):
    bsz, ne, cap, dm = xg.shape
    ff = w_gate.shape[3]
    tf = 512
    return pl.pallas_call(
        _expert_body,
        grid=(ne, bsz, ff // tf),
        in_specs=[
            pl.BlockSpec((None, None, cap, dm), lambda e, b, f: (b, e, 0, 0)),
            pl.BlockSpec((None, None, dm, tf), lambda e, b, f: (layer, e, 0, f)),
            pl.BlockSpec((None, None, dm, tf), lambda e, b, f: (layer, e, 0, f)),
            pl.BlockSpec((None, None, tf, dm), lambda e, b, f: (layer, e, f, 0)),
            pl.BlockSpec((None, None, cap, 128), lambda e, b, f: (b, e, 0, 0)),
        ],
        out_specs=pl.BlockSpec((None, None, cap, dm), lambda e, b, f: (b, e, 0, 0)),
        out_shape=jax.ShapeDtypeStruct((bsz, ne, cap, dm), BF16),
        scratch_shapes=[pltpu.VMEM((cap, dm), F32)],
        compiler_params=_params("parallel", "parallel", "arbitrary"),
        name="expert_swiglu",
    )(xg, w_gate, w_up, w_down, gate_slots)


def _combine_body(base_ref, yg_hbm, h_ref, slot_ref, out_ref, wbuf, xbuf, wsem, xsem, *, nt, cap, n_steps):
    n = pl.program_id(0)
    ts = h_ref.shape[0]

    def window_copies(step, half):
        b = step // nt
        i = step % nt
        copies = []
        for e in range(N_EXPERTS):
            start, _ = _slot_window(base_ref, b, e, i, nt, cap)
            copies.append(pltpu.make_async_copy(yg_hbm.at[b, e, pl.ds(start, SLOT_WIN), :],
                                                wbuf.at[half, e], wsem.at[half]))
        return copies

    @pl.when(n == 0)
    def _():
        for c in window_copies(0, 0):
            c.start()

    @pl.when(n + 1 < n_steps)
    def _():
        for c in window_copies(n + 1, (n + 1) % 2):
            c.start()

    b = n // nt
    i = n % nt
    half = n % 2
    lane = lax.broadcasted_iota(I32, (ts, SLOT_WIN), 1)
    wins = [_slot_window(base_ref, b, e, i, nt, cap) for e in range(N_EXPERTS)]
    onehot = jnp.concatenate(
        [jnp.where(slot_ref[:, e:e + 1] - wins[e][0] == lane, 1.0, 0.0).astype(BF16) for e in range(N_EXPERTS)],
        axis=1)
    for c in window_copies(n, half):
        c.wait()
    rows = wbuf[half].reshape(N_EXPERTS * SLOT_WIN, wbuf.shape[-1])
    out_ref[...] = h_ref[...] + jnp.dot(onehot, rows, preferred_element_type=F32)

    for e in range(N_EXPERTS):
        start, hi = wins[e]

        def extra(k, carry, e=e, start=start):
            first, st = _extra_window(start, k, cap)
            copy = pltpu.make_async_copy(yg_hbm.at[b, e, pl.ds(st, SLOT_WIN), :], xbuf, xsem)
            copy.start()
            scol = slot_ref[:, e:e + 1]
            scol = jnp.where(scol >= first, scol, -1)
            oh = jnp.where(scol - st == lane, 1.0, 0.0).astype(BF16)
            copy.wait()
            out_ref[...] += jnp.dot(oh, xbuf[...], preferred_element_type=F32)
            return carry

        lax.fori_loop(1, 1 + _n_extra_windows(start, hi), extra, 0)


def _combine(base, yg, h, slot_tok, bsz, seq):
    t, dm = h.shape
    cap = yg.shape[2]
    ts = MOE_TILE
    nt = seq // ts
    n_steps = bsz * nt
    grid_spec = pltpu.PrefetchScalarGridSpec(
        num_scalar_prefetch=1,
        grid=(n_steps,),
        in_specs=[
            pl.BlockSpec(memory_space=pl.ANY),
            pl.BlockSpec((ts, dm), lambda n, base: (n, 0)),
            pl.BlockSpec((ts, N_EXPERTS), lambda n, base: (n, 0)),
        ],
        out_specs=pl.BlockSpec((ts, dm), lambda n, base: (n, 0)),
        scratch_shapes=[pltpu.VMEM((2, N_EXPERTS, SLOT_WIN, dm), BF16),
                        pltpu.VMEM((SLOT_WIN, dm), BF16),
                        pltpu.SemaphoreType.DMA((2,)),
                        pltpu.SemaphoreType.DMA(())],
    )
    return pl.pallas_call(
        functools.partial(_combine_body, nt=nt, cap=cap, n_steps=n_steps),
        grid_spec=grid_spec,
        out_shape=jax.ShapeDtypeStruct((t, dm), F32),
        compiler_params=_params("arbitrary"),
        name="moe_combine",
    )(base, yg, h, slot_tok)


def _moe(h, gain, w_router_t, w_gate, w_up, w_down, layer, bsz, seq):
    cap = CAPACITY_FACTOR * seq // N_EXPERTS
    xn, logits_t = _router_logits(h, gain, w_router_t)
    gate_t, pos_t, sel_t = _select_tokens(logits_t, bsz, seq)

    slot_t = jnp.where(sel_t > 0, pos_t, -1)
    ts = MOE_TILE
    nt = seq // ts
    starts = pos_t.reshape(N_EXPERTS, bsz, nt, ts)[..., 0]
    base = jnp.concatenate([jnp.transpose(starts, (1, 0, 2)),
                            jnp.full((bsz, N_EXPERTS, 1), cap, I32)], axis=-1).reshape(-1)
    gate_tok = gate_t.T
    g_hi = gate_tok.astype(BF16)
    r1 = gate_tok - g_hi.astype(F32)
    g_mid = r1.astype(BF16)
    g_lo = (r1 - g_mid.astype(F32)).astype(BF16)
    pieces = jnp.stack([g_hi, g_mid, g_lo], axis=-1).reshape(-1, 3 * N_EXPERTS)
    pieces = jnp.pad(pieces, ((0, 0), (0, 128 - 3 * N_EXPERTS)))

    xg = _dispatch(base, xn, slot_t, bsz, seq, cap, cw=512, out_dtype=BF16, name="moe_dispatch")
    gate_slots = _dispatch(base, pieces, slot_t, bsz, seq, cap, cw=128, out_dtype=F32, name="moe_gate_dispatch")
    yg = _expert_ffn(xg, w_gate, w_up, w_down, layer, gate_slots)
    return _combine(base, yg, h, slot_t.T, bsz, seq)


def _ple_body(h_ref, g_ref, wg_ref, p_ref, wp_ref, hres_ref, o_ref, xn_ref):
    @pl.when(pl.program_id(1) == 0)
    def _():
        xn_ref[...] = _rms(h_ref[...], g_ref[...]).astype(BF16)

    gate = jax.nn.sigmoid(jnp.dot(xn_ref[...], wg_ref[...], preferred_element_type=F32))
    ple = jnp.dot(p_ref[...].astype(BF16), wp_ref[...], preferred_element_type=F32)
    o_ref[...] = hres_ref[...] + gate * ple


def _ple(h, gain, w_gate, p, w_proj, layer):
    t, dm = h.shape
    tm, tn = 1024, 512
    pd = p.shape[2]
    return pl.pallas_call(
        _ple_body,
        grid=(t // tm, dm // tn),
        in_specs=[
            pl.BlockSpec((tm, dm), lambda i, j: (i, 0)),
            pl.BlockSpec((1, dm), lambda i, j: (0, 0)),
            pl.BlockSpec((None, dm, tn), lambda i, j: (layer, 0, j)),
            pl.BlockSpec((None, tm, pd), lambda i, j: (layer, i, 0)),
            pl.BlockSpec((None, pd, tn), lambda i, j: (layer, 0, j)),
            pl.BlockSpec((tm, tn), lambda i, j: (i, j)),
        ],
        out_specs=pl.BlockSpec((tm, tn), lambda i, j: (i, j)),
        out_shape=jax.ShapeDtypeStruct((t, dm), F32),
        scratch_shapes=[pltpu.VMEM((tm, dm), BF16)],
        compiler_params=_params("parallel", "arbitrary"),
        name="ple_gate",
    )(h, gain, w_gate, p, w_proj, h)


def _rope_tables(positions):
    inv_freq = jnp.power(ROPE_THETA, -jnp.arange(ROT_HALF, dtype=F32) * 2.0 / (2 * ROT_HALF))
    ang = positions.astype(F32)[..., None] * inv_freq
    cos, sin = jnp.cos(ang), jnp.sin(ang)
    rest = HEAD_DIM - 2 * ROT_HALF
    cos_t = jnp.concatenate([cos, cos, jnp.ones(ang.shape[:-1] + (rest,), F32)], axis=-1)
    sin_t = jnp.concatenate([-sin, sin, jnp.zeros(ang.shape[:-1] + (rest,), F32)], axis=-1)
    return cos_t.reshape(-1, HEAD_DIM), sin_t.reshape(-1, HEAD_DIM)


def kernel(x, p, positions, norm_mix, w_in, q_norm, k_norm, w_attn_br, ssm_a_re, ssm_a_im, ssm_log_dt,
           ssm_b_re, ssm_b_im, ssm_c_re, ssm_c_im, ssm_d, w_ssm_br, w_out, norm_ffn, w_router,
           w_exp_gate, w_exp_up, w_exp_down, norm_ple, w_ple_gate, w_ple_proj):
    bsz, seq, dm = x.shape
    depth = w_in.shape[0]
    t = bsz * seq
    n_attn = 3 * HEADS_PER_GROUP * len(DILATIONS) * HEAD_DIM
    ssm_width = ssm_d.shape[1]
    nc = seq // SSM_CHUNK
    n_steps = int(math.log2(nc))
    assert seq % PERM_TILE == 0 and nc == 1 << n_steps

    cos_t, sin_t = _rope_tables(positions)
    w_in_b, w_attn_b, w_ssm_b, w_out_b = (w.astype(BF16) for w in (w_in, w_attn_br, w_ssm_br, w_out))
    w_eg_b, w_eu_b, w_ed_b = (w.astype(BF16) for w in (w_exp_gate, w_exp_up, w_exp_down))
    w_pg_b, w_pp_b = w_ple_gate.astype(BF16), w_ple_proj.astype(BF16)
    tq, pm, sc = jax.vmap(functools.partial(_ssm_tables, n_steps=n_steps))(
        ssm_a_re, ssm_a_im, ssm_log_dt, ssm_b_re, ssm_b_im, ssm_c_re, ssm_c_im)
    comp, tabs = jax.vmap(_ssm_compact)(tq, pm, sc)
    w_slab = _ssm_expand(comp)
    p_rows = p.reshape(depth, t, p.shape[-1])
    h = x.reshape(t, dm)

    for l in range(depth):
        gain = norm_mix[l][None]
        qk_gain = jnp.stack([q_norm[l], k_norm[l], jnp.ones_like(q_norm[l])])[:, None, :]

        qkv = _qkv_proj(h, gain, w_in_b, l, qk_gain, cos_t, sin_t)
        u = _norm_matmul(h, gain, w_in_b, l, n_attn, ssm_width, out_dtype=F32, name="u_proj")
        gates = _norm_matmul(h, gain, w_in_b, l, n_attn + ssm_width, 2 * dm, out_dtype=BF16, act="sigmoid",
                             name="gate_proj")

        outs, lses = zip(*[_attention(qkv, gi, bsz, seq) for gi in range(len(DILATIONS))])
        gated_attn = _attn_merge_proj(outs, lses, w_attn_b, l, gates)

        y = _ssm_scan(u, w_slab, tabs, l, bsz, seq)
        merged = _ssm_glu_merge(y, u, ssm_d[l][None], w_ssm_b, l, gated_attn, gates)
        h = _resid_matmul(merged, w_out_b, l, h)

        h = _moe(h, norm_ffn[l][None], w_router[l].T, w_eg_b, w_eu_b, w_ed_b, l, bsz, seq)
        h = _ple(h, norm_ple[l][None], w_pg_b, p_rows, w_pp_b, l)
    return h.reshape(bsz, seq, dm)
```

```python
import functools
import math

import jax
import jax.numpy as jnp
from jax import lax
from jax.experimental import pallas as pl
from jax.experimental.pallas import tpu as pltpu

F32 = jnp.float32
BF16 = jnp.bfloat16
I32 = jnp.int32

NORM_EPS = 1e-6
MASK_VALUE = -1e30
ROPE_THETA = 500000.0

LANES = 128
HEAD_DIM = 128
HEADS_PER_GROUP = 4
GROUP_WIDTH = HEADS_PER_GROUP * HEAD_DIM
DILATIONS = (1, 4, 16)
N_SIDE = 64
ROT_HALF = 16

SSM_GROUP = 16
SSM_STATE = 64
SSM_CHUNK = 16

N_EXPERTS = 16
CAPACITY_FACTOR = 2
SLOT_WIN = 128
SLOT_ALIGN = 16
MOE_TILE = 512

PERM_TILE = 1024
VMEM_LIMIT = 56 * 1024 * 1024


def _params(*sem):
    return pltpu.CompilerParams(dimension_semantics=sem, vmem_limit_bytes=VMEM_LIMIT)


def _rms(x, gain):
    var = jnp.mean(x * x, axis=-1, keepdims=True)
    return x * lax.rsqrt(var + NORM_EPS) * gain


def _qkv_body(xn_ref, wq_ref, wk_ref, wv_ref, qkg_ref, cos_ref, sin_ref, o_ref, slab_ref, *, d):
    tm = xn_ref.shape[0]
    n = tm // d
    xn = xn_ref[...]
    cos = cos_ref[...]
    sin = sin_ref[...]
    for part, w_ref in enumerate((wq_ref, wk_ref, wv_ref)):
        acc = jnp.dot(xn, w_ref[...], preferred_element_type=F32)
        for hs in range(HEADS_PER_GROUP):
            a = acc[:, hs * HEAD_DIM:(hs + 1) * HEAD_DIM]
            if part < 2:
                a = _rms(a, qkg_ref[part])
                a = a * cos + pltpu.roll(a, HEAD_DIM // 2, 1) * sin
            cols = slice(part * GROUP_WIDTH + hs * HEAD_DIM, part * GROUP_WIDTH + (hs + 1) * HEAD_DIM)
            if d == 1:
                o_ref[:, cols] = a.astype(BF16)
            else:
                slab_ref[part, hs] = a
                for r in range(d):
                    o_ref[r * n:(r + 1) * n, cols] = slab_ref[part, hs, pl.ds(r, n, stride=d), :].astype(BF16)


def _qkv_proj(xn, w_qk, w_in, layer, gi, qk_gain, cos, sin):
    t, dm = xn.shape
    tm, tn = PERM_TILE, GROUP_WIDTH
    ng = len(DILATIONS)

    def weight(part):
        return pl.BlockSpec((None, dm, tn), lambda i: (layer, 0, part * ng + gi))

    return pl.pallas_call(
        functools.partial(_qkv_body, d=DILATIONS[gi]),
        grid=(t // tm,),
        in_specs=[
            pl.BlockSpec((tm, dm), lambda i: (i, 0)),
            weight(0), weight(1), weight(2),
            pl.BlockSpec((2, 1, HEAD_DIM), lambda i: (0, 0, 0)),
            pl.BlockSpec((tm, HEAD_DIM), lambda i: (i, 0)),
            pl.BlockSpec((tm, HEAD_DIM), lambda i: (i, 0)),
        ],
        out_specs=pl.BlockSpec((tm, 3 * tn), lambda i: (i, 0)),
        out_shape=jax.ShapeDtypeStruct((t, 3 * tn), BF16),
        scratch_shapes=[pltpu.VMEM((3, HEADS_PER_GROUP, tm, HEAD_DIM), F32)],
        compiler_params=_params("parallel"),
        name=f"qkv_proj_d{DILATIONS[gi]}",
    )(xn, w_qk, w_qk, w_in, qk_gain, cos, sin)


def _norm_mm_body(h_ref, g_ref, w_ref, o_ref, xn_out_ref, xn_ref):
    @pl.when(pl.program_id(1) == 0)
    def _():
        xn = _rms(h_ref[...], g_ref[...]).astype(BF16)
        xn_ref[...] = xn
        xn_out_ref[...] = xn

    o_ref[...] = jnp.dot(xn_ref[...], w_ref[...], preferred_element_type=F32).astype(o_ref.dtype)


def _norm_matmul(h, gain, w, layer, col0, ncols, *, out_dtype, tn=512, name):
    t, dm = h.shape
    tm = 1024
    nj = ncols // tn
    blk0 = col0 // tn
    return pl.pallas_call(
        _norm_mm_body,
        grid=(t // tm, nj),
        in_specs=[
            pl.BlockSpec((tm, dm), lambda i, j: (i, 0)),
            pl.BlockSpec((1, dm), lambda i, j: (0, 0)),
            pl.BlockSpec((None, dm, tn), lambda i, j: (layer, 0, blk0 + j)),
        ],
        out_specs=[pl.BlockSpec((tm, tn), lambda i, j: (i, j)),
                   pl.BlockSpec((tm, dm), lambda i, j: (i, 0))],
        out_shape=[jax.ShapeDtypeStruct((t, nj * tn), out_dtype), jax.ShapeDtypeStruct((t, dm), BF16)],
        scratch_shapes=[pltpu.VMEM((tm, dm), BF16)],
        compiler_params=_params("parallel", "arbitrary"),
        name=name,
    )(h, gain, w)


def _act_mm_body(x_ref, w_ref, o_ref):
    o_ref[...] = jax.nn.sigmoid(jnp.dot(x_ref[...], w_ref[...], preferred_element_type=F32)).astype(o_ref.dtype)


def _sigmoid_matmul(x, w, layer, col0, ncols, *, tn=512, name):
    t, dm = x.shape
    tm = 1024
    nj = ncols // tn
    blk0 = col0 // tn
    return pl.pallas_call(
        _act_mm_body,
        grid=(t // tm, nj),
        in_specs=[
            pl.BlockSpec((tm, dm), lambda i, j: (i, 0)),
            pl.BlockSpec((None, dm, tn), lambda i, j: (layer, 0, blk0 + j)),
        ],
        out_specs=pl.BlockSpec((tm, tn), lambda i, j: (i, j)),
        out_shape=jax.ShapeDtypeStruct((t, nj * tn), BF16),
        compiler_params=_params("parallel", "arbitrary"),
        name=name,
    )(x, w)


def _attn_body(q_ref, kp_ref, km_ref, kn_ref, vp_ref, vm_ref, vn_ref, o_ref, l_ref,
               kw_ref, vw_ref, os_ref, ls_ref, *, tq, sub_len):
    c = pl.program_id(2)
    kw_ref[0:N_SIDE] = kp_ref[...]
    kw_ref[N_SIDE:N_SIDE + tq] = km_ref[...].reshape(tq, GROUP_WIDTH)
    kw_ref[N_SIDE + tq:] = kn_ref[...]
    vw_ref[0:N_SIDE] = vp_ref[...]
    vw_ref[N_SIDE:N_SIDE + tq] = vm_ref[...].reshape(tq, GROUP_WIDTH)
    vw_ref[N_SIDE + tq:] = vn_ref[...]
    q = q_ref[...].reshape(tq, GROUP_WIDTH)

    sb = min(128, tq)
    nk = sb + 2 * N_SIDE
    scale = HEAD_DIM ** -0.5
    row = lax.broadcasted_iota(I32, (sb, nk), 0)
    col = lax.broadcasted_iota(I32, (sb, nk), 1)
    band = jnp.abs(col - row - N_SIDE) <= N_SIDE
    for i in range(tq // sb):
        kpos = c * tq + (i * sb - N_SIDE) + col
        valid = band & (kpos >= 0) & (kpos < sub_len)
        for hs in range(HEADS_PER_GROUP):
            lanes = slice(hs * HEAD_DIM, (hs + 1) * HEAD_DIM)
            qh = q[i * sb:(i + 1) * sb, lanes]
            kh = kw_ref[i * sb:i * sb + nk, lanes]
            vh = vw_ref[i * sb:i * sb + nk, lanes]
            s = lax.dot_general(qh, kh, (((1,), (1,)), ((), ())), preferred_element_type=F32) * scale
            s = jnp.where(valid, s, MASK_VALUE)
            m = jnp.max(s, axis=-1, keepdims=True)
            e = jnp.exp(s - m)
            den = jnp.sum(e, axis=-1, keepdims=True)
            o = jnp.dot((e / den).astype(BF16), vh, preferred_element_type=F32)
            os_ref[i * sb:(i + 1) * sb, lanes] = o
            ls_ref[i * sb:(i + 1) * sb, lanes] = jnp.broadcast_to(m + jnp.log(den), (sb, HEAD_DIM))
    o_ref[...] = os_ref[...].reshape(o_ref.shape)
    l_ref[...] = ls_ref[...].reshape(l_ref.shape)


def _attention(qkv, gi, bsz, seq):
    d = DILATIONS[gi]
    t = bsz * seq
    sub_len = seq // d
    nbt = PERM_TILE // (N_SIDE * d)
    ntile = seq // PERM_TILE
    tq = min(512, sub_len)
    nbq = tq // N_SIDE
    nblk = sub_len // N_SIDE
    view = (bsz, ntile, d, nbt, N_SIDE, qkv.shape[1])
    oview = (bsz, ntile, d, nbt, N_SIDE, GROUP_WIDTH)
    gw = GROUP_WIDTH

    if nbt >= nbq:
        per = nbt // nbq
        main_shape = (None, None, None, nbq, N_SIDE, gw)

        def main_idx(col):
            return lambda b, r, c: (b, c // per, r, c % per, 0, col)
    else:
        main_shape = (None, nbq // nbt, None, nbt, N_SIDE, gw)

        def main_idx(col):
            return lambda b, r, c: (b, c, r, 0, 0, col)

    halo_shape = (None, None, None, None, N_SIDE, gw)

    def prev_idx(col):
        def f(b, r, c):
            n = jnp.maximum(c * nbq - 1, 0)
            return (b, n // nbt, r, n % nbt, 0, col)
        return f

    def next_idx(col):
        def f(b, r, c):
            n = jnp.minimum((c + 1) * nbq, nblk - 1)
            return (b, n // nbt, r, n % nbt, 0, col)
        return f

    qc, kc, vc = 0, 1, 2
    x = qkv.reshape(view)
    o, l = pl.pallas_call(
        functools.partial(_attn_body, tq=tq, sub_len=sub_len),
        grid=(bsz, d, sub_len // tq),
        in_specs=[
            pl.BlockSpec(main_shape, main_idx(qc)),
            pl.BlockSpec(halo_shape, prev_idx(kc)),
            pl.BlockSpec(main_shape, main_idx(kc)),
            pl.BlockSpec(halo_shape, next_idx(kc)),
            pl.BlockSpec(halo_shape, prev_idx(vc)),
            pl.BlockSpec(main_shape, main_idx(vc)),
            pl.BlockSpec(halo_shape, next_idx(vc)),
        ],
        out_specs=[pl.BlockSpec(main_shape, main_idx(0)), pl.BlockSpec(main_shape, main_idx(0))],
        out_shape=[jax.ShapeDtypeStruct(oview, F32), jax.ShapeDtypeStruct(oview, F32)],
        scratch_shapes=[pltpu.VMEM((tq + 2 * N_SIDE, gw), BF16), pltpu.VMEM((tq + 2 * N_SIDE, gw), BF16),
                        pltpu.VMEM((tq, gw), F32), pltpu.VMEM((tq, gw), F32)],
        compiler_params=_params("parallel", "parallel", "arbitrary"),
        name=f"dilated_attn_d{d}",
    )(x, x, x, x, x, x, x)
    return o.reshape(t, gw), l.reshape(t, gw)


def _attn_merge_body(o0, l0, o1, l1, o2, l2, w_ref, g_ref, out_ref, comb_ref, so_ref, sl_ref):
    tm = out_ref.shape[0]

    @pl.when(pl.program_id(1) == 0)
    def _():
        for gi, (o_ref, l_ref) in enumerate(((o0, l0), (o1, l1), (o2, l2))):
            d = DILATIONS[gi]
            n = tm // d
            for hs in range(HEADS_PER_GROUP):
                lanes = slice(hs * HEAD_DIM, (hs + 1) * HEAD_DIM)
                for r in range(d):
                    ov = o_ref[r * n:(r + 1) * n, lanes]
                    lv = l_ref[r * n:(r + 1) * n, lanes]
                    if d == 1:
                        so_ref[gi, hs] = ov
                        sl_ref[gi, hs] = lv
                    else:
                        so_ref[gi, hs, pl.ds(r, n, stride=d), :] = ov
                        sl_ref[gi, hs, pl.ds(r, n, stride=d), :] = lv
        for hs in range(HEADS_PER_GROUP):
            ls = [sl_ref[gi, hs] for gi in range(3)]
            mx = jnp.maximum(jnp.maximum(ls[0], ls[1]), ls[2])
            ws = [jnp.exp(l - mx) for l in ls]
            num = ws[0] * so_ref[0, hs] + ws[1] * so_ref[1, hs] + ws[2] * so_ref[2, hs]
            comb = num / (ws[0] + ws[1] + ws[2])
            comb_ref[:, hs * HEAD_DIM:(hs + 1) * HEAD_DIM] = comb.astype(BF16)

    acc = jnp.dot(comb_ref[...], w_ref[...], preferred_element_type=F32)
    out_ref[...] = (acc * g_ref[...].astype(F32)).astype(out_ref.dtype)


def _attn_merge_proj(outs, lses, w_attn, layer, gates):
    t = outs[0].shape[0]
    dm = w_attn.shape[2]
    tm, tn = PERM_TILE, 512
    row = pl.BlockSpec((tm, GROUP_WIDTH), lambda i, j: (i, 0))
    return pl.pallas_call(
        _attn_merge_body,
        grid=(t // tm, dm // tn),
        in_specs=[row, row, row, row, row, row,
                  pl.BlockSpec((None, GROUP_WIDTH, tn), lambda i, j: (layer, 0, j)),
                  pl.BlockSpec((tm, tn), lambda i, j: (i, j))],
        out_specs=pl.BlockSpec((tm, tn), lambda i, j: (i, j)),
        out_shape=jax.ShapeDtypeStruct((t, dm), BF16),
        scratch_shapes=[pltpu.VMEM((tm, GROUP_WIDTH), BF16),
                        pltpu.VMEM((3, HEADS_PER_GROUP, tm, HEAD_DIM), F32),
                        pltpu.VMEM((3, HEADS_PER_GROUP, tm, HEAD_DIM), F32)],
        compiler_params=_params("parallel", "arbitrary"),
        name="attn_merge_proj",
    )(outs[0], lses[0], outs[1], lses[1], outs[2], lses[2], w_attn, gates)


def _ssm_tables(a_re, a_im, log_dt, b_re, b_im, c_re, c_im, n_steps):
    hi = lax.Precision.HIGHEST
    lc = SSM_CHUNK
    a_re, a_im = a_re.astype(F32), a_im.astype(F32)
    dt = jnp.exp(log_dt.astype(F32))[..., None]

    def lam_pow(n):
        n = n.astype(F32)[:, None, None, None]
        mag = jnp.exp(a_re * dt * n)
        ang = a_im * dt * n
        return mag * jnp.cos(ang), mag * jnp.sin(ang)

    one_re, one_im = lam_pow(jnp.ones((1,)))
    lb_re, lb_im = one_re[0], one_im[0]
    den = a_re * a_re + a_im * a_im
    nr, ni = lb_re - 1.0, lb_im
    coef_re = ((nr * a_re + ni * a_im) / den)[..., None]
    coef_im = ((ni * a_re - nr * a_im) / den)[..., None]
    b_re, b_im = b_re.astype(F32), b_im.astype(F32)
    bb_re = coef_re * b_re - coef_im * b_im
    bb_im = coef_re * b_im + coef_im * b_re
    c_re, c_im = c_re.astype(F32), c_im.astype(F32)

    pw_re, pw_im = lam_pow(jnp.arange(lc + 1))
    e_re = pw_re[..., None] * bb_re - pw_im[..., None] * bb_im
    e_im = pw_re[..., None] * bb_im + pw_im[..., None] * bb_re
    kern = (jnp.einsum('dgop,tdgpi->tdgoi', c_re, e_re, precision=hi)
            - jnp.einsum('dgop,tdgpi->tdgoi', c_im, e_im, precision=hi))
    s_idx = jnp.arange(lc)[:, None]
    t_idx = jnp.arange(lc)[None, :]
    lag_f = jnp.clip(t_idx - s_idx, 0, lc)
    lag_b = jnp.clip(s_idx - t_idx, 0, lc)
    kf = jnp.where((t_idx >= s_idx)[..., None, None, None], kern[lag_f, 0], 0.0)
    kb = jnp.where((s_idx >= t_idx)[..., None, None, None], kern[lag_b, 1], 0.0)
    toep = jnp.transpose(kf + kb, (2, 0, 4, 1, 3))
    g = toep.shape[0]
    toep = toep.reshape(g, lc * SSM_GROUP, lc * SSM_GROUP)

    def state_in(e, direction, taus):
        return jnp.transpose(e[taus, direction], (1, 0, 3, 2)).reshape(g, lc * SSM_GROUP, SSM_STATE)

    tau_f = lc - 1 - jnp.arange(lc)
    tau_b = jnp.arange(lc)
    q_in = jnp.concatenate([state_in(e_re, 0, tau_f), state_in(e_im, 0, tau_f),
                            state_in(e_re, 1, tau_b), state_in(e_im, 1, tau_b)], axis=-1)
    tq = jnp.concatenate([toep, q_in], axis=-1)

    def state_out(direction, taus):
        lr = pw_re[taus, direction][:, :, None, :]
        li = pw_im[taus, direction][:, :, None, :]
        cr, ci = c_re[direction][None], c_im[direction][None]
        mr = cr * lr - ci * li
        mi = cr * li + ci * lr
        to_rows = lambda m: jnp.transpose(m, (1, 3, 0, 2)).reshape(g, SSM_STATE, lc * SSM_GROUP)
        return jnp.concatenate([to_rows(mr), to_rows(-mi)], axis=1)

    pm = jnp.concatenate([state_out(0, jnp.arange(lc) + 1), state_out(1, lc - jnp.arange(lc))], axis=1)

    sc_re, sc_im = lam_pow(lc * (2 ** jnp.arange(n_steps)))
    return tq.astype(BF16), pm.astype(BF16), jnp.stack([sc_re, sc_im], axis=2)


def _ssm_compact(tq, pm, sc):
    g = tq.shape[0]
    gps = LANES // SSM_GROUP
    ns, npair = g // gps, gps // 2
    lc, hg, p = SSM_CHUNK, SSM_GROUP, SSM_STATE
    kw = lc * hg
    side = lc * LANES

    def rows_sgh(m):
        return m.reshape(ns, gps, lc, hg, m.shape[-1]).transpose(0, 2, 1, 3, 4).reshape(ns, side, m.shape[-1])

    c_sout = pm.reshape(ns, npair, 2, 2, 2, p, kw).transpose(0, 3, 1, 4, 2, 5, 6).reshape(ns, side, kw)
    comp = jnp.stack([rows_sgh(tq[..., :kw]), rows_sgh(tq[..., kw:]), c_sout], axis=1)
    n_steps = sc.shape[0]
    tab = sc.reshape(n_steps, 2, 2, ns, npair, 2, p).transpose(3, 1, 4, 0, 2, 5, 6)
    return comp, tab.reshape(ns, 2 * npair * n_steps * 2, 2 * p)


def _ssm_expand_constants():
    lc, hg, p = SSM_CHUNK, SSM_GROUP, SSM_STATE
    r = jnp.arange(lc * hg)[:, None]
    c = jnp.arange(lc * LANES)[None, :]
    e_time = (r // hg == c // LANES) & (r % hg == c % hg)
    e_state = (r // (2 * p) == c // (lc * LANES // 2)) & ((r // p) % 2 == (c // LANES) % 2) & (r % p == c % p)
    rr = jnp.arange(lc * LANES)[:, None]
    lane_group = lambda i: (i % LANES) // hg
    state_group = lambda i: 2 * ((i // (2 * LANES)) % (LANES // hg // 2)) + (i % LANES) // p
    masks = [lane_group(rr) == lane_group(c), lane_group(rr) == state_group(c), state_group(rr) == lane_group(c)]
    return (jnp.stack([e_time, e_state, e_time]).astype(BF16), jnp.stack(masks).astype(BF16))


def _ssm_expand_body(c_ref, e_ref, m_ref, o_ref):
    rows = 512
    for r in range(0, o_ref.shape[0], rows):
        acc = jnp.dot(c_ref[r:r + rows], e_ref[...], preferred_element_type=F32)
        o_ref[r:r + rows] = jnp.where(m_ref[r:r + rows] > 0, acc, 0.0).astype(o_ref.dtype)


def _ssm_expand(comp):
    depth, ns, three, side, kw = comp.shape
    e, mask = _ssm_expand_constants()
    return pl.pallas_call(
        _ssm_expand_body,
        grid=(three, depth * ns),
        in_specs=[
            pl.BlockSpec((None, None, None, side, kw), lambda m, n: (n // ns, n % ns, m, 0, 0)),
            pl.BlockSpec((None, kw, side), lambda m, n: (m, 0, 0)),
            pl.BlockSpec((None, side, side), lambda m, n: (m, 0, 0)),
        ],
        out_specs=pl.BlockSpec((None, None, None, side, side), lambda m, n: (n // ns, n % ns, m, 0, 0)),
        out_shape=jax.ShapeDtypeStruct((depth, ns, three, side, side), BF16),
        compiler_params=_params("arbitrary", "arbitrary"),
        name="s5_expand_weights",
    )(comp, e, mask)


def _ssm_body(u_ref, toe_ref, sin_ref, sout_ref, tab_ref, y_ref, lhs_ref, inj_ref, st_ref, *, n_steps):
    seq = u_ref.shape[0]
    lc = SSM_CHUNK
    nc = seq // lc
    n_slabs = inj_ref.shape[0]
    per_dir = n_slabs // 4
    row = lax.broadcasted_iota(I32, (nc, LANES), 0)

    for s in range(lc):
        lhs_ref[:, s * LANES:(s + 1) * LANES] = u_ref[pl.ds(s, nc, stride=lc), :].astype(BF16)
    lhs = lhs_ref[...]

    for j in range(0, n_slabs, 2):
        r = jnp.dot(lhs, sin_ref[:, j * LANES:(j + 2) * LANES], preferred_element_type=F32)
        inj_ref[j] = r[:, :LANES]
        inj_ref[j + 1] = r[:, LANES:]

    def shifted(x, sh, direction):
        if sh % 8 == 0:
            zeros = jnp.zeros((sh, LANES), x.dtype)
            if direction == 0:
                return jnp.concatenate([zeros, x[:nc - sh]], axis=0)
            return jnp.concatenate([x[sh:], zeros], axis=0)
        if direction == 0:
            return jnp.where(row >= sh, pltpu.roll(x, sh, 0), 0.0)
        return jnp.where(row < nc - sh, pltpu.roll(x, nc - sh, 0), 0.0)

    for direction in range(2):
        def scan_pair(jq, carry, direction=direction):
            xr = inj_ref[2 * jq]
            xi = inj_ref[2 * jq + 1]
            for k in range(n_steps):
                base = (jq * n_steps + k) * 2
                ar = tab_ref[pl.ds(base, 1), :]
                ai = tab_ref[pl.ds(base + 1, 1), :]
                sr = shifted(xr, 1 << k, direction)
                si = shifted(xi, 1 << k, direction)
                xr, xi = xr + ar * sr - ai * si, xi + ar * si + ai * sr
            st_ref[2 * jq] = shifted(xr, 1, direction).astype(BF16)
            st_ref[2 * jq + 1] = shifted(xi, 1, direction).astype(BF16)
            return carry

        lax.fori_loop(direction * per_dir, (direction + 1) * per_dir, scan_pair, 0)

    states = jnp.concatenate([st_ref[j] for j in range(n_slabs)], axis=1)
    for t in range(0, lc, 2):
        cols = slice(t * LANES, (t + 2) * LANES)
        r = (jnp.dot(lhs, toe_ref[:, cols], preferred_element_type=F32)
             + jnp.dot(states, sout_ref[:, cols], preferred_element_type=F32))
        y_ref[pl.ds(t, nc, stride=lc), :] = r[:, :LANES]
        y_ref[pl.ds(t + 1, nc, stride=lc), :] = r[:, LANES:]


def _ssm_scan(u, w_slab, tab, layer, bsz, seq):
    t, width = u.shape
    ns = width // LANES
    nc = seq // SSM_CHUNK
    n_steps = int(math.log2(nc))
    side = SSM_CHUNK * LANES
    n_slabs = side // LANES
    once = pl.Buffered(1)

    def weight(m):
        return pl.BlockSpec((None, None, None, side, side), lambda g, b: (layer, g, m, 0, 0), pipeline_mode=once)

    return pl.pallas_call(
        functools.partial(_ssm_body, n_steps=n_steps),
        grid=(ns, bsz),
        in_specs=[
            pl.BlockSpec((seq, LANES), lambda g, b: (b, g)),
            weight(0), weight(1), weight(2),
            pl.BlockSpec((None, None, tab.shape[-2], LANES), lambda g, b: (layer, g, 0, 0)),
        ],
        out_specs=pl.BlockSpec((seq, LANES), lambda g, b: (b, g)),
        out_shape=jax.ShapeDtypeStruct((t, width), F32),
        scratch_shapes=[pltpu.VMEM((nc, side), BF16),
                        pltpu.VMEM((n_slabs, nc, LANES), F32),
                        pltpu.VMEM((n_slabs, nc, LANES), BF16)],
        compiler_params=_params("arbitrary", "arbitrary"),
        name="s5_chunk_scan",
    )(u, w_slab, w_slab, w_slab, tab)


def _ssm_glu_body(y_ref, u_ref, d_ref, wa_ref, wb_ref, ga_ref, gs_ref, o_ref, act_ref):
    @pl.when(pl.program_id(1) == 0)
    def _():
        act_ref[...] = jax.nn.gelu(y_ref[...] + d_ref[...] * u_ref[...]).astype(BF16)

    act = act_ref[...]
    za = jnp.dot(act, wa_ref[...], preferred_element_type=F32)
    zb = jnp.dot(act, wb_ref[...], preferred_element_type=F32)
    s_branch = za * jax.nn.sigmoid(zb)
    o_ref[...] = (ga_ref[...].astype(F32) + gs_ref[...].astype(F32) * s_branch).astype(o_ref.dtype)


def _ssm_glu_merge(y, u, d_skip, w_ssm, layer, gated_attn, gates):
    t, width = y.shape
    dm = gated_attn.shape[1]
    tm, tn = 1024, 512
    nj = dm // tn
    return pl.pallas_call(
        _ssm_glu_body,
        grid=(t // tm, nj),
        in_specs=[
            pl.BlockSpec((tm, width), lambda i, j: (i, 0)),
            pl.BlockSpec((tm, width), lambda i, j: (i, 0)),
            pl.BlockSpec((1, width), lambda i, j: (0, 0)),
            pl.BlockSpec((None, width, tn), lambda i, j: (layer, 0, j)),
            pl.BlockSpec((None, width, tn), lambda i, j: (layer, 0, j + nj)),
            pl.BlockSpec((tm, tn), lambda i, j: (i, j)),
            pl.BlockSpec((tm, tn), lambda i, j: (i, j + nj)),
        ],
        out_specs=pl.BlockSpec((tm, tn), lambda i, j: (i, j)),
        out_shape=jax.ShapeDtypeStruct((t, dm), BF16),
        scratch_shapes=[pltpu.VMEM((tm, width), BF16)],
        compiler_params=_params("parallel", "arbitrary"),
        name="s5_glu_merge",
    )(y, u, d_skip, w_ssm, w_ssm, gated_attn, gates)


def _resid_mm_body(x_ref, w_ref, h_ref, o_ref):
    o_ref[...] = h_ref[...] + jnp.dot(x_ref[...], w_ref[...], preferred_element_type=F32)


def _resid_matmul(x, w, layer, h):
    t, k = x.shape
    dm = w.shape[2]
    tm, tn = 1024, 512
    return pl.pallas_call(
        _resid_mm_body,
        grid=(t // tm, dm // tn),
        in_specs=[
            pl.BlockSpec((tm, k), lambda i, j: (i, 0)),
            pl.BlockSpec((None, k, tn), lambda i, j: (layer, 0, j)),
            pl.BlockSpec((tm, tn), lambda i, j: (i, j)),
        ],
        out_specs=pl.BlockSpec((tm, tn), lambda i, j: (i, j)),
        out_shape=jax.ShapeDtypeStruct((t, dm), F32),
        compiler_params=_params("parallel", "arbitrary"),
        name="out_proj",
    )(x, w, h)


def _router_body(h_ref, g_ref, wr_ref, xn_ref, lg_ref):
    xn = _rms(h_ref[...], g_ref[...])
    xn_ref[...] = xn.astype(BF16)
    lg_ref[...] = lax.dot_general(wr_ref[...], xn, (((1,), (1,)), ((), ())),
                                  precision=lax.Precision.HIGHEST, preferred_element_type=F32)


def _router_logits(h, gain, w_router_t):
    t, dm = h.shape
    tm = 512
    return pl.pallas_call(
        _router_body,
        grid=(t // tm,),
        in_specs=[
            pl.BlockSpec((tm, dm), lambda i: (i, 0)),
            pl.BlockSpec((1, dm), lambda i: (0, 0)),
            pl.BlockSpec((N_EXPERTS, dm), lambda i: (0, 0)),
        ],
        out_specs=[pl.BlockSpec((tm, dm), lambda i: (i, 0)),
                   pl.BlockSpec((N_EXPERTS, tm), lambda i: (0, i))],
        out_shape=[jax.ShapeDtypeStruct((t, dm), BF16), jax.ShapeDtypeStruct((N_EXPERTS, t), F32)],
        compiler_params=_params("parallel"),
        name="router_logits",
    )(h, gain, w_router_t)


def _select_body(lg_ref, gate_ref, pos_ref, sel_ref, *, cap):
    lg = lg_ref[...]
    ne, s = lg.shape
    m = jnp.max(lg, axis=0, keepdims=True)
    e = jnp.exp(lg - m)
    aff = e / jnp.sum(e, axis=0, keepdims=True)
    bits = lax.bitcast_convert_type(aff, I32)

    def count(ind):
        return jnp.sum(ind, axis=1, keepdims=True)

    def value_bit(k, thr):
        cand = thr | jnp.left_shift(jnp.int32(1), 30 - k)
        return jnp.where(count(jnp.where(bits >= cand, 1.0, 0.0)) >= cap, cand, thr)

    thr = lax.fori_loop(0, 31, value_bit, jnp.zeros((ne, 1), I32))
    above = jnp.where(bits > thr, 1.0, 0.0)
    tie = jnp.where(bits == thr, 1.0, 0.0)
    need = cap - count(above)
    idx = lax.broadcasted_iota(I32, (ne, s), 1)
    n_bits = int(math.log2(s)) + 1

    def index_bit(k, bound):
        cand = bound | jnp.left_shift(jnp.int32(1), n_bits - 1 - k)
        below = count(jnp.where(idx < cand, tie, 0.0))
        return jnp.where(below < need, cand, bound)

    bound = lax.fori_loop(0, n_bits, index_bit, jnp.zeros((ne, 1), I32))
    sel = above + jnp.where(idx <= bound, tie, 0.0)
    gate_ref[...] = sel * aff
    sel_ref[...] = sel.astype(I32)

    blk = 256
    tri = jnp.where(lax.broadcasted_iota(I32, (blk, blk), 0) <= lax.broadcasted_iota(I32, (blk, blk), 1),
                    1.0, 0.0).astype(BF16)
    carry = jnp.zeros((ne, 1), F32)
    for j in range(s // blk):
        seg = sel[:, j * blk:(j + 1) * blk]
        inc = jnp.dot(seg.astype(BF16), tri, preferred_element_type=F32)
        pos_ref[:, j * blk:(j + 1) * blk] = (inc - seg + carry).astype(I32)
        carry = carry + inc[:, blk - 1:blk]


def _select_tokens(logits_t, bsz, seq):
    cap = CAPACITY_FACTOR * seq // N_EXPERTS
    t = bsz * seq
    spec = pl.BlockSpec((N_EXPERTS, seq), lambda b: (0, b))
    return pl.pallas_call(
        functools.partial(_select_body, cap=cap),
        grid=(bsz,),
        in_specs=[spec],
        out_specs=[spec, spec, spec],
        out_shape=[jax.ShapeDtypeStruct((N_EXPERTS, t), F32), jax.ShapeDtypeStruct((N_EXPERTS, t), I32),
                   jax.ShapeDtypeStruct((N_EXPERTS, t), I32)],
        compiler_params=_params("parallel"),
        name="expert_choice_select",
    )(logits_t)


def _slot_window(base_ref, b, e, i, nt, cap):
    k = (b * N_EXPERTS + e) * (nt + 1) + i
    lo = base_ref[k]
    start = jnp.minimum((lo // SLOT_ALIGN) * SLOT_ALIGN, cap - SLOT_WIN)
    return pl.multiple_of(start, SLOT_ALIGN), base_ref[k + 1]


def _n_extra_windows(start, hi):
    return (jnp.maximum(hi - (start + SLOT_WIN), 0) + SLOT_WIN - 1) // SLOT_WIN


def _extra_window(start, k, cap):
    first = start + SLOT_WIN * k
    return first, pl.multiple_of(jnp.minimum(first, cap - SLOT_WIN), SLOT_ALIGN)


def _dispatch_body(base_ref, x_ref, slot_ref, out_ref, *, nt, cap):
    b = pl.program_id(0)
    i = pl.program_id(2)
    ts = x_ref.shape[0]

    @pl.when(i == 0)
    def _():
        out_ref[...] = jnp.zeros_like(out_ref)

    x = x_ref[...]
    wio = lax.broadcasted_iota(I32, (SLOT_WIN, ts), 0)
    wins = [_slot_window(base_ref, b, e, i, nt, cap) for e in range(N_EXPERTS)]
    onehot = jnp.concatenate(
        [jnp.where(slot_ref[e:e + 1, :] - wins[e][0] == wio, 1.0, 0.0).astype(x.dtype) for e in range(N_EXPERTS)],
        axis=0)
    res = jnp.dot(onehot, x, preferred_element_type=F32)
    for e in range(N_EXPERTS):
        out_ref[e, pl.ds(wins[e][0], SLOT_WIN), :] += res[e * SLOT_WIN:(e + 1) * SLOT_WIN].astype(out_ref.dtype)

    for e in range(N_EXPERTS):
        start, hi = wins[e]

        def extra(k, carry, e=e, start=start):
            first, st = _extra_window(start, k, cap)
            srow = slot_ref[e:e + 1, :]
            srow = jnp.where(srow >= first, srow, -1)
            oh = jnp.where(srow - st == wio, 1.0, 0.0).astype(x.dtype)
            out_ref[e, pl.ds(st, SLOT_WIN), :] += jnp.dot(oh, x, preferred_element_type=F32).astype(out_ref.dtype)
            return carry

        lax.fori_loop(1, 1 + _n_extra_windows(start, hi), extra, 0)


def _dispatch(base, x, slot_t, bsz, seq, cap, *, cw, out_dtype, name):
    t, width = x.shape
    ts = MOE_TILE
    nt = seq // ts
    grid_spec = pltpu.PrefetchScalarGridSpec(
        num_scalar_prefetch=1,
        grid=(bsz, width // cw, nt),
        in_specs=[
            pl.BlockSpec((ts, cw), lambda b, c, i, base: (b * nt + i, c)),
            pl.BlockSpec((N_EXPERTS, ts), lambda b, c, i, base: (0, b * nt + i)),
        ],
        out_specs=pl.BlockSpec((None, N_EXPERTS, cap, cw), lambda b, c, i, base: (b, 0, 0, c)),
    )
    return pl.pallas_call(
        functools.partial(_dispatch_body, nt=nt, cap=cap),
        grid_spec=grid_spec,
        out_shape=jax.ShapeDtypeStruct((bsz, N_EXPERTS, cap, width), out_dtype),
        compiler_params=_params("parallel", "parallel", "arbitrary"),
        name=name,
    )(base, x, slot_t)


def _expert_body(x_ref, wg_ref, wu_ref, wd_ref, gs_ref, y_ref, acc_ref):
    e = pl.program_id(0)
    f = pl.program_id(2)
    x = x_ref[...]
    hid = (jax.nn.silu(jnp.dot(x, wg_ref[...], preferred_element_type=F32))
           * jnp.dot(x, wu_ref[...], preferred_element_type=F32)).astype(BF16)
    part = jnp.dot(hid, wd_ref[...], preferred_element_type=F32)

    @pl.when(f == 0)
    def _():
        acc_ref[...] = part

    @pl.when(f > 0)
    def _():
        acc_ref[...] += part

    @pl.when(f == pl.num_programs(2) - 1)
    def _():
        pieces = gs_ref[...]
        lane = lax.broadcasted_iota(I32, pieces.shape, 1)
        mine = (lane >= 3 * e) & (lane < 3 * e + 3)
        gate = jnp.sum(jnp.where(mine, pieces, 0.0), axis=1, keepdims=True)
        y_ref[...] = (acc_ref[...] * gate).astype(y_ref.dtype)


def _expert_ffn(xg, w_gate, w_up, w_down, layer, gate_slots):
    bsz, ne, cap, dm = xg.shape
    ff = w_gate.shape[3]
    tf = 512
    return pl.pallas_call(
        _expert_body,
        grid=(ne, bsz, ff // tf),
        in_specs=[
            pl.BlockSpec((None, None, cap, dm), lambda e, b, f: (b, e, 0, 0)),
            pl.BlockSpec((None, None, dm, tf), lambda e, b, f: (layer, e, 0, f)),
            pl.BlockSpec((None, None, dm, tf), lambda e, b, f: (layer, e, 0, f)),
            pl.BlockSpec((None, None, tf, dm), lambda e, b, f: (layer, e, f, 0)),
            pl.BlockSpec((None, None, cap, 128), lambda e, b, f: (b, e, 0, 0)),
        ],
        out_specs=pl.BlockSpec((None, None, cap, dm), lambda e, b, f: (b, e, 0, 0)),
        out_shape=jax.ShapeDtypeStruct((bsz, ne, cap, dm), BF16),
        scratch_shapes=[pltpu.VMEM((cap, dm), F32)],
        compiler_params=_params("parallel", "parallel", "arbitrary"),
        name="expert_swiglu",
    )(xg, w_gate, w_up, w_down, gate_slots)


def _combine_body(base_ref, yg_hbm, h_ref, slot_ref, out_ref, wbuf, xbuf, wsem, xsem, *, nt, cap, n_steps):
    n = pl.program_id(0)
    ts = h_ref.shape[0]

    def window_copies(step, half):
        b = step // nt
        i = step % nt
        copies = []
        for e in range(N_EXPERTS):
            start, _ = _slot_window(base_ref, b, e, i, nt, cap)
            copies.append(pltpu.make_async_copy(yg_hbm.at[b, e, pl.ds(start, SLOT_WIN), :],
                                                wbuf.at[half, e], wsem.at[half]))
        return copies

    @pl.when(n == 0)
    def _():
        for c in window_copies(0, 0):
            c.start()

    @pl.when(n + 1 < n_steps)
    def _():
        for c in window_copies(n + 1, (n + 1) % 2):
            c.start()

    b = n // nt
    i = n % nt
    half = n % 2
    lane = lax.broadcasted_iota(I32, (ts, SLOT_WIN), 1)
    wins = [_slot_window(base_ref, b, e, i, nt, cap) for e in range(N_EXPERTS)]
    onehot = jnp.concatenate(
        [jnp.where(slot_ref[:, e:e + 1] - wins[e][0] == lane, 1.0, 0.0).astype(BF16) for e in range(N_EXPERTS)],
        axis=1)
    for c in window_copies(n, half):
        c.wait()
    rows = wbuf[half].reshape(N_EXPERTS * SLOT_WIN, wbuf.shape[-1])
    out_ref[...] = h_ref[...] + jnp.dot(onehot, rows, preferred_element_type=F32)

    for e in range(N_EXPERTS):
        start, hi = wins[e]

        def extra(k, carry, e=e, start=start):
            first, st = _extra_window(start, k, cap)
            copy = pltpu.make_async_copy(yg_hbm.at[b, e, pl.ds(st, SLOT_WIN), :], xbuf, xsem)
            copy.start()
            scol = slot_ref[:, e:e + 1]
            scol = jnp.where(scol >= first, scol, -1)
            oh = jnp.where(scol - st == lane, 1.0, 0.0).astype(BF16)
            copy.wait()
            out_ref[...] += jnp.dot(oh, xbuf[...], preferred_element_type=F32)
            return carry

        lax.fori_loop(1, 1 + _n_extra_windows(start, hi), extra, 0)


def _combine(base, yg, h, slot_tok, bsz, seq):
    t, dm = h.shape
    cap = yg.shape[2]
    ts = MOE_TILE
    nt = seq // ts
    n_steps = bsz * nt
    grid_spec = pltpu.PrefetchScalarGridSpec(
        num_scalar_prefetch=1,
        grid=(n_steps,),
        in_specs=[
            pl.BlockSpec(memory_space=pl.ANY),
            pl.BlockSpec((ts, dm), lambda n, base: (n, 0)),
            pl.BlockSpec((ts, N_EXPERTS), lambda n, base: (n, 0)),
        ],
        out_specs=pl.BlockSpec((ts, dm), lambda n, base: (n, 0)),
        scratch_shapes=[pltpu.VMEM((2, N_EXPERTS, SLOT_WIN, dm), BF16),
                        pltpu.VMEM((SLOT_WIN, dm), BF16),
                        pltpu.SemaphoreType.DMA((2,)),
                        pltpu.SemaphoreType.DMA(())],
    )
    return pl.pallas_call(
        functools.partial(_combine_body, nt=nt, cap=cap, n_steps=n_steps),
        grid_spec=grid_spec,
        out_shape=jax.ShapeDtypeStruct((t, dm), F32),
        compiler_params=_params("arbitrary"),
        name="moe_combine",
    )(base, yg, h, slot_tok)


def _moe(h, gain, w_router_t, w_gate, w_up, w_down, layer, bsz, seq):
    cap = CAPACITY_FACTOR * seq // N_EXPERTS
    xn, logits_t = _router_logits(h, gain, w_router_t)
    gate_t, pos_t, sel_t = _select_tokens(logits_t, bsz, seq)

    slot_t = jnp.where(sel_t > 0, pos_t, -1)
    ts = MOE_TILE
    nt = seq // ts
    starts = pos_t.reshape(N_EXPERTS, bsz, nt, ts)[..., 0]
    base = jnp.concatenate([jnp.transpose(starts, (1, 0, 2)),
                            jnp.full((bsz, N_EXPERTS, 1), cap, I32)], axis=-1).reshape(-1)
    gate_tok = gate_t.T
    g_hi = gate_tok.astype(BF16)
    r1 = gate_tok - g_hi.astype(F32)
    g_mid = r1.astype(BF16)
    g_lo = (r1 - g_mid.astype(F32)).astype(BF16)
    pieces = jnp.stack([g_hi, g_mid, g_lo], axis=-1).reshape(-1, 3 * N_EXPERTS)
    pieces = jnp.pad(pieces, ((0, 0), (0, 128 - 3 * N_EXPERTS)))

    xg = _dispatch(base, xn, slot_t, bsz, seq, cap, cw=512, out_dtype=BF16, name="moe_dispatch")
    gate_slots = _dispatch(base, pieces, slot_t, bsz, seq, cap, cw=128, out_dtype=F32, name="moe_gate_dispatch")
    yg = _expert_ffn(xg, w_gate, w_up, w_down, layer, gate_slots)
    return _combine(base, yg, h, slot_t.T, bsz, seq)


def _ple_body(h_ref, g_ref, wg_ref, p_ref, wp_ref, hres_ref, o_ref, xn_ref):
    @pl.when(pl.program_id(1) == 0)
    def _():
        xn_ref[...] = _rms(h_ref[...], g_ref[...]).astype(BF16)

    gate = jax.nn.sigmoid(jnp.dot(xn_ref[...], wg_ref[...], preferred_element_type=F32))
    ple = jnp.dot(p_ref[...].astype(BF16), wp_ref[...], preferred_element_type=F32)
    o_ref[...] = hres_ref[...] + gate * ple


def _ple(h, gain, w_gate, p, w_proj, layer):
    t, dm = h.shape
    tm, tn = 1024, 512
    pd = p.shape[2]
    return pl.pallas_call(
        _ple_body,
        grid=(t // tm, dm // tn),
        in_specs=[
            pl.BlockSpec((tm, dm), lambda i, j: (i, 0)),
            pl.BlockSpec((1, dm), lambda i, j: (0, 0)),
            pl.BlockSpec((None, dm, tn), lambda i, j: (layer, 0, j)),
            pl.BlockSpec((None, tm, pd), lambda i, j: (layer, i, 0)),
            pl.BlockSpec((None, pd, tn), lambda i, j: (layer, 0, j)),
            pl.BlockSpec((tm, tn), lambda i, j: (i, j)),
        ],
        out_specs=pl.BlockSpec((tm, tn), lambda i, j: (i, j)),
        out_shape=jax.ShapeDtypeStruct((t, dm), F32),
        scratch_shapes=[pltpu.VMEM((tm, dm), BF16)],
        compiler_params=_params("parallel", "arbitrary"),
        name="ple_gate",
    )(h, gain, w_gate, p, w_proj, h)


def _rotary_lane_order(a):
    blocks = HEAD_DIM // ROT_HALF
    order = list(range(blocks))
    order[1], order[blocks // 2] = order[blocks // 2], order[1]
    shaped = a.reshape(a.shape[:-1] + (a.shape[-1] // HEAD_DIM, blocks, ROT_HALF))
    return shaped[..., jnp.array(order), :].reshape(a.shape)


def _rope_tables(positions):
    inv_freq = jnp.power(ROPE_THETA, -jnp.arange(ROT_HALF, dtype=F32) * 2.0 / (2 * ROT_HALF))
    ang = positions.astype(F32)[..., None] * inv_freq
    cos, sin = jnp.cos(ang), jnp.sin(ang)
    gap = HEAD_DIM // 2 - ROT_HALF
    ones = jnp.ones(ang.shape[:-1] + (gap,), F32)
    zeros = jnp.zeros(ang.shape[:-1] + (gap,), F32)
    cos_t = jnp.concatenate([cos, ones, cos, ones], axis=-1)
    sin_t = jnp.concatenate([-sin, zeros, sin, zeros], axis=-1)
    return cos_t.reshape(-1, HEAD_DIM), sin_t.reshape(-1, HEAD_DIM)


def kernel(x, p, positions, norm_mix, w_in, q_norm, k_norm, w_attn_br, ssm_a_re, ssm_a_im, ssm_log_dt,
           ssm_b_re, ssm_b_im, ssm_c_re, ssm_c_im, ssm_d, w_ssm_br, w_out, norm_ffn, w_router,
           w_exp_gate, w_exp_up, w_exp_down, norm_ple, w_ple_gate, w_ple_proj):
    bsz, seq, dm = x.shape
    depth = w_in.shape[0]
    t = bsz * seq
    n_attn = 3 * HEADS_PER_GROUP * len(DILATIONS) * HEAD_DIM
    ssm_width = ssm_d.shape[1]
    nc = seq // SSM_CHUNK
    n_steps = int(math.log2(nc))
    assert seq % PERM_TILE == 0 and nc == 1 << n_steps

    cos_t, sin_t = _rope_tables(positions)
    w_in_b, w_attn_b, w_ssm_b, w_out_b = (w.astype(BF16) for w in (w_in, w_attn_br, w_ssm_br, w_out))
    w_eg_b, w_eu_b, w_ed_b = (w.astype(BF16) for w in (w_exp_gate, w_exp_up, w_exp_down))
    w_pg_b, w_pp_b = w_ple_gate.astype(BF16), w_ple_proj.astype(BF16)
    w_qk_b = _rotary_lane_order(w_in_b[:, :, :2 * n_attn // 3])
    tq, pm, sc = jax.vmap(functools.partial(_ssm_tables, n_steps=n_steps))(
        ssm_a_re, ssm_a_im, ssm_log_dt, ssm_b_re, ssm_b_im, ssm_c_re, ssm_c_im)
    comp, tabs = jax.vmap(_ssm_compact)(tq, pm, sc)
    w_slab = _ssm_expand(comp)
    p_rows = p.reshape(depth, t, p.shape[-1])
    h = x.reshape(t, dm)

    for l in range(depth):
        gain = norm_mix[l][None]
        qk_gain = _rotary_lane_order(jnp.stack([q_norm[l], k_norm[l]]))[:, None, :]

        u, xn = _norm_matmul(h, gain, w_in_b, l, n_attn, ssm_width, out_dtype=F32, name="u_proj")
        gates = _sigmoid_matmul(xn, w_in_b, l, n_attn + ssm_width, 2 * dm, name="gate_proj")
        outs, lses = zip(*[_attention(_qkv_proj(xn, w_qk_b, w_in_b, l, gi, qk_gain, cos_t, sin_t), gi, bsz, seq)
                           for gi in range(len(DILATIONS))])
        gated_attn = _attn_merge_proj(outs, lses, w_attn_b, l, gates)

        y = _ssm_scan(u, w_slab, tabs, l, bsz, seq)
        merged = _ssm_glu_merge(y, u, ssm_d[l][None], w_ssm_b, l, gated_attn, gates)
        h = _resid_matmul(merged, w_out_b, l, h)

        h = _moe(h, norm_ffn[l][None], w_router[l].T, w_eg_b, w_eu_b, w_ed_b, l, bsz, seq)
        h = _ple(h, norm_ple[l][None], w_pg_b, p_rows, w_pp_b, l)
    return h.reshape(bsz, seq, dm)
```

```python
import functools
import math

import jax
import jax.numpy as jnp
from jax import lax
from jax.experimental import pallas as pl
from jax.experimental.pallas import tpu as pltpu

F32 = jnp.float32
BF16 = jnp.bfloat16
I32 = jnp.int32

NORM_EPS = 1e-6
MASK_VALUE = -1e30
ROPE_THETA = 500000.0

LANES = 128
HEAD_DIM = 128
HEADS_PER_GROUP = 4
GROUP_WIDTH = HEADS_PER_GROUP * HEAD_DIM
DILATIONS = (1, 4, 16)
N_SIDE = 64
ROT_HALF = 16

SSM_GROUP = 16
SSM_STATE = 64
SSM_CHUNK = 16

N_EXPERTS = 16
CAPACITY_FACTOR = 2
SLOT_WIN = 128
SLOT_ALIGN = 16
MOE_TILE = 512

PERM_TILE = 1024
VMEM_LIMIT = 56 * 1024 * 1024


def _params(*sem):
    return pltpu.CompilerParams(dimension_semantics=sem, vmem_limit_bytes=VMEM_LIMIT)


def _rms(x, gain):
    var = jnp.mean(x * x, axis=-1, keepdims=True)
    return x * lax.rsqrt(var + NORM_EPS) * gain


def _qkv_body(xn_ref, wq_ref, wk_ref, wv_ref, qkg_ref, cos_ref, sin_ref, o_ref, slab_ref, *, d):
    tm = xn_ref.shape[0]
    n = tm // d
    xn = xn_ref[...]
    cos = cos_ref[...]
    sin = sin_ref[...]
    for part, w_ref in enumerate((wq_ref, wk_ref, wv_ref)):
        acc = jnp.dot(xn, w_ref[...], preferred_element_type=F32)
        for hs in range(HEADS_PER_GROUP):
            a = acc[:, hs * HEAD_DIM:(hs + 1) * HEAD_DIM]
            if part < 2:
                a = _rms(a, qkg_ref[part])
                a = a * cos + pltpu.roll(a, HEAD_DIM // 2, 1) * sin
            cols = slice(part * GROUP_WIDTH + hs * HEAD_DIM, part * GROUP_WIDTH + (hs + 1) * HEAD_DIM)
            if d == 1:
                o_ref[:, cols] = a.astype(BF16)
            else:
                slab_ref[part, hs] = a
                for r in range(d):
                    o_ref[r * n:(r + 1) * n, cols] = slab_ref[part, hs, pl.ds(r, n, stride=d), :].astype(BF16)


def _qkv_proj(xn, w_qk, w_in, layer, gi, qk_gain, cos, sin):
    t, dm = xn.shape
    tm, tn = PERM_TILE, GROUP_WIDTH
    ng = len(DILATIONS)

    def weight(part):
        return pl.BlockSpec((None, dm, tn), lambda i: (layer, 0, part * ng + gi))

    return pl.pallas_call(
        functools.partial(_qkv_body, d=DILATIONS[gi]),
        grid=(t // tm,),
        in_specs=[
            pl.BlockSpec((tm, dm), lambda i: (i, 0)),
            weight(0), weight(1), weight(2),
            pl.BlockSpec((2, 1, HEAD_DIM), lambda i: (0, 0, 0)),
            pl.BlockSpec((tm, HEAD_DIM), lambda i: (i, 0)),
            pl.BlockSpec((tm, HEAD_DIM), lambda i: (i, 0)),
        ],
        out_specs=pl.BlockSpec((tm, 3 * tn), lambda i: (i, 0)),
        out_shape=jax.ShapeDtypeStruct((t, 3 * tn), BF16),
        scratch_shapes=[pltpu.VMEM((3, HEADS_PER_GROUP, tm, HEAD_DIM), F32)],
        compiler_params=_params("parallel"),
        name=f"qkv_proj_d{DILATIONS[gi]}",
    )(xn, w_qk, w_qk, w_in, qk_gain, cos, sin)


def _norm_mm_body(h_ref, g_ref, w_ref, o_ref, xn_out_ref, xn_ref):
    @pl.when(pl.program_id(1) == 0)
    def _():
        xn = _rms(h_ref[...], g_ref[...]).astype(BF16)
        xn_ref[...] = xn
        xn_out_ref[...] = xn

    o_ref[...] = jnp.dot(xn_ref[...], w_ref[...], preferred_element_type=F32).astype(o_ref.dtype)


def _norm_matmul(h, gain, w, layer, col0, ncols, *, out_dtype, tn=512, name):
    t, dm = h.shape
    tm = 1024
    nj = ncols // tn
    blk0 = col0 // tn
    return pl.pallas_call(
        _norm_mm_body,
        grid=(t // tm, nj),
        in_specs=[
            pl.BlockSpec((tm, dm), lambda i, j: (i, 0)),
            pl.BlockSpec((1, dm), lambda i, j: (0, 0)),
            pl.BlockSpec((None, dm, tn), lambda i, j: (layer, 0, blk0 + j)),
        ],
        out_specs=[pl.BlockSpec((tm, tn), lambda i, j: (i, j)),
                   pl.BlockSpec((tm, dm), lambda i, j: (i, 0))],
        out_shape=[jax.ShapeDtypeStruct((t, nj * tn), out_dtype), jax.ShapeDtypeStruct((t, dm), BF16)],
        scratch_shapes=[pltpu.VMEM((tm, dm), BF16)],
        compiler_params=_params("parallel", "arbitrary"),
        name=name,
    )(h, gain, w)


def _act_mm_body(x_ref, w_ref, o_ref):
    o_ref[...] = jax.nn.sigmoid(jnp.dot(x_ref[...], w_ref[...], preferred_element_type=F32)).astype(o_ref.dtype)


def _sigmoid_matmul(x, w, layer, col0, ncols, *, tn=512, name):
    t, dm = x.shape
    tm = 1024
    nj = ncols // tn
    blk0 = col0 // tn
    return pl.pallas_call(
        _act_mm_body,
        grid=(t // tm, nj),
        in_specs=[
            pl.BlockSpec((tm, dm), lambda i, j: (i, 0)),
            pl.BlockSpec((None, dm, tn), lambda i, j: (layer, 0, blk0 + j)),
        ],
        out_specs=pl.BlockSpec((tm, tn), lambda i, j: (i, j)),
        out_shape=jax.ShapeDtypeStruct((t, nj * tn), BF16),
        compiler_params=_params("parallel", "arbitrary"),
        name=name,
    )(x, w)


def _attn_body(q_ref, kp_ref, km_ref, kn_ref, vp_ref, vm_ref, vn_ref, o_ref, l_ref,
               kw_ref, vw_ref, os_ref, ls_ref, *, tq, sub_len):
    c = pl.program_id(2)
    kw_ref[0:N_SIDE] = kp_ref[...]
    kw_ref[N_SIDE:N_SIDE + tq] = km_ref[...].reshape(tq, GROUP_WIDTH)
    kw_ref[N_SIDE + tq:] = kn_ref[...]
    vw_ref[0:N_SIDE] = vp_ref[...]
    vw_ref[N_SIDE:N_SIDE + tq] = vm_ref[...].reshape(tq, GROUP_WIDTH)
    vw_ref[N_SIDE + tq:] = vn_ref[...]
    q = q_ref[...].reshape(tq, GROUP_WIDTH)

    sb = min(128, tq)
    nk = sb + 2 * N_SIDE
    scale = HEAD_DIM ** -0.5
    row = lax.broadcasted_iota(I32, (sb, nk), 0)
    col = lax.broadcasted_iota(I32, (sb, nk), 1)
    band = jnp.abs(col - row - N_SIDE) <= N_SIDE
    for i in range(tq // sb):
        kpos = c * tq + (i * sb - N_SIDE) + col
        valid = band & (kpos >= 0) & (kpos < sub_len)
        for hs in range(HEADS_PER_GROUP):
            lanes = slice(hs * HEAD_DIM, (hs + 1) * HEAD_DIM)
            qh = q[i * sb:(i + 1) * sb, lanes]
            kh = kw_ref[i * sb:i * sb + nk, lanes]
            vh = vw_ref[i * sb:i * sb + nk, lanes]
            s = lax.dot_general(qh, kh, (((1,), (1,)), ((), ())), preferred_element_type=F32) * scale
            s = jnp.where(valid, s, MASK_VALUE)
            m = jnp.max(s, axis=-1, keepdims=True)
            e = jnp.exp(s - m)
            den = jnp.sum(e, axis=-1, keepdims=True)
            o = jnp.dot((e / den).astype(BF16), vh, preferred_element_type=F32)
            os_ref[i * sb:(i + 1) * sb, lanes] = o
            ls_ref[i * sb:(i + 1) * sb, lanes] = jnp.broadcast_to(m + jnp.log(den), (sb, HEAD_DIM))
    o_ref[...] = os_ref[...].reshape(o_ref.shape)
    l_ref[...] = ls_ref[...].reshape(l_ref.shape)


def _attention(qkv, gi, bsz, seq):
    d = DILATIONS[gi]
    t = bsz * seq
    sub_len = seq // d
    nbt = PERM_TILE // (N_SIDE * d)
    ntile = seq // PERM_TILE
    tq = min(512, sub_len)
    nbq = tq // N_SIDE
    nblk = sub_len // N_SIDE
    view = (bsz, ntile, d, nbt, N_SIDE, qkv.shape[1])
    oview = (bsz, ntile, d, nbt, N_SIDE, GROUP_WIDTH)
    gw = GROUP_WIDTH

    if nbt >= nbq:
        per = nbt // nbq
        main_shape = (None, None, None, nbq, N_SIDE, gw)

        def main_idx(col):
            return lambda b, r, c: (b, c // per, r, c % per, 0, col)
    else:
        main_shape = (None, nbq // nbt, None, nbt, N_SIDE, gw)

        def main_idx(col):
            return lambda b, r, c: (b, c, r, 0, 0, col)

    halo_shape = (None, None, None, None, N_SIDE, gw)

    def prev_idx(col):
        def f(b, r, c):
            n = jnp.maximum(c * nbq - 1, 0)
            return (b, n // nbt, r, n % nbt, 0, col)
        return f

    def next_idx(col):
        def f(b, r, c):
            n = jnp.minimum((c + 1) * nbq, nblk - 1)
            return (b, n // nbt, r, n % nbt, 0, col)
        return f

    qc, kc, vc = 0, 1, 2
    x = qkv.reshape(view)
    o, l = pl.pallas_call(
        functools.partial(_attn_body, tq=tq, sub_len=sub_len),
        grid=(bsz, d, sub_len // tq),
        in_specs=[
            pl.BlockSpec(main_shape, main_idx(qc)),
            pl.BlockSpec(halo_shape, prev_idx(kc)),
            pl.BlockSpec(main_shape, main_idx(kc)),
            pl.BlockSpec(halo_shape, next_idx(kc)),
            pl.BlockSpec(halo_shape, prev_idx(vc)),
            pl.BlockSpec(main_shape, main_idx(vc)),
            pl.BlockSpec(halo_shape, next_idx(vc)),
        ],
        out_specs=[pl.BlockSpec(main_shape, main_idx(0)), pl.BlockSpec(main_shape, main_idx(0))],
        out_shape=[jax.ShapeDtypeStruct(oview, F32), jax.ShapeDtypeStruct(oview, F32)],
        scratch_shapes=[pltpu.VMEM((tq + 2 * N_SIDE, gw), BF16), pltpu.VMEM((tq + 2 * N_SIDE, gw), BF16),
                        pltpu.VMEM((tq, gw), F32), pltpu.VMEM((tq, gw), F32)],
        compiler_params=_params("parallel", "parallel", "arbitrary"),
        name=f"dilated_attn_d{d}",
    )(x, x, x, x, x, x, x)
    return o.reshape(t, gw), l.reshape(t, gw)


def _attn_merge_body(o0, l0, o1, l1, o2, l2, w_ref, g_ref, out_ref, comb_ref, so_ref, sl_ref):
    tm = out_ref.shape[0]

    @pl.when(pl.program_id(1) == 0)
    def _():
        for gi, (o_ref, l_ref) in enumerate(((o0, l0), (o1, l1), (o2, l2))):
            d = DILATIONS[gi]
            n = tm // d
            for hs in range(HEADS_PER_GROUP):
                lanes = slice(hs * HEAD_DIM, (hs + 1) * HEAD_DIM)
                for r in range(d):
                    ov = o_ref[r * n:(r + 1) * n, lanes]
                    lv = l_ref[r * n:(r + 1) * n, lanes]
                    if d == 1:
                        so_ref[gi, hs] = ov
                        sl_ref[gi, hs] = lv
                    else:
                        so_ref[gi, hs, pl.ds(r, n, stride=d), :] = ov
                        sl_ref[gi, hs, pl.ds(r, n, stride=d), :] = lv
        for hs in range(HEADS_PER_GROUP):
            ls = [sl_ref[gi, hs] for gi in range(3)]
            mx = jnp.maximum(jnp.maximum(ls[0], ls[1]), ls[2])
            ws = [jnp.exp(l - mx) for l in ls]
            num = ws[0] * so_ref[0, hs] + ws[1] * so_ref[1, hs] + ws[2] * so_ref[2, hs]
            comb = num / (ws[0] + ws[1] + ws[2])
            comb_ref[:, hs * HEAD_DIM:(hs + 1) * HEAD_DIM] = comb.astype(BF16)

    acc = jnp.dot(comb_ref[...], w_ref[...], preferred_element_type=F32)
    out_ref[...] = (acc * g_ref[...].astype(F32)).astype(out_ref.dtype)


def _attn_merge_proj(outs, lses, w_attn, layer, gates):
    t = outs[0].shape[0]
    dm = w_attn.shape[2]
    tm, tn = PERM_TILE, 512
    row = pl.BlockSpec((tm, GROUP_WIDTH), lambda i, j: (i, 0))
    return pl.pallas_call(
        _attn_merge_body,
        grid=(t // tm, dm // tn),
        in_specs=[row, row, row, row, row, row,
                  pl.BlockSpec((None, GROUP_WIDTH, tn), lambda i, j: (layer, 0, j)),
                  pl.BlockSpec((tm, tn), lambda i, j: (i, j))],
        out_specs=pl.BlockSpec((tm, tn), lambda i, j: (i, j)),
        out_shape=jax.ShapeDtypeStruct((t, dm), BF16),
        scratch_shapes=[pltpu.VMEM((tm, GROUP_WIDTH), BF16),
                        pltpu.VMEM((3, HEADS_PER_GROUP, tm, HEAD_DIM), F32),
                        pltpu.VMEM((3, HEADS_PER_GROUP, tm, HEAD_DIM), F32)],
        compiler_params=_params("parallel", "arbitrary"),
        name="attn_merge_proj",
    )(outs[0], lses[0], outs[1], lses[1], outs[2], lses[2], w_attn, gates)


def _ssm_fold_body(sp_re, sp_im, si_re, si_im, sot_re, sot_im, bb_re, bb_im, c_re, c_im, ct_re, ct_im,
                   rep_ref, tile_ref, o_ref):
    lc, hg, p = SSM_CHUNK, SSM_GROUP, SSM_STATE
    gps = LANES // hg
    kw = lc * hg
    hi_prec = lax.Precision.HIGHEST
    nt = (((1,), (1,)), ((), ()))
    for g in range(gps):
        strip = jnp.zeros((hg, 2 * kw), F32)
        injections = []
        for d in range(2):
            br, bi = bb_re[d, g], bb_im[d, g]
            cr, ci = c_re[d, g], c_im[d, g]
            pr, pi = sp_re[d, g][:, None, :], sp_im[d, g][:, None, :]
            y_re = (pr * cr[None] - pi * ci[None]).reshape(2 * kw, p)
            y_im = (pr * ci[None] + pi * cr[None]).reshape(2 * kw, p)
            strip = strip + lax.dot_general(jnp.concatenate([br, -bi], axis=1), jnp.concatenate([y_re, y_im], axis=1),
                                            nt, precision=hi_prec, preferred_element_type=F32)
            qr, qi = si_re[d, g][:, None, :], si_im[d, g][:, None, :]
            injections += [(qr * br[None] - qi * bi[None]).reshape(kw, p),
                           (qr * bi[None] + qi * br[None]).reshape(kw, p)]
            ar = jnp.dot(sot_re[d, g], rep_ref[...], precision=hi_prec, preferred_element_type=F32)
            ai = jnp.dot(sot_im[d, g], rep_ref[...], precision=hi_prec, preferred_element_type=F32)
            xr = jnp.dot(ct_re[d, g], tile_ref[...], precision=hi_prec, preferred_element_type=F32)
            xi = jnp.dot(ct_im[d, g], tile_ref[...], precision=hi_prec, preferred_element_type=F32)
            row0 = ((d * (gps // 2) + g // 2) * 2) * LANES + (g % 2) * p
            o_ref[2, row0:row0 + p, :] = (ar * xr - ai * xi).astype(o_ref.dtype)
            o_ref[2, row0 + LANES:row0 + LANES + p, :] = (-(ar * xi + ai * xr)).astype(o_ref.dtype)
        state_in = jnp.concatenate(injections, axis=1)
        for s in range(lc):
            rows = slice(s * LANES + g * hg, s * LANES + (g + 1) * hg)
            o_ref[0, rows, :] = strip[:, (lc - 1 - s) * hg:(lc - 1 - s) * hg + kw].astype(o_ref.dtype)
            o_ref[1, rows, :] = state_in[s * hg:(s + 1) * hg, :].astype(o_ref.dtype)


def _ssm_fold(a_re, a_im, log_dt, b_re, b_im, c_re, c_im, n_steps):
    f32 = lambda v: v.astype(F32)
    a_re, a_im, b_re, b_im, c_re, c_im = map(f32, (a_re, a_im, b_re, b_im, c_re, c_im))
    depth, _, g, p = a_re.shape
    lc, hg = SSM_CHUNK, SSM_GROUP
    gps = LANES // hg
    ns, npair = g // gps, gps // 2
    kw = lc * hg
    side = lc * LANES
    dt = jnp.exp(f32(log_dt))[..., None]

    def lam_pow(n):
        n = n.astype(F32)[:, None]
        mag = jnp.exp((a_re * dt)[..., None, :] * n)
        ang = (a_im * dt)[..., None, :] * n
        return mag * jnp.cos(ang), mag * jnp.sin(ang)

    pw_re, pw_im = lam_pow(jnp.arange(lc + 1))
    lb_re, lb_im = pw_re[..., 1, :], pw_im[..., 1, :]
    den = a_re * a_re + a_im * a_im
    nr, ni = lb_re - 1.0, lb_im
    coef_re = ((nr * a_re + ni * a_im) / den)[..., None, :]
    coef_im = ((ni * a_re - nr * a_im) / den)[..., None, :]
    bt_re, bt_im = jnp.swapaxes(b_re, -1, -2), jnp.swapaxes(b_im, -1, -2)
    bb_re = coef_re * bt_re - coef_im * bt_im
    bb_im = coef_re * bt_im + coef_im * bt_re

    def per_dir(pw, fwd, bwd):
        return jnp.stack([pw[:, 0][..., fwd, :], pw[:, 1][..., bwd, :]], axis=1)

    steps = jnp.arange(lc)
    slot = jnp.arange(2 * lc)
    f_on = ((slot >= lc - 1) & (slot < 2 * lc - 1)).astype(F32)[:, None]
    b_on = (slot <= lc - 1).astype(F32)[:, None]
    f_idx = jnp.clip(slot - (lc - 1), 0, lc)
    b_idx = jnp.clip(lc - 1 - slot, 0, lc)
    strip_pw = lambda pw: jnp.stack([pw[:, 0][..., f_idx, :] * f_on, pw[:, 1][..., b_idx, :] * b_on], axis=1)
    sp_re, sp_im = strip_pw(pw_re), strip_pw(pw_im)
    si_re, si_im = per_dir(pw_re, lc - 1 - steps, steps), per_dir(pw_im, lc - 1 - steps, steps)
    so_re, so_im = per_dir(pw_re, steps + 1, lc - steps), per_dir(pw_im, steps + 1, lc - steps)
    sot_re, sot_im = jnp.swapaxes(so_re, -1, -2), jnp.swapaxes(so_im, -1, -2)
    ct_re, ct_im = jnp.swapaxes(c_re, -1, -2), jnp.swapaxes(c_im, -1, -2)
    col = jnp.arange(kw)[None, :]
    rep = (jnp.arange(lc)[:, None] == col // hg).astype(F32)
    tile = (jnp.arange(hg)[:, None] == col % hg).astype(F32)

    def rows_spec(rows):
        return pl.BlockSpec((None, 2, gps, rows, p), lambda l, s: (l, 0, s, 0, 0))

    def cols_spec(cols):
        return pl.BlockSpec((None, 2, gps, p, cols), lambda l, s: (l, 0, s, 0, 0))

    const = pl.BlockSpec((lc, kw), lambda l, s: (0, 0))
    comp = pl.pallas_call(
        _ssm_fold_body,
        grid=(depth, ns),
        in_specs=[rows_spec(2 * lc), rows_spec(2 * lc), rows_spec(lc), rows_spec(lc), cols_spec(lc), cols_spec(lc),
                  rows_spec(hg), rows_spec(hg), rows_spec(hg), rows_spec(hg), cols_spec(hg), cols_spec(hg),
                  const, const],
        out_specs=pl.BlockSpec((None, None, 3, side, kw), lambda l, s: (l, s, 0, 0, 0)),
        out_shape=jax.ShapeDtypeStruct((depth, ns, 3, side, kw), BF16),
        compiler_params=_params("parallel", "parallel"),
        name="s5_fold_params",
    )(sp_re, sp_im, si_re, si_im, sot_re, sot_im, bb_re, bb_im, c_re, c_im, ct_re, ct_im, rep, tile)

    sc_re, sc_im = lam_pow(lc * (2 ** jnp.arange(n_steps)))
    tab = jnp.stack([sc_re, sc_im], axis=-2).reshape(depth, 2, ns, npair, 2, n_steps, 2, p)
    tab = jnp.transpose(tab, (0, 2, 1, 3, 5, 6, 4, 7)).reshape(depth, ns, 2 * npair * n_steps * 2, 2 * p)
    return comp, tab


def _ssm_expand_constants():
    lc, hg, p = SSM_CHUNK, SSM_GROUP, SSM_STATE
    r = jnp.arange(lc * hg)[:, None]
    c = jnp.arange(lc * LANES)[None, :]
    e_time = (r // hg == c // LANES) & (r % hg == c % hg)
    e_state = (r // (2 * p) == c // (lc * LANES // 2)) & ((r // p) % 2 == (c // LANES) % 2) & (r % p == c % p)
    rr = jnp.arange(lc * LANES)[:, None]
    lane_group = lambda i: (i % LANES) // hg
    state_group = lambda i: 2 * ((i // (2 * LANES)) % (LANES // hg // 2)) + (i % LANES) // p
    masks = [lane_group(rr) == lane_group(c), lane_group(rr) == state_group(c), state_group(rr) == lane_group(c)]
    return (jnp.stack([e_time, e_state, e_time]).astype(BF16), jnp.stack(masks).astype(BF16))


def _ssm_expand_body(c_ref, e_ref, m_ref, o_ref):
    rows = 512
    for r in range(0, o_ref.shape[0], rows):
        acc = jnp.dot(c_ref[r:r + rows], e_ref[...], preferred_element_type=F32)
        o_ref[r:r + rows] = jnp.where(m_ref[r:r + rows] > 0, acc, 0.0).astype(o_ref.dtype)


def _ssm_expand(comp):
    depth, ns, three, side, kw = comp.shape
    e, mask = _ssm_expand_constants()
    return pl.pallas_call(
        _ssm_expand_body,
        grid=(three, depth * ns),
        in_specs=[
            pl.BlockSpec((None, None, None, side, kw), lambda m, n: (n // ns, n % ns, m, 0, 0)),
            pl.BlockSpec((None, kw, side), lambda m, n: (m, 0, 0)),
            pl.BlockSpec((None, side, side), lambda m, n: (m, 0, 0)),
        ],
        out_specs=pl.BlockSpec((None, None, None, side, side), lambda m, n: (n // ns, n % ns, m, 0, 0)),
        out_shape=jax.ShapeDtypeStruct((depth, ns, three, side, side), BF16),
        compiler_params=_params("arbitrary", "arbitrary"),
        name="s5_expand_weights",
    )(comp, e, mask)


def _ssm_body(u_ref, toe_ref, sin_ref, sout_ref, tab_ref, y_ref, lhs_ref, inj_ref, st_ref, *, n_steps):
    seq = u_ref.shape[0]
    lc = SSM_CHUNK
    nc = seq // lc
    n_slabs = inj_ref.shape[0]
    per_dir = n_slabs // 4
    row = lax.broadcasted_iota(I32, (nc, LANES), 0)

    for s in range(lc):
        lhs_ref[:, s * LANES:(s + 1) * LANES] = u_ref[pl.ds(s, nc, stride=lc), :].astype(BF16)
    lhs = lhs_ref[...]

    for j in range(0, n_slabs, 2):
        r = jnp.dot(lhs, sin_ref[:, j * LANES:(j + 2) * LANES], preferred_element_type=F32)
        inj_ref[j] = r[:, :LANES]
        inj_ref[j + 1] = r[:, LANES:]

    def shifted(x, sh, direction):
        if sh % 8 == 0:
            zeros = jnp.zeros((sh, LANES), x.dtype)
            if direction == 0:
                return jnp.concatenate([zeros, x[:nc - sh]], axis=0)
            return jnp.concatenate([x[sh:], zeros], axis=0)
        if direction == 0:
            return jnp.where(row >= sh, pltpu.roll(x, sh, 0), 0.0)
        return jnp.where(row < nc - sh, pltpu.roll(x, nc - sh, 0), 0.0)

    for direction in range(2):
        def scan_pair(jq, carry, direction=direction):
            xr = inj_ref[2 * jq]
            xi = inj_ref[2 * jq + 1]
            for k in range(n_steps):
                base = (jq * n_steps + k) * 2
                ar = tab_ref[pl.ds(base, 1), :]
                ai = tab_ref[pl.ds(base + 1, 1), :]
                sr = shifted(xr, 1 << k, direction)
                si = shifted(xi, 1 << k, direction)
                xr, xi = xr + ar * sr - ai * si, xi + ar * si + ai * sr
            st_ref[2 * jq] = shifted(xr, 1, direction).astype(BF16)
            st_ref[2 * jq + 1] = shifted(xi, 1, direction).astype(BF16)
            return carry

        lax.fori_loop(direction * per_dir, (direction + 1) * per_dir, scan_pair, 0)

    states = jnp.concatenate([st_ref[j] for j in range(n_slabs)], axis=1)
    for t in range(0, lc, 2):
        cols = slice(t * LANES, (t + 2) * LANES)
        r = (jnp.dot(lhs, toe_ref[:, cols], preferred_element_type=F32)
             + jnp.dot(states, sout_ref[:, cols], preferred_element_type=F32))
        y_ref[pl.ds(t, nc, stride=lc), :] = r[:, :LANES]
        y_ref[pl.ds(t + 1, nc, stride=lc), :] = r[:, LANES:]


def _ssm_scan(u, w_slab, tab, layer, bsz, seq):
    t, width = u.shape
    ns = width // LANES
    nc = seq // SSM_CHUNK
    n_steps = int(math.log2(nc))
    side = SSM_CHUNK * LANES
    n_slabs = side // LANES
    once = pl.Buffered(1)

    def weight(m):
        return pl.BlockSpec((None, None, None, side, side), lambda g, b: (layer, g, m, 0, 0), pipeline_mode=once)

    return pl.pallas_call(
        functools.partial(_ssm_body, n_steps=n_steps),
        grid=(ns, bsz),
        in_specs=[
            pl.BlockSpec((seq, LANES), lambda g, b: (b, g)),
            weight(0), weight(1), weight(2),
            pl.BlockSpec((None, None, tab.shape[-2], LANES), lambda g, b: (layer, g, 0, 0)),
        ],
        out_specs=pl.BlockSpec((seq, LANES), lambda g, b: (b, g)),
        out_shape=jax.ShapeDtypeStruct((t, width), F32),
        scratch_shapes=[pltpu.VMEM((nc, side), BF16),
                        pltpu.VMEM((n_slabs, nc, LANES), F32),
                        pltpu.VMEM((n_slabs, nc, LANES), BF16)],
        compiler_params=_params("arbitrary", "arbitrary"),
        name="s5_chunk_scan",
    )(u, w_slab, w_slab, w_slab, tab)


def _ssm_glu_body(y_ref, u_ref, d_ref, wa_ref, wb_ref, ga_ref, gs_ref, o_ref, act_ref):
    @pl.when(pl.program_id(1) == 0)
    def _():
        act_ref[...] = jax.nn.gelu(y_ref[...] + d_ref[...] * u_ref[...]).astype(BF16)

    act = act_ref[...]
    za = jnp.dot(act, wa_ref[...], preferred_element_type=F32)
    zb = jnp.dot(act, wb_ref[...], preferred_element_type=F32)
    s_branch = za * jax.nn.sigmoid(zb)
    o_ref[...] = (ga_ref[...].astype(F32) + gs_ref[...].astype(F32) * s_branch).astype(o_ref.dtype)


def _ssm_glu_merge(y, u, d_skip, w_ssm, layer, gated_attn, gates):
    t, width = y.shape
    dm = gated_attn.shape[1]
    tm, tn = 1024, 512
    nj = dm // tn
    return pl.pallas_call(
        _ssm_glu_body,
        grid=(t // tm, nj),
        in_specs=[
            pl.BlockSpec((tm, width), lambda i, j: (i, 0)),
            pl.BlockSpec((tm, width), lambda i, j: (i, 0)),
            pl.BlockSpec((1, width), lambda i, j: (0, 0)),
            pl.BlockSpec((None, width, tn), lambda i, j: (layer, 0, j)),
            pl.BlockSpec((None, width, tn), lambda i, j: (layer, 0, j + nj)),
            pl.BlockSpec((tm, tn), lambda i, j: (i, j)),
            pl.BlockSpec((tm, tn), lambda i, j: (i, j + nj)),
        ],
        out_specs=pl.BlockSpec((tm, tn), lambda i, j: (i, j)),
        out_shape=jax.ShapeDtypeStruct((t, dm), BF16),
        scratch_shapes=[pltpu.VMEM((tm, width), BF16)],
        compiler_params=_params("parallel", "arbitrary"),
        name="s5_glu_merge",
    )(y, u, d_skip, w_ssm, w_ssm, gated_attn, gates)


def _resid_mm_body(x_ref, w_ref, h_ref, o_ref):
    o_ref[...] = h_ref[...] + jnp.dot(x_ref[...], w_ref[...], preferred_element_type=F32)


def _resid_matmul(x, w, layer, h):
    t, k = x.shape
    dm = w.shape[2]
    tm, tn = 1024, 512
    return pl.pallas_call(
        _resid_mm_body,
        grid=(t // tm, dm // tn),
        in_specs=[
            pl.BlockSpec((tm, k), lambda i, j: (i, 0)),
            pl.BlockSpec((None, k, tn), lambda i, j: (layer, 0, j)),
            pl.BlockSpec((tm, tn), lambda i, j: (i, j)),
        ],
        out_specs=pl.BlockSpec((tm, tn), lambda i, j: (i, j)),
        out_shape=jax.ShapeDtypeStruct((t, dm), F32),
        compiler_params=_params("parallel", "arbitrary"),
        name="out_proj",
    )(x, w, h)


def _router_body(h_ref, g_ref, wr_ref, xn_ref, lg_ref):
    xn = _rms(h_ref[...], g_ref[...])
    xn_ref[...] = xn.astype(BF16)
    lg_ref[...] = lax.dot_general(wr_ref[...], xn, (((1,), (1,)), ((), ())),
                                  precision=lax.Precision.HIGHEST, preferred_element_type=F32)


def _router_logits(h, gain, w_router_t):
    t, dm = h.shape
    tm = 512
    return pl.pallas_call(
        _router_body,
        grid=(t // tm,),
        in_specs=[
            pl.BlockSpec((tm, dm), lambda i: (i, 0)),
            pl.BlockSpec((1, dm), lambda i: (0, 0)),
            pl.BlockSpec((N_EXPERTS, dm), lambda i: (0, 0)),
        ],
        out_specs=[pl.BlockSpec((tm, dm), lambda i: (i, 0)),
                   pl.BlockSpec((N_EXPERTS, tm), lambda i: (0, i))],
        out_shape=[jax.ShapeDtypeStruct((t, dm), BF16), jax.ShapeDtypeStruct((N_EXPERTS, t), F32)],
        compiler_params=_params("parallel"),
        name="router_logits",
    )(h, gain, w_router_t)


def _select_body(lg_ref, gate_ref, pos_ref, sel_ref, *, cap):
    lg = lg_ref[...]
    ne, s = lg.shape
    m = jnp.max(lg, axis=0, keepdims=True)
    e = jnp.exp(lg - m)
    aff = e / jnp.sum(e, axis=0, keepdims=True)
    bits = lax.bitcast_convert_type(aff, I32)

    def count(ind):
        return jnp.sum(ind, axis=1, keepdims=True)

    def value_bit(k, thr):
        cand = thr | jnp.left_shift(jnp.int32(1), 30 - k)
        return jnp.where(count(jnp.where(bits >= cand, 1.0, 0.0)) >= cap, cand, thr)

    thr = lax.fori_loop(0, 31, value_bit, jnp.zeros((ne, 1), I32))
    above = jnp.where(bits > thr, 1.0, 0.0)
    tie = jnp.where(bits == thr, 1.0, 0.0)
    need = cap - count(above)
    idx = lax.broadcasted_iota(I32, (ne, s), 1)
    n_bits = int(math.log2(s)) + 1

    def index_bit(k, bound):
        cand = bound | jnp.left_shift(jnp.int32(1), n_bits - 1 - k)
        below = count(jnp.where(idx < cand, tie, 0.0))
        return jnp.where(below < need, cand, bound)

    bound = lax.fori_loop(0, n_bits, index_bit, jnp.zeros((ne, 1), I32))
    sel = above + jnp.where(idx <= bound, tie, 0.0)
    gate_ref[...] = sel * aff
    sel_ref[...] = sel.astype(I32)

    blk = 256
    tri = jnp.where(lax.broadcasted_iota(I32, (blk, blk), 0) <= lax.broadcasted_iota(I32, (blk, blk), 1),
                    1.0, 0.0).astype(BF16)
    carry = jnp.zeros((ne, 1), F32)
    for j in range(s // blk):
        seg = sel[:, j * blk:(j + 1) * blk]
        inc = jnp.dot(seg.astype(BF16), tri, preferred_element_type=F32)
        pos_ref[:, j * blk:(j + 1) * blk] = (inc - seg + carry).astype(I32)
        carry = carry + inc[:, blk - 1:blk]


def _select_tokens(logits_t, bsz, seq):
    cap = CAPACITY_FACTOR * seq // N_EXPERTS
    t = bsz * seq
    spec = pl.BlockSpec((N_EXPERTS, seq), lambda b: (0, b))
    return pl.pallas_call(
        functools.partial(_select_body, cap=cap),
        grid=(bsz,),
        in_specs=[spec],
        out_specs=[spec, spec, spec],
        out_shape=[jax.ShapeDtypeStruct((N_EXPERTS, t), F32), jax.ShapeDtypeStruct((N_EXPERTS, t), I32),
                   jax.ShapeDtypeStruct((N_EXPERTS, t), I32)],
        compiler_params=_params("parallel"),
        name="expert_choice_select",
    )(logits_t)


def _slot_window(base_ref, b, e, i, nt, cap):
    k = (b * N_EXPERTS + e) * (nt + 1) + i
    lo = base_ref[k]
    start = jnp.minimum((lo // SLOT_ALIGN) * SLOT_ALIGN, cap - SLOT_WIN)
    return pl.multiple_of(start, SLOT_ALIGN), base_ref[k + 1]


def _n_extra_windows(start, hi):
    return (jnp.maximum(hi - (start + SLOT_WIN), 0) + SLOT_WIN - 1) // SLOT_WIN


def _extra_window(start, k, cap):
    first = start + SLOT_WIN * k
    return first, pl.multiple_of(jnp.minimum(first, cap - SLOT_WIN), SLOT_ALIGN)


def _dispatch_body(base_ref, x_ref, slot_ref, out_ref, *, nt, cap):
    b = pl.program_id(0)
    i = pl.program_id(2)
    ts = x_ref.shape[0]

    @pl.when(i == 0)
    def _():
        out_ref[...] = jnp.zeros_like(out_ref)

    x = x_ref[...]
    wio = lax.broadcasted_iota(I32, (SLOT_WIN, ts), 0)
    wins = [_slot_window(base_ref, b, e, i, nt, cap) for e in range(N_EXPERTS)]
    onehot = jnp.concatenate(
        [jnp.where(slot_ref[e:e + 1, :] - wins[e][0] == wio, 1.0, 0.0).astype(x.dtype) for e in range(N_EXPERTS)],
        axis=0)
    res = jnp.dot(onehot, x, preferred_element_type=F32)
    for e in range(N_EXPERTS):
        out_ref[e, pl.ds(wins[e][0], SLOT_WIN), :] += res[e * SLOT_WIN:(e + 1) * SLOT_WIN].astype(out_ref.dtype)

    for e in range(N_EXPERTS):
        start, hi = wins[e]

        def extra(k, carry, e=e, start=start):
            first, st = _extra_window(start, k, cap)
            srow = slot_ref[e:e + 1, :]
            srow = jnp.where(srow >= first, srow, -1)
            oh = jnp.where(srow - st == wio, 1.0, 0.0).astype(x.dtype)
            out_ref[e, pl.ds(st, SLOT_WIN), :] += jnp.dot(oh, x, preferred_element_type=F32).astype(out_ref.dtype)
            return carry

        lax.fori_loop(1, 1 + _n_extra_windows(start, hi), extra, 0)


def _dispatch(base, x, slot_t, bsz, seq, cap, *, cw, out_dtype, name):
    t, width = x.shape
    ts = MOE_TILE
    nt = seq // ts
    grid_spec = pltpu.PrefetchScalarGridSpec(
        num_scalar_prefetch=1,
        grid=(bsz, width // cw, nt),
        in_specs=[
            pl.BlockSpec((ts, cw), lambda b, c, i, base: (b * nt + i, c)),
            pl.BlockSpec((N_EXPERTS, ts), lambda b, c, i, base: (0, b * nt + i)),
        ],
        out_specs=pl.BlockSpec((None, N_EXPERTS, cap, cw), lambda b, c, i, base: (b, 0, 0, c)),
    )
    return pl.pallas_call(
        functools.partial(_dispatch_body, nt=nt, cap=cap),
        grid_spec=grid_spec,
        out_shape=jax.ShapeDtypeStruct((bsz, N_EXPERTS, cap, width), out_dtype),
        compiler_params=_params("parallel", "parallel", "arbitrary"),
        name=name,
    )(base, x, slot_t)


def _expert_body(x_ref, wg_ref, wu_ref, wd_ref, gs_ref, y_ref, acc_ref):
    e = pl.program_id(0)
    f = pl.program_id(2)
    x = x_ref[...]
    hid = (jax.nn.silu(jnp.dot(x, wg_ref[...], preferred_element_type=F32))
           * jnp.dot(x, wu_ref[...], preferred_element_type=F32)).astype(BF16)
    part = jnp.dot(hid, wd_ref[...], preferred_element_type=F32)

    @pl.when(f == 0)
    def _():
        acc_ref[...] = part

    @pl.when(f > 0)
    def _():
        acc_ref[...] += part

    @pl.when(f == pl.num_programs(2) - 1)
    def _():
        pieces = gs_ref[...]
        lane = lax.broadcasted_iota(I32, pieces.shape, 1)
        mine = (lane >= 3 * e) & (lane < 3 * e + 3)
        gate = jnp.sum(jnp.where(mine, pieces, 0.0), axis=1, keepdims=True)
        y_ref[...] = (acc_ref[...] * gate).astype(y_ref.dtype)


def _expert_ffn(xg, w_gate, w_up, w_down, layer, gate_slots):
    bsz, ne, cap, dm = xg.shape
    ff = w_gate.shape[3]
    tf = 512
    return pl.pallas_call(
        _expert_body,
        grid=(ne, bsz, ff // tf),
        in_specs=[
            pl.BlockSpec((None, None, cap, dm), lambda e, b, f: (b, e, 0, 0)),
            pl.BlockSpec((None, None, dm, tf), lambda e, b, f: (layer, e, 0, f)),
            pl.BlockSpec((None, None, dm, tf), lambda e, b, f: (layer, e, 0, f)),
            pl.BlockSpec((None, None, tf, dm), lambda e, b, f: (layer, e, f, 0)),
            pl.BlockSpec((None, None, cap, 128), lambda e, b, f: (b, e, 0, 0)),
        ],
        out_specs=pl.BlockSpec((None, None, cap, dm), lambda e, b, f: (b, e, 0, 0)),
        out_shape=jax.ShapeDtypeStruct((bsz, ne, cap, dm), BF16),
        scratch_shapes=[pltpu.VMEM((cap, dm), F32)],
        compiler_params=_params("parallel", "parallel", "arbitrary"),
        name="expert_swiglu",
    )(xg, w_gate, w_up, w_down, gate_slots)


def _combine_body(base_ref, yg_hbm, h_ref, slot_ref, out_ref, wbuf, xbuf, wsem, xsem, *, nt, cap, n_steps):
    n = pl.program_id(0)
    ts = h_ref.shape[0]

    def window_copies(step, half):
        b = step // nt
        i = step % nt
        copies = []
        for e in range(N_EXPERTS):
            start, _ = _slot_window(base_ref, b, e, i, nt, cap)
            copies.append(pltpu.make_async_copy(yg_hbm.at[b, e, pl.ds(start, SLOT_WIN), :],
                                                wbuf.at[half, e], wsem.at[half]))
        return copies

    @pl.when(n == 0)
    def _():
        for c in window_copies(0, 0):
            c.start()

    @pl.when(n + 1 < n_steps)
    def _():
        for c in window_copies(n + 1, (n + 1) % 2):
            c.start()

    b = n // nt
    i = n % nt
    half = n % 2
    lane = lax.broadcasted_iota(I32, (ts, SLOT_WIN), 1)
    wins = [_slot_window(base_ref, b, e, i, nt, cap) for e in range(N_EXPERTS)]
    onehot = jnp.concatenate(
        [jnp.where(slot_ref[:, e:e + 1] - wins[e][0] == lane, 1.0, 0.0).astype(BF16) for e in range(N_EXPERTS)],
        axis=1)
    for c in window_copies(n, half):
        c.wait()
    rows = wbuf[half].reshape(N_EXPERTS * SLOT_WIN, wbuf.shape[-1])
    out_ref[...] = h_ref[...] + jnp.dot(onehot, rows, preferred_element_type=F32)

    for e in range(N_EXPERTS):
        start, hi = wins[e]

        def extra(k, carry, e=e, start=start):
            first, st = _extra_window(start, k, cap)
            copy = pltpu.make_async_copy(yg_hbm.at[b, e, pl.ds(st, SLOT_WIN), :], xbuf, xsem)
            copy.start()
            scol = slot_ref[:, e:e + 1]
            scol = jnp.where(scol >= first, scol, -1)
            oh = jnp.where(scol - st == lane, 1.0, 0.0).astype(BF16)
            copy.wait()
            out_ref[...] += jnp.dot(oh, xbuf[...], preferred_element_type=F32)
            return carry

        lax.fori_loop(1, 1 + _n_extra_windows(start, hi), extra, 0)


def _combine(base, yg, h, slot_tok, bsz, seq):
    t, dm = h.shape
    cap = yg.shape[2]
    ts = MOE_TILE
    nt = seq // ts
    n_steps = bsz * nt
    grid_spec = pltpu.PrefetchScalarGridSpec(
        num_scalar_prefetch=1,
        grid=(n_steps,),
        in_specs=[
            pl.BlockSpec(memory_space=pl.ANY),
            pl.BlockSpec((ts, dm), lambda n, base: (n, 0)),
            pl.BlockSpec((ts, N_EXPERTS), lambda n, base: (n, 0)),
        ],
        out_specs=pl.BlockSpec((ts, dm), lambda n, base: (n, 0)),
        scratch_shapes=[pltpu.VMEM((2, N_EXPERTS, SLOT_WIN, dm), BF16),
                        pltpu.VMEM((SLOT_WIN, dm), BF16),
                        pltpu.SemaphoreType.DMA((2,)),
                        pltpu.SemaphoreType.DMA(())],
    )
    return pl.pallas_call(
        functools.partial(_combine_body, nt=nt, cap=cap, n_steps=n_steps),
        grid_spec=grid_spec,
        out_shape=jax.ShapeDtypeStruct((t, dm), F32),
        compiler_params=_params("arbitrary"),
        name="moe_combine",
    )(base, yg, h, slot_tok)


def _moe(h, gain, w_router_t, w_gate, w_up, w_down, layer, bsz, seq):
    cap = CAPACITY_FACTOR * seq // N_EXPERTS
    xn, logits_t = _router_logits(h, gain, w_router_t)
    gate_t, pos_t, sel_t = _select_tokens(logits_t, bsz, seq)

    slot_t = jnp.where(sel_t > 0, pos_t, -1)
    ts = MOE_TILE
    nt = seq // ts
    starts = pos_t.reshape(N_EXPERTS, bsz, nt, ts)[..., 0]
    base = jnp.concatenate([jnp.transpose(starts, (1, 0, 2)),
                            jnp.full((bsz, N_EXPERTS, 1), cap, I32)], axis=-1).reshape(-1)
    gate_tok = gate_t.T
    g_hi = gate_tok.astype(BF16)
    r1 = gate_tok - g_hi.astype(F32)
    g_mid = r1.astype(BF16)
    g_lo = (r1 - g_mid.astype(F32)).astype(BF16)
    pieces = jnp.stack([g_hi, g_mid, g_lo], axis=-1).reshape(-1, 3 * N_EXPERTS)
    pieces = jnp.pad(pieces, ((0, 0), (0, 128 - 3 * N_EXPERTS)))

    xg = _dispatch(base, xn, slot_t, bsz, seq, cap, cw=512, out_dtype=BF16, name="moe_dispatch")
    gate_slots = _dispatch(base, pieces, slot_t, bsz, seq, cap, cw=128, out_dtype=F32, name="moe_gate_dispatch")
    yg = _expert_ffn(xg, w_gate, w_up, w_down, layer, gate_slots)
    return _combine(base, yg, h, slot_t.T, bsz, seq)


def _ple_body(h_ref, g_ref, wg_ref, p_ref, wp_ref, hres_ref, o_ref, xn_ref):
    @pl.when(pl.program_id(1) == 0)
    def _():
        xn_ref[...] = _rms(h_ref[...], g_ref[...]).astype(BF16)

    gate = jax.nn.sigmoid(jnp.dot(xn_ref[...], wg_ref[...], preferred_element_type=F32))
    ple = jnp.dot(p_ref[...].astype(BF16), wp_ref[...], preferred_element_type=F32)
    o_ref[...] = hres_ref[...] + gate * ple


def _ple(h, gain, w_gate, p, w_proj, layer):
    t, dm = h.shape
    tm, tn = 1024, 512
    pd = p.shape[2]
    return pl.pallas_call(
        _ple_body,
        grid=(t // tm, dm // tn),
        in_specs=[
            pl.BlockSpec((tm, dm), lambda i, j: (i, 0)),
            pl.BlockSpec((1, dm), lambda i, j: (0, 0)),
            pl.BlockSpec((None, dm, tn), lambda i, j: (layer, 0, j)),
            pl.BlockSpec((None, tm, pd), lambda i, j: (layer, i, 0)),
            pl.BlockSpec((None, pd, tn), lambda i, j: (layer, 0, j)),
            pl.BlockSpec((tm, tn), lambda i, j: (i, j)),
        ],
        out_specs=pl.BlockSpec((tm, tn), lambda i, j: (i, j)),
        out_shape=jax.ShapeDtypeStruct((t, dm), F32),
        scratch_shapes=[pltpu.VMEM((tm, dm), BF16)],
        compiler_params=_params("parallel", "arbitrary"),
        name="ple_gate",
    )(h, gain, w_gate, p, w_proj, h)


def _rotary_lane_order(a):
    blocks = HEAD_DIM // ROT_HALF
    order = list(range(blocks))
    order[1], order[blocks // 2] = order[blocks // 2], order[1]
    shaped = a.reshape(a.shape[:-1] + (a.shape[-1] // HEAD_DIM, blocks, ROT_HALF))
    return shaped[..., jnp.array(order), :].reshape(a.shape)


def _rope_tables(positions):
    inv_freq = jnp.power(ROPE_THETA, -jnp.arange(ROT_HALF, dtype=F32) * 2.0 / (2 * ROT_HALF))
    ang = positions.astype(F32)[..., None] * inv_freq
    cos, sin = jnp.cos(ang), jnp.sin(ang)
    gap = HEAD_DIM // 2 - ROT_HALF
    ones = jnp.ones(ang.shape[:-1] + (gap,), F32)
    zeros = jnp.zeros(ang.shape[:-1] + (gap,), F32)
    cos_t = jnp.concatenate([cos, ones, cos, ones], axis=-1)
    sin_t = jnp.concatenate([-sin, zeros, sin, zeros], axis=-1)
    return cos_t.reshape(-1, HEAD_DIM), sin_t.reshape(-1, HEAD_DIM)


def kernel(x, p, positions, norm_mix, w_in, q_norm, k_norm, w_attn_br, ssm_a_re, ssm_a_im, ssm_log_dt,
           ssm_b_re, ssm_b_im, ssm_c_re, ssm_c_im, ssm_d, w_ssm_br, w_out, norm_ffn, w_router,
           w_exp_gate, w_exp_up, w_exp_down, norm_ple, w_ple_gate, w_ple_proj):
    bsz, seq, dm = x.shape
    depth = w_in.shape[0]
    t = bsz * seq
    n_attn = 3 * HEADS_PER_GROUP * len(DILATIONS) * HEAD_DIM
    ssm_width = ssm_d.shape[1]
    nc = seq // SSM_CHUNK
    n_steps = int(math.log2(nc))
    assert seq % PERM_TILE == 0 and nc == 1 << n_steps

    cos_t, sin_t = _rope_tables(positions)
    w_in_b, w_attn_b, w_ssm_b, w_out_b = (w.astype(BF16) for w in (w_in, w_attn_br, w_ssm_br, w_out))
    w_eg_b, w_eu_b, w_ed_b = (w.astype(BF16) for w in (w_exp_gate, w_exp_up, w_exp_down))
    w_pg_b, w_pp_b = w_ple_gate.astype(BF16), w_ple_proj.astype(BF16)
    w_qk_b = _rotary_lane_order(w_in_b[:, :, :2 * n_attn // 3])
    comp, tabs = _ssm_fold(ssm_a_re, ssm_a_im, ssm_log_dt, ssm_b_re, ssm_b_im, ssm_c_re, ssm_c_im, n_steps)
    w_slab = _ssm_expand(comp)
    p_rows = p.reshape(depth, t, p.shape[-1])
    h = x.reshape(t, dm)

    for l in range(depth):
        gain = norm_mix[l][None]
        qk_gain = _rotary_lane_order(jnp.stack([q_norm[l], k_norm[l]]))[:, None, :]

        u, xn = _norm_matmul(h, gain, w_in_b, l, n_attn, ssm_width, out_dtype=F32, name="u_proj")
        gates = _sigmoid_matmul(xn, w_in_b, l, n_attn + ssm_width, 2 * dm, name="gate_proj")
        outs, lses = zip(*[_attention(_qkv_proj(xn, w_qk_b, w_in_b, l, gi, qk_gain, cos_t, sin_t), gi, bsz, seq)
                           for gi in range(len(DILATIONS))])
        gated_attn = _attn_merge_proj(outs, lses, w_attn_b, l, gates)

        y = _ssm_scan(u, w_slab, tabs, l, bsz, seq)
        merged = _ssm_glu_merge(y, u, ssm_d[l][None], w_ssm_b, l, gated_attn, gates)
        h = _resid_matmul(merged, w_out_b, l, h)

        h = _moe(h, norm_ffn[l][None], w_router[l].T, w_eg_b, w_eu_b, w_ed_b, l, bsz, seq)
        h = _ple(h, norm_ple[l][None], w_pg_b, p_rows, w_pp_b, l)
    return h.reshape(bsz, seq, dm)
```

```python
import functools
import math

import jax
import jax.numpy as jnp
from jax import lax
from jax.experimental import pallas as pl
from jax.experimental.pallas import tpu as pltpu

F32 = jnp.float32
BF16 = jnp.bfloat16
I32 = jnp.int32

NORM_EPS = 1e-6
MASK_VALUE = -1e30
ROPE_THETA = 500000.0

LANES = 128
HEAD_DIM = 128
HEADS_PER_GROUP = 4
GROUP_WIDTH = HEADS_PER_GROUP * HEAD_DIM
DILATIONS = (1, 4, 16)
N_SIDE = 64
ROT_HALF = 16

SSM_GROUP = 16
SSM_STATE = 64
SSM_CHUNK = 16

N_EXPERTS = 16
CAPACITY_FACTOR = 2
SLOT_WIN = 128
SLOT_ALIGN = 16
MOE_TILE = 512
MOE_STACK = 4

PERM_TILE = 1024
VMEM_LIMIT = 56 * 1024 * 1024


def _params(*sem):
    return pltpu.CompilerParams(dimension_semantics=sem, vmem_limit_bytes=VMEM_LIMIT)


def _rms(x, gain):
    var = jnp.mean(x * x, axis=-1, keepdims=True)
    return x * lax.rsqrt(var + NORM_EPS) * gain


def _qkv_body(xn_ref, wq_ref, wk_ref, wv_ref, qkg_ref, cos_ref, sin_ref, o_ref, slab_ref, *, d):
    tm = xn_ref.shape[0]
    n = tm // d
    xn = xn_ref[...]
    cos = cos_ref[...]
    sin = sin_ref[...]
    for part, w_ref in enumerate((wq_ref, wk_ref, wv_ref)):
        acc = jnp.dot(xn, w_ref[...], preferred_element_type=F32)
        for hs in range(HEADS_PER_GROUP):
            a = acc[:, hs * HEAD_DIM:(hs + 1) * HEAD_DIM]
            if part < 2:
                a = _rms(a, qkg_ref[part])
                a = a * cos + pltpu.roll(a, HEAD_DIM // 2, 1) * sin
            cols = slice(part * GROUP_WIDTH + hs * HEAD_DIM, part * GROUP_WIDTH + (hs + 1) * HEAD_DIM)
            if d == 1:
                o_ref[:, cols] = a.astype(BF16)
            else:
                slab_ref[part, hs] = a
                for r in range(d):
                    o_ref[r * n:(r + 1) * n, cols] = slab_ref[part, hs, pl.ds(r, n, stride=d), :].astype(BF16)


def _qkv_proj(xn, w_qk, w_in, layer, gi, qk_gain, cos, sin):
    t, dm = xn.shape
    tm, tn = PERM_TILE, GROUP_WIDTH
    ng = len(DILATIONS)

    def weight(part):
        return pl.BlockSpec((None, dm, tn), lambda i: (layer, 0, part * ng + gi))

    return pl.pallas_call(
        functools.partial(_qkv_body, d=DILATIONS[gi]),
        grid=(t // tm,),
        in_specs=[
            pl.BlockSpec((tm, dm), lambda i: (i, 0)),
            weight(0), weight(1), weight(2),
            pl.BlockSpec((2, 1, HEAD_DIM), lambda i: (0, 0, 0)),
            pl.BlockSpec((tm, HEAD_DIM), lambda i: (i, 0)),
            pl.BlockSpec((tm, HEAD_DIM), lambda i: (i, 0)),
        ],
        out_specs=pl.BlockSpec((tm, 3 * tn), lambda i: (i, 0)),
        out_shape=jax.ShapeDtypeStruct((t, 3 * tn), BF16),
        scratch_shapes=[pltpu.VMEM((3, HEADS_PER_GROUP, tm, HEAD_DIM), F32)],
        compiler_params=_params("parallel"),
        name=f"qkv_proj_d{DILATIONS[gi]}",
    )(xn, w_qk, w_qk, w_in, qk_gain, cos, sin)


def _norm_mm_body(h_ref, g_ref, w_ref, o_ref, xn_out_ref, xn_ref):
    @pl.when(pl.program_id(1) == 0)
    def _():
        xn = _rms(h_ref[...], g_ref[...]).astype(BF16)
        xn_ref[...] = xn
        xn_out_ref[...] = xn

    o_ref[...] = jnp.dot(xn_ref[...], w_ref[...], preferred_element_type=F32).astype(o_ref.dtype)


def _norm_matmul(h, gain, w, layer, col0, ncols, *, out_dtype, tn=512, name):
    t, dm = h.shape
    tm = 1024
    nj = ncols // tn
    blk0 = col0 // tn
    return pl.pallas_call(
        _norm_mm_body,
        grid=(t // tm, nj),
        in_specs=[
            pl.BlockSpec((tm, dm), lambda i, j: (i, 0)),
            pl.BlockSpec((1, dm), lambda i, j: (0, 0)),
            pl.BlockSpec((None, dm, tn), lambda i, j: (layer, 0, blk0 + j)),
        ],
        out_specs=[pl.BlockSpec((tm, tn), lambda i, j: (i, j)),
                   pl.BlockSpec((tm, dm), lambda i, j: (i, 0))],
        out_shape=[jax.ShapeDtypeStruct((t, nj * tn), out_dtype), jax.ShapeDtypeStruct((t, dm), BF16)],
        scratch_shapes=[pltpu.VMEM((tm, dm), BF16)],
        compiler_params=_params("parallel", "arbitrary"),
        name=name,
    )(h, gain, w)


def _act_mm_body(x_ref, w_ref, o_ref):
    o_ref[...] = jax.nn.sigmoid(jnp.dot(x_ref[...], w_ref[...], preferred_element_type=F32)).astype(o_ref.dtype)


def _sigmoid_matmul(x, w, layer, col0, ncols, *, tn=512, name):
    t, dm = x.shape
    tm = 1024
    nj = ncols // tn
    blk0 = col0 // tn
    return pl.pallas_call(
        _act_mm_body,
        grid=(t // tm, nj),
        in_specs=[
            pl.BlockSpec((tm, dm), lambda i, j: (i, 0)),
            pl.BlockSpec((None, dm, tn), lambda i, j: (layer, 0, blk0 + j)),
        ],
        out_specs=pl.BlockSpec((tm, tn), lambda i, j: (i, j)),
        out_shape=jax.ShapeDtypeStruct((t, nj * tn), BF16),
        compiler_params=_params("parallel", "arbitrary"),
        name=name,
    )(x, w)


def _attn_body(q_ref, kp_ref, km_ref, kn_ref, vp_ref, vm_ref, vn_ref, o_ref, l_ref,
               kw_ref, vw_ref, os_ref, ls_ref, *, tq, sub_len):
    c = pl.program_id(2)
    kw_ref[0:N_SIDE] = kp_ref[...]
    kw_ref[N_SIDE:N_SIDE + tq] = km_ref[...].reshape(tq, GROUP_WIDTH)
    kw_ref[N_SIDE + tq:] = kn_ref[...]
    vw_ref[0:N_SIDE] = vp_ref[...]
    vw_ref[N_SIDE:N_SIDE + tq] = vm_ref[...].reshape(tq, GROUP_WIDTH)
    vw_ref[N_SIDE + tq:] = vn_ref[...]
    q = q_ref[...].reshape(tq, GROUP_WIDTH)

    sb = min(128, tq)
    nk = sb + 2 * N_SIDE
    scale = HEAD_DIM ** -0.5
    row = lax.broadcasted_iota(I32, (sb, nk), 0)
    col = lax.broadcasted_iota(I32, (sb, nk), 1)
    band = jnp.abs(col - row - N_SIDE) <= N_SIDE
    for i in range(tq // sb):
        kpos = c * tq + (i * sb - N_SIDE) + col
        valid = band & (kpos >= 0) & (kpos < sub_len)
        for hs in range(HEADS_PER_GROUP):
            lanes = slice(hs * HEAD_DIM, (hs + 1) * HEAD_DIM)
            qh = q[i * sb:(i + 1) * sb, lanes]
            kh = kw_ref[i * sb:i * sb + nk, lanes]
            vh = vw_ref[i * sb:i * sb + nk, lanes]
            s = lax.dot_general(qh, kh, (((1,), (1,)), ((), ())), preferred_element_type=F32) * scale
            s = jnp.where(valid, s, MASK_VALUE)
            m = jnp.max(s, axis=-1, keepdims=True)
            e = jnp.exp(s - m)
            den = jnp.sum(e, axis=-1, keepdims=True)
            o = jnp.dot((e / den).astype(BF16), vh, preferred_element_type=F32)
            os_ref[i * sb:(i + 1) * sb, lanes] = o
            ls_ref[i * sb:(i + 1) * sb, lanes] = jnp.broadcast_to(m + jnp.log(den), (sb, HEAD_DIM))
    o_ref[...] = os_ref[...].reshape(o_ref.shape)
    l_ref[...] = ls_ref[...].reshape(l_ref.shape)


def _attention(qkv, gi, bsz, seq):
    d = DILATIONS[gi]
    t = bsz * seq
    sub_len = seq // d
    nbt = PERM_TILE // (N_SIDE * d)
    ntile = seq // PERM_TILE
    tq = min(512, sub_len)
    nbq = tq // N_SIDE
    nblk = sub_len // N_SIDE
    view = (bsz, ntile, d, nbt, N_SIDE, qkv.shape[1])
    oview = (bsz, ntile, d, nbt, N_SIDE, GROUP_WIDTH)
    gw = GROUP_WIDTH

    if nbt >= nbq:
        per = nbt // nbq
        main_shape = (None, None, None, nbq, N_SIDE, gw)

        def main_idx(col):
            return lambda b, r, c: (b, c // per, r, c % per, 0, col)
    else:
        main_shape = (None, nbq // nbt, None, nbt, N_SIDE, gw)

        def main_idx(col):
            return lambda b, r, c: (b, c, r, 0, 0, col)

    halo_shape = (None, None, None, None, N_SIDE, gw)

    def prev_idx(col):
        def f(b, r, c):
            n = jnp.maximum(c * nbq - 1, 0)
            return (b, n // nbt, r, n % nbt, 0, col)
        return f

    def next_idx(col):
        def f(b, r, c):
            n = jnp.minimum((c + 1) * nbq, nblk - 1)
            return (b, n // nbt, r, n % nbt, 0, col)
        return f

    qc, kc, vc = 0, 1, 2
    x = qkv.reshape(view)
    o, l = pl.pallas_call(
        functools.partial(_attn_body, tq=tq, sub_len=sub_len),
        grid=(bsz, d, sub_len // tq),
        in_specs=[
            pl.BlockSpec(main_shape, main_idx(qc)),
            pl.BlockSpec(halo_shape, prev_idx(kc)),
            pl.BlockSpec(main_shape, main_idx(kc)),
            pl.BlockSpec(halo_shape, next_idx(kc)),
            pl.BlockSpec(halo_shape, prev_idx(vc)),
            pl.BlockSpec(main_shape, main_idx(vc)),
            pl.BlockSpec(halo_shape, next_idx(vc)),
        ],
        out_specs=[pl.BlockSpec(main_shape, main_idx(0)), pl.BlockSpec(main_shape, main_idx(0))],
        out_shape=[jax.ShapeDtypeStruct(oview, F32), jax.ShapeDtypeStruct(oview, F32)],
        scratch_shapes=[pltpu.VMEM((tq + 2 * N_SIDE, gw), BF16), pltpu.VMEM((tq + 2 * N_SIDE, gw), BF16),
                        pltpu.VMEM((tq, gw), F32), pltpu.VMEM((tq, gw), F32)],
        compiler_params=_params("parallel", "parallel", "arbitrary"),
        name=f"dilated_attn_d{d}",
    )(x, x, x, x, x, x, x)
    return o.reshape(t, gw), l.reshape(t, gw)


def _attn_merge_body(o0, l0, o1, l1, o2, l2, w_ref, g_ref, out_ref, comb_ref, so_ref, sl_ref):
    tm = out_ref.shape[0]

    @pl.when(pl.program_id(1) == 0)
    def _():
        for gi, (o_ref, l_ref) in enumerate(((o0, l0), (o1, l1), (o2, l2))):
            d = DILATIONS[gi]
            n = tm // d
            for hs in range(HEADS_PER_GROUP):
                lanes = slice(hs * HEAD_DIM, (hs + 1) * HEAD_DIM)
                for r in range(d):
                    ov = o_ref[r * n:(r + 1) * n, lanes]
                    lv = l_ref[r * n:(r + 1) * n, lanes]
                    if d == 1:
                        so_ref[gi, hs] = ov
                        sl_ref[gi, hs] = lv
                    else:
                        so_ref[gi, hs, pl.ds(r, n, stride=d), :] = ov
                        sl_ref[gi, hs, pl.ds(r, n, stride=d), :] = lv
        for hs in range(HEADS_PER_GROUP):
            ls = [sl_ref[gi, hs] for gi in range(3)]
            mx = jnp.maximum(jnp.maximum(ls[0], ls[1]), ls[2])
            ws = [jnp.exp(l - mx) for l in ls]
            num = ws[0] * so_ref[0, hs] + ws[1] * so_ref[1, hs] + ws[2] * so_ref[2, hs]
            comb = num / (ws[0] + ws[1] + ws[2])
            comb_ref[:, hs * HEAD_DIM:(hs + 1) * HEAD_DIM] = comb.astype(BF16)

    acc = jnp.dot(comb_ref[...], w_ref[...], preferred_element_type=F32)
    out_ref[...] = (acc * g_ref[...].astype(F32)).astype(out_ref.dtype)


def _attn_merge_proj(outs, lses, w_attn, layer, gates):
    t = outs[0].shape[0]
    dm = w_attn.shape[2]
    tm, tn = PERM_TILE, 512
    row = pl.BlockSpec((tm, GROUP_WIDTH), lambda i, j: (i, 0))
    return pl.pallas_call(
        _attn_merge_body,
        grid=(t // tm, dm // tn),
        in_specs=[row, row, row, row, row, row,
                  pl.BlockSpec((None, GROUP_WIDTH, tn), lambda i, j: (layer, 0, j)),
                  pl.BlockSpec((tm, tn), lambda i, j: (i, j))],
        out_specs=pl.BlockSpec((tm, tn), lambda i, j: (i, j)),
        out_shape=jax.ShapeDtypeStruct((t, dm), BF16),
        scratch_shapes=[pltpu.VMEM((tm, GROUP_WIDTH), BF16),
                        pltpu.VMEM((3, HEADS_PER_GROUP, tm, HEAD_DIM), F32),
                        pltpu.VMEM((3, HEADS_PER_GROUP, tm, HEAD_DIM), F32)],
        compiler_params=_params("parallel", "arbitrary"),
        name="attn_merge_proj",
    )(outs[0], lses[0], outs[1], lses[1], outs[2], lses[2], w_attn, gates)


def _ssm_fold_body(sp_re, sp_im, si_re, si_im, sot_re, sot_im, bb_re, bb_im, c_re, c_im, ct_re, ct_im,
                   rep_ref, tile_ref, o_ref):
    lc, hg, p = SSM_CHUNK, SSM_GROUP, SSM_STATE
    gps = LANES // hg
    kw = lc * hg
    hi_prec = lax.Precision.HIGHEST
    nt = (((1,), (1,)), ((), ()))
    for g in range(gps):
        strip = jnp.zeros((hg, 2 * kw), F32)
        injections = []
        for d in range(2):
            br, bi = bb_re[d, g], bb_im[d, g]
            cr, ci = c_re[d, g], c_im[d, g]
            pr, pi = sp_re[d, g][:, None, :], sp_im[d, g][:, None, :]
            y_re = (pr * cr[None] - pi * ci[None]).reshape(2 * kw, p)
            y_im = (pr * ci[None] + pi * cr[None]).reshape(2 * kw, p)
            strip = strip + lax.dot_general(jnp.concatenate([br, -bi], axis=1), jnp.concatenate([y_re, y_im], axis=1),
                                            nt, precision=hi_prec, preferred_element_type=F32)
            qr, qi = si_re[d, g][:, None, :], si_im[d, g][:, None, :]
            injections += [(qr * br[None] - qi * bi[None]).reshape(kw, p),
                           (qr * bi[None] + qi * br[None]).reshape(kw, p)]
            ar = jnp.dot(sot_re[d, g], rep_ref[...], precision=hi_prec, preferred_element_type=F32)
            ai = jnp.dot(sot_im[d, g], rep_ref[...], precision=hi_prec, preferred_element_type=F32)
            xr = jnp.dot(ct_re[d, g], tile_ref[...], precision=hi_prec, preferred_element_type=F32)
            xi = jnp.dot(ct_im[d, g], tile_ref[...], precision=hi_prec, preferred_element_type=F32)
            row0 = ((d * (gps // 2) + g // 2) * 2) * LANES + (g % 2) * p
            o_ref[2, row0:row0 + p, :] = (ar * xr - ai * xi).astype(o_ref.dtype)
            o_ref[2, row0 + LANES:row0 + LANES + p, :] = (-(ar * xi + ai * xr)).astype(o_ref.dtype)
        state_in = jnp.concatenate(injections, axis=1)
        for s in range(lc):
            rows = slice(s * LANES + g * hg, s * LANES + (g + 1) * hg)
            o_ref[0, rows, :] = strip[:, (lc - 1 - s) * hg:(lc - 1 - s) * hg + kw].astype(o_ref.dtype)
            o_ref[1, rows, :] = state_in[s * hg:(s + 1) * hg, :].astype(o_ref.dtype)


def _ssm_fold(a_re, a_im, log_dt, b_re, b_im, c_re, c_im, n_steps):
    f32 = lambda v: v.astype(F32)
    a_re, a_im, b_re, b_im, c_re, c_im = map(f32, (a_re, a_im, b_re, b_im, c_re, c_im))
    depth, _, g, p = a_re.shape
    lc, hg = SSM_CHUNK, SSM_GROUP
    gps = LANES // hg
    ns, npair = g // gps, gps // 2
    kw = lc * hg
    side = lc * LANES
    dt = jnp.exp(f32(log_dt))[..., None]

    def lam_pow(n):
        n = n.astype(F32)[:, None]
        mag = jnp.exp((a_re * dt)[..., None, :] * n)
        ang = (a_im * dt)[..., None, :] * n
        return mag * jnp.cos(ang), mag * jnp.sin(ang)

    pw_re, pw_im = lam_pow(jnp.arange(lc + 1))
    lb_re, lb_im = pw_re[..., 1, :], pw_im[..., 1, :]
    den = a_re * a_re + a_im * a_im
    nr, ni = lb_re - 1.0, lb_im
    coef_re = ((nr * a_re + ni * a_im) / den)[..., None, :]
    coef_im = ((ni * a_re - nr * a_im) / den)[..., None, :]
    bt_re, bt_im = jnp.swapaxes(b_re, -1, -2), jnp.swapaxes(b_im, -1, -2)
    bb_re = coef_re * bt_re - coef_im * bt_im
    bb_im = coef_re * bt_im + coef_im * bt_re

    def per_dir(pw, fwd, bwd):
        return jnp.stack([pw[:, 0][..., fwd, :], pw[:, 1][..., bwd, :]], axis=1)

    steps = jnp.arange(lc)
    slot = jnp.arange(2 * lc)
    f_on = ((slot >= lc - 1) & (slot < 2 * lc - 1)).astype(F32)[:, None]
    b_on = (slot <= lc - 1).astype(F32)[:, None]
    f_idx = jnp.clip(slot - (lc - 1), 0, lc)
    b_idx = jnp.clip(lc - 1 - slot, 0, lc)
    strip_pw = lambda pw: jnp.stack([pw[:, 0][..., f_idx, :] * f_on, pw[:, 1][..., b_idx, :] * b_on], axis=1)
    sp_re, sp_im = strip_pw(pw_re), strip_pw(pw_im)
    si_re, si_im = per_dir(pw_re, lc - 1 - steps, steps), per_dir(pw_im, lc - 1 - steps, steps)
    so_re, so_im = per_dir(pw_re, steps + 1, lc - steps), per_dir(pw_im, steps + 1, lc - steps)
    sot_re, sot_im = jnp.swapaxes(so_re, -1, -2), jnp.swapaxes(so_im, -1, -2)
    ct_re, ct_im = jnp.swapaxes(c_re, -1, -2), jnp.swapaxes(c_im, -1, -2)
    col = jnp.arange(kw)[None, :]
    rep = (jnp.arange(lc)[:, None] == col // hg).astype(F32)
    tile = (jnp.arange(hg)[:, None] == col % hg).astype(F32)

    def rows_spec(rows):
        return pl.BlockSpec((None, 2, gps, rows, p), lambda l, s: (l, 0, s, 0, 0))

    def cols_spec(cols):
        return pl.BlockSpec((None, 2, gps, p, cols), lambda l, s: (l, 0, s, 0, 0))

    const = pl.BlockSpec((lc, kw), lambda l, s: (0, 0))
    comp = pl.pallas_call(
        _ssm_fold_body,
        grid=(depth, ns),
        in_specs=[rows_spec(2 * lc), rows_spec(2 * lc), rows_spec(lc), rows_spec(lc), cols_spec(lc), cols_spec(lc),
                  rows_spec(hg), rows_spec(hg), rows_spec(hg), rows_spec(hg), cols_spec(hg), cols_spec(hg),
                  const, const],
        out_specs=pl.BlockSpec((None, None, 3, side, kw), lambda l, s: (l, s, 0, 0, 0)),
        out_shape=jax.ShapeDtypeStruct((depth, ns, 3, side, kw), BF16),
        compiler_params=_params("parallel", "parallel"),
        name="s5_fold_params",
    )(sp_re, sp_im, si_re, si_im, sot_re, sot_im, bb_re, bb_im, c_re, c_im, ct_re, ct_im, rep, tile)

    sc_re, sc_im = lam_pow(lc * (2 ** jnp.arange(n_steps)))
    tab = jnp.stack([sc_re, sc_im], axis=-2).reshape(depth, 2, ns, npair, 2, n_steps, 2, p)
    tab = jnp.transpose(tab, (0, 2, 1, 3, 5, 6, 4, 7)).reshape(depth, ns, 2 * npair * n_steps * 2, 2 * p)
    return comp, tab


def _ssm_expand_constants():
    lc, hg, p = SSM_CHUNK, SSM_GROUP, SSM_STATE
    r = jnp.arange(lc * hg)[:, None]
    c = jnp.arange(lc * LANES)[None, :]
    e_time = (r // hg == c // LANES) & (r % hg == c % hg)
    e_state = (r // (2 * p) == c // (lc * LANES // 2)) & ((r // p) % 2 == (c // LANES) % 2) & (r % p == c % p)
    rr = jnp.arange(lc * LANES)[:, None]
    lane_group = lambda i: (i % LANES) // hg
    state_group = lambda i: 2 * ((i // (2 * LANES)) % (LANES // hg // 2)) + (i % LANES) // p
    masks = [lane_group(rr) == lane_group(c), lane_group(rr) == state_group(c), state_group(rr) == lane_group(c)]
    return (jnp.stack([e_time, e_state, e_time]).astype(BF16), jnp.stack(masks).astype(BF16))


def _ssm_expand_body(c_ref, e_ref, m_ref, o_ref):
    rows = 512
    for r in range(0, o_ref.shape[0], rows):
        acc = jnp.dot(c_ref[r:r + rows], e_ref[...], preferred_element_type=F32)
        o_ref[r:r + rows] = jnp.where(m_ref[r:r + rows] > 0, acc, 0.0).astype(o_ref.dtype)


def _ssm_expand(comp):
    depth, ns, three, side, kw = comp.shape
    e, mask = _ssm_expand_constants()
    return pl.pallas_call(
        _ssm_expand_body,
        grid=(three, depth * ns),
        in_specs=[
            pl.BlockSpec((None, None, None, side, kw), lambda m, n: (n // ns, n % ns, m, 0, 0)),
            pl.BlockSpec((None, kw, side), lambda m, n: (m, 0, 0)),
            pl.BlockSpec((None, side, side), lambda m, n: (m, 0, 0)),
        ],
        out_specs=pl.BlockSpec((None, None, None, side, side), lambda m, n: (n // ns, n % ns, m, 0, 0)),
        out_shape=jax.ShapeDtypeStruct((depth, ns, three, side, side), BF16),
        compiler_params=_params("arbitrary", "arbitrary"),
        name="s5_expand_weights",
    )(comp, e, mask)


def _ssm_body(u_ref, toe_ref, sin_ref, sout_ref, tab_ref, y_ref, lhs_ref, inj_ref, st_ref, intra_ref, *, n_steps):
    seq = u_ref.shape[0]
    lc = SSM_CHUNK
    nc = seq // lc
    n_slabs = inj_ref.shape[0]
    per_dir = n_slabs // 4
    row = lax.broadcasted_iota(I32, (nc, LANES), 0)

    for s in range(lc):
        lhs_ref[:, s * LANES:(s + 1) * LANES] = u_ref[pl.ds(s, nc, stride=lc), :].astype(BF16)
    lhs = lhs_ref[...]

    for j in range(0, n_slabs, 2):
        r = jnp.dot(lhs, sin_ref[:, j * LANES:(j + 2) * LANES], preferred_element_type=F32)
        inj_ref[j] = r[:, :LANES]
        inj_ref[j + 1] = r[:, LANES:]

    def shifted(x, sh, direction):
        if sh % 8 == 0:
            zeros = jnp.zeros((sh, LANES), x.dtype)
            if direction == 0:
                return jnp.concatenate([zeros, x[:nc - sh]], axis=0)
            return jnp.concatenate([x[sh:], zeros], axis=0)
        if direction == 0:
            return jnp.where(row >= sh, pltpu.roll(x, sh, 0), 0.0)
        return jnp.where(row < nc - sh, pltpu.roll(x, nc - sh, 0), 0.0)

    for jq in range(2 * per_dir):
        direction = jq // per_dir
        xr = inj_ref[2 * jq]
        xi = inj_ref[2 * jq + 1]
        for k in range(n_steps):
            base = (jq * n_steps + k) * 2
            ar = tab_ref[base:base + 1, :]
            ai = tab_ref[base + 1:base + 2, :]
            sr = shifted(xr, 1 << k, direction)
            si = shifted(xi, 1 << k, direction)
            xr, xi = xr + ar * sr - ai * si, xi + ar * si + ai * sr
        st_ref[2 * jq] = shifted(xr, 1, direction).astype(BF16)
        st_ref[2 * jq + 1] = shifted(xi, 1, direction).astype(BF16)
        cols = slice(2 * jq * LANES, (2 * jq + 2) * LANES)
        intra_ref[jq] = jnp.dot(lhs, toe_ref[:, cols], preferred_element_type=F32)

    states = jnp.concatenate([st_ref[j] for j in range(n_slabs)], axis=1)
    for t in range(0, lc, 2):
        cols = slice(t * LANES, (t + 2) * LANES)
        r = intra_ref[t // 2] + jnp.dot(states, sout_ref[:, cols], preferred_element_type=F32)
        y_ref[pl.ds(t, nc, stride=lc), :] = r[:, :LANES]
        y_ref[pl.ds(t + 1, nc, stride=lc), :] = r[:, LANES:]


def _ssm_scan(u, w_slab, tab, layer, bsz, seq):
    t, width = u.shape
    ns = width // LANES
    nc = seq // SSM_CHUNK
    n_steps = int(math.log2(nc))
    side = SSM_CHUNK * LANES
    n_slabs = side // LANES
    once = pl.Buffered(1)

    def weight(m):
        return pl.BlockSpec((None, None, None, side, side), lambda g, b: (layer, g, m, 0, 0), pipeline_mode=once)

    return pl.pallas_call(
        functools.partial(_ssm_body, n_steps=n_steps),
        grid=(ns, bsz),
        in_specs=[
            pl.BlockSpec((seq, LANES), lambda g, b: (b, g)),
            weight(0), weight(1), weight(2),
            pl.BlockSpec((None, None, tab.shape[-2], LANES), lambda g, b: (layer, g, 0, 0)),
        ],
        out_specs=pl.BlockSpec((seq, LANES), lambda g, b: (b, g)),
        out_shape=jax.ShapeDtypeStruct((t, width), F32),
        scratch_shapes=[pltpu.VMEM((nc, side), BF16),
                        pltpu.VMEM((n_slabs, nc, LANES), F32),
                        pltpu.VMEM((n_slabs, nc, LANES), BF16),
                        pltpu.VMEM((n_slabs // 2, nc, 2 * LANES), F32)],
        compiler_params=_params("arbitrary", "arbitrary"),
        name="s5_chunk_scan",
    )(u, w_slab, w_slab, w_slab, tab)


def _ssm_glu_body(y_ref, u_ref, d_ref, wa_ref, wb_ref, ga_ref, gs_ref, o_ref, act_ref):
    @pl.when(pl.program_id(1) == 0)
    def _():
        act_ref[...] = jax.nn.gelu(y_ref[...] + d_ref[...] * u_ref[...]).astype(BF16)

    act = act_ref[...]
    za = jnp.dot(act, wa_ref[...], preferred_element_type=F32)
    zb = jnp.dot(act, wb_ref[...], preferred_element_type=F32)
    s_branch = za * jax.nn.sigmoid(zb)
    o_ref[...] = (ga_ref[...].astype(F32) + gs_ref[...].astype(F32) * s_branch).astype(o_ref.dtype)


def _ssm_glu_merge(y, u, d_skip, w_ssm, layer, gated_attn, gates):
    t, width = y.shape
    dm = gated_attn.shape[1]
    tm, tn = 1024, 512
    nj = dm // tn
    return pl.pallas_call(
        _ssm_glu_body,
        grid=(t // tm, nj),
        in_specs=[
            pl.BlockSpec((tm, width), lambda i, j: (i, 0)),
            pl.BlockSpec((tm, width), lambda i, j: (i, 0)),
            pl.BlockSpec((1, width), lambda i, j: (0, 0)),
            pl.BlockSpec((None, width, tn), lambda i, j: (layer, 0, j)),
            pl.BlockSpec((None, width, tn), lambda i, j: (layer, 0, j + nj)),
            pl.BlockSpec((tm, tn), lambda i, j: (i, j)),
            pl.BlockSpec((tm, tn), lambda i, j: (i, j + nj)),
        ],
        out_specs=pl.BlockSpec((tm, tn), lambda i, j: (i, j)),
        out_shape=jax.ShapeDtypeStruct((t, dm), BF16),
        scratch_shapes=[pltpu.VMEM((tm, width), BF16)],
        compiler_params=_params("parallel", "arbitrary"),
        name="s5_glu_merge",
    )(y, u, d_skip, w_ssm, w_ssm, gated_attn, gates)


def _resid_mm_body(x_ref, w_ref, h_ref, o_ref):
    o_ref[...] = h_ref[...] + jnp.dot(x_ref[...], w_ref[...], preferred_element_type=F32)


def _resid_matmul(x, w, layer, h):
    t, k = x.shape
    dm = w.shape[2]
    tm, tn = 1024, 512
    return pl.pallas_call(
        _resid_mm_body,
        grid=(t // tm, dm // tn),
        in_specs=[
            pl.BlockSpec((tm, k), lambda i, j: (i, 0)),
            pl.BlockSpec((None, k, tn), lambda i, j: (layer, 0, j)),
            pl.BlockSpec((tm, tn), lambda i, j: (i, j)),
        ],
        out_specs=pl.BlockSpec((tm, tn), lambda i, j: (i, j)),
        out_shape=jax.ShapeDtypeStruct((t, dm), F32),
        compiler_params=_params("parallel", "arbitrary"),
        name="out_proj",
    )(x, w, h)


def _router_body(h_ref, g_ref, wr_ref, xn_ref, lg_ref):
    xn = _rms(h_ref[...], g_ref[...])
    xn_ref[...] = xn.astype(BF16)
    lg_ref[...] = lax.dot_general(wr_ref[...], xn, (((1,), (1,)), ((), ())),
                                  precision=lax.Precision.HIGHEST, preferred_element_type=F32)


def _router_logits(h, gain, w_router_t):
    t, dm = h.shape
    tm = 512
    return pl.pallas_call(
        _router_body,
        grid=(t // tm,),
        in_specs=[
            pl.BlockSpec((tm, dm), lambda i: (i, 0)),
            pl.BlockSpec((1, dm), lambda i: (0, 0)),
            pl.BlockSpec((N_EXPERTS, dm), lambda i: (0, 0)),
        ],
        out_specs=[pl.BlockSpec((tm, dm), lambda i: (i, 0)),
                   pl.BlockSpec((N_EXPERTS, tm), lambda i: (0, i))],
        out_shape=[jax.ShapeDtypeStruct((t, dm), BF16), jax.ShapeDtypeStruct((N_EXPERTS, t), F32)],
        compiler_params=_params("parallel"),
        name="router_logits",
    )(h, gain, w_router_t)


def _select_body(lg_ref, gate_ref, pos_ref, sel_ref, *, cap):
    lg = lg_ref[...]
    ne, s = lg.shape
    m = jnp.max(lg, axis=0, keepdims=True)
    e = jnp.exp(lg - m)
    aff = e / jnp.sum(e, axis=0, keepdims=True)
    bits = lax.bitcast_convert_type(aff, I32)

    def count(ind):
        return jnp.sum(ind, axis=1, keepdims=True)

    def value_bit(k, thr):
        cand = thr | jnp.left_shift(jnp.int32(1), 30 - k)
        return jnp.where(count(jnp.where(bits >= cand, 1.0, 0.0)) >= cap, cand, thr)

    thr = lax.fori_loop(0, 31, value_bit, jnp.zeros((ne, 1), I32))
    above = jnp.where(bits > thr, 1.0, 0.0)
    tie = jnp.where(bits == thr, 1.0, 0.0)
    need = cap - count(above)
    idx = lax.broadcasted_iota(I32, (ne, s), 1)
    n_bits = int(math.log2(s)) + 1

    def index_bit(k, bound):
        cand = bound | jnp.left_shift(jnp.int32(1), n_bits - 1 - k)
        below = count(jnp.where(idx < cand, tie, 0.0))
        return jnp.where(below < need, cand, bound)

    bound = lax.fori_loop(0, n_bits, index_bit, jnp.zeros((ne, 1), I32))
    sel = above + jnp.where(idx <= bound, tie, 0.0)
    gate_ref[...] = sel * aff
    sel_ref[...] = sel.astype(I32)

    blk = 256
    tri = jnp.where(lax.broadcasted_iota(I32, (blk, blk), 0) <= lax.broadcasted_iota(I32, (blk, blk), 1),
                    1.0, 0.0).astype(BF16)
    carry = jnp.zeros((ne, 1), F32)
    for j in range(s // blk):
        seg = sel[:, j * blk:(j + 1) * blk]
        inc = jnp.dot(seg.astype(BF16), tri, preferred_element_type=F32)
        pos_ref[:, j * blk:(j + 1) * blk] = (inc - seg + carry).astype(I32)
        carry = carry + inc[:, blk - 1:blk]


def _select_tokens(logits_t, bsz, seq):
    cap = CAPACITY_FACTOR * seq // N_EXPERTS
    t = bsz * seq
    spec = pl.BlockSpec((N_EXPERTS, seq), lambda b: (0, b))
    return pl.pallas_call(
        functools.partial(_select_body, cap=cap),
        grid=(bsz,),
        in_specs=[spec],
        out_specs=[spec, spec, spec],
        out_shape=[jax.ShapeDtypeStruct((N_EXPERTS, t), F32), jax.ShapeDtypeStruct((N_EXPERTS, t), I32),
                   jax.ShapeDtypeStruct((N_EXPERTS, t), I32)],
        compiler_params=_params("parallel"),
        name="expert_choice_select",
    )(logits_t)


def _slot_window(base_ref, b, e, i, nt, cap):
    k = (b * N_EXPERTS + e) * (nt + 1) + i
    lo = base_ref[k]
    start = jnp.minimum((lo // SLOT_ALIGN) * SLOT_ALIGN, cap - SLOT_WIN)
    return pl.multiple_of(start, SLOT_ALIGN), base_ref[k + 1]


def _n_extra_windows(start, hi):
    return (jnp.maximum(hi - (start + SLOT_WIN), 0) + SLOT_WIN - 1) // SLOT_WIN


def _extra_window(start, k, cap):
    first = start + SLOT_WIN * k
    return first, pl.multiple_of(jnp.minimum(first, cap - SLOT_WIN), SLOT_ALIGN)


def _dispatch_body(base_ref, x_ref, slot_ref, out_ref, *, nt, cap):
    b = pl.program_id(0)
    i = pl.program_id(2)
    ts = x_ref.shape[0]

    @pl.when(i == 0)
    def _():
        out_ref[...] = jnp.zeros_like(out_ref)

    x = x_ref[...]
    wio = lax.broadcasted_iota(I32, (SLOT_WIN, ts), 0)
    wins = [_slot_window(base_ref, b, e, i, nt, cap) for e in range(N_EXPERTS)]
    onehot = jnp.concatenate(
        [jnp.where(slot_ref[e:e + 1, :] - wins[e][0] == wio, 1.0, 0.0).astype(x.dtype) for e in range(N_EXPERTS)],
        axis=0)
    res = jnp.dot(onehot, x, preferred_element_type=F32)
    for e in range(N_EXPERTS):
        out_ref[e, pl.ds(wins[e][0], SLOT_WIN), :] += res[e * SLOT_WIN:(e + 1) * SLOT_WIN].astype(out_ref.dtype)

    for e in range(N_EXPERTS):
        start, hi = wins[e]

        def extra(k, carry, e=e, start=start):
            first, st = _extra_window(start, k, cap)
            srow = slot_ref[e:e + 1, :]
            srow = jnp.where(srow >= first, srow, -1)
            oh = jnp.where(srow - st == wio, 1.0, 0.0).astype(x.dtype)
            out_ref[e, pl.ds(st, SLOT_WIN), :] += jnp.dot(oh, x, preferred_element_type=F32).astype(out_ref.dtype)
            return carry

        lax.fori_loop(1, 1 + _n_extra_windows(start, hi), extra, 0)


def _dispatch(base, x, slot_t, bsz, seq, cap, *, cw, out_dtype, name):
    t, width = x.shape
    ts = MOE_TILE
    nt = seq // ts
    grid_spec = pltpu.PrefetchScalarGridSpec(
        num_scalar_prefetch=1,
        grid=(bsz, width // cw, nt),
        in_specs=[
            pl.BlockSpec((ts, cw), lambda b, c, i, base: (b * nt + i, c)),
            pl.BlockSpec((N_EXPERTS, ts), lambda b, c, i, base: (0, b * nt + i)),
        ],
        out_specs=pl.BlockSpec((None, N_EXPERTS, cap, cw), lambda b, c, i, base: (b, 0, 0, c)),
    )
    return pl.pallas_call(
        functools.partial(_dispatch_body, nt=nt, cap=cap),
        grid_spec=grid_spec,
        out_shape=jax.ShapeDtypeStruct((bsz, N_EXPERTS, cap, width), out_dtype),
        compiler_params=_params("parallel", "parallel", "arbitrary"),
        name=name,
    )(base, x, slot_t)


def _expert_body(x_ref, wg_ref, wu_ref, wd_ref, gs_ref, y_ref, acc_ref):
    e = pl.program_id(0)
    f = pl.program_id(2)
    x = x_ref[...]
    hid = (jax.nn.silu(jnp.dot(x, wg_ref[...], preferred_element_type=F32))
           * jnp.dot(x, wu_ref[...], preferred_element_type=F32)).astype(BF16)
    part = jnp.dot(hid, wd_ref[...], preferred_element_type=F32)

    @pl.when(f == 0)
    def _():
        acc_ref[...] = part

    @pl.when(f > 0)
    def _():
        acc_ref[...] += part

    @pl.when(f == pl.num_programs(2) - 1)
    def _():
        pieces = gs_ref[...]
        lane = lax.broadcasted_iota(I32, pieces.shape, 1)
        mine = (lane >= 3 * e) & (lane < 3 * e + 3)
        gate = jnp.sum(jnp.where(mine, pieces, 0.0), axis=1, keepdims=True)
        y_ref[...] = (acc_ref[...] * gate).astype(y_ref.dtype)


def _expert_ffn(xg, w_gate, w_up, w_down, layer, gate_slots):
    bsz, ne, cap, dm = xg.shape
    ff = w_gate.shape[3]
    tf = 512
    return pl.pallas_call(
        _expert_body,
        grid=(ne, bsz, ff // tf),
        in_specs=[
            pl.BlockSpec((None, None, cap, dm), lambda e, b, f: (b, e, 0, 0)),
            pl.BlockSpec((None, None, dm, tf), lambda e, b, f: (layer, e, 0, f)),
            pl.BlockSpec((None, None, dm, tf), lambda e, b, f: (layer, e, 0, f)),
            pl.BlockSpec((None, None, tf, dm), lambda e, b, f: (layer, e, f, 0)),
            pl.BlockSpec((None, None, cap, 128), lambda e, b, f: (b, e, 0, 0)),
        ],
        out_specs=pl.BlockSpec((None, None, cap, dm), lambda e, b, f: (b, e, 0, 0)),
        out_shape=jax.ShapeDtypeStruct((bsz, ne, cap, dm), BF16),
        scratch_shapes=[pltpu.VMEM((cap, dm), F32)],
        compiler_params=_params("parallel", "parallel", "arbitrary"),
        name="expert_swiglu",
    )(xg, w_gate, w_up, w_down, gate_slots)


def _combine_body(base_ref, yg_hbm, h_ref, slot_ref, out_ref, wbuf, xbuf, wsem, xsem, *, nt, cap, n_steps):
    n = pl.program_id(0)
    ts = h_ref.shape[0]

    def window_copies(step, half):
        b = step // nt
        i = step % nt
        copies = []
        for e in range(N_EXPERTS):
            start, _ = _slot_window(base_ref, b, e, i, nt, cap)
            copies.append(pltpu.make_async_copy(yg_hbm.at[b, e, pl.ds(start, SLOT_WIN), :],
                                                wbuf.at[half, e], wsem.at[half]))
        return copies

    @pl.when(n == 0)
    def _():
        for c in window_copies(0, 0):
            c.start()

    @pl.when(n + 1 < n_steps)
    def _():
        for c in window_copies(n + 1, (n + 1) % 2):
            c.start()

    b = n // nt
    i = n % nt
    half = n % 2
    lane = lax.broadcasted_iota(I32, (ts, SLOT_WIN), 1)
    wins = [_slot_window(base_ref, b, e, i, nt, cap) for e in range(N_EXPERTS)]
    for c in window_copies(n, half):
        c.wait()
    acc = h_ref[...]
    for e0 in range(0, N_EXPERTS, MOE_STACK):
        onehot = jnp.concatenate(
            [jnp.where(slot_ref[:, e:e + 1] - wins[e][0] == lane, 1.0, 0.0).astype(BF16)
             for e in range(e0, e0 + MOE_STACK)], axis=1)
        rows = wbuf[half, e0:e0 + MOE_STACK].reshape(MOE_STACK * SLOT_WIN, wbuf.shape[-1])
        acc = acc + jnp.dot(onehot, rows, preferred_element_type=F32)
    out_ref[...] = acc

    for e in range(N_EXPERTS):
        start, hi = wins[e]

        def extra(k, carry, e=e, start=start):
            first, st = _extra_window(start, k, cap)
            copy = pltpu.make_async_copy(yg_hbm.at[b, e, pl.ds(st, SLOT_WIN), :], xbuf, xsem)
            copy.start()
            scol = slot_ref[:, e:e + 1]
            scol = jnp.where(scol >= first, scol, -1)
            oh = jnp.where(scol - st == lane, 1.0, 0.0).astype(BF16)
            copy.wait()
            out_ref[...] += jnp.dot(oh, xbuf[...], preferred_element_type=F32)
            return carry

        lax.fori_loop(1, 1 + _n_extra_windows(start, hi), extra, 0)


def _combine(base, yg, h, slot_tok, bsz, seq):
    t, dm = h.shape
    cap = yg.shape[2]
    ts = MOE_TILE
    nt = seq // ts
    n_steps = bsz * nt
    grid_spec = pltpu.PrefetchScalarGridSpec(
        num_scalar_prefetch=1,
        grid=(n_steps,),
        in_specs=[
            pl.BlockSpec(memory_space=pl.ANY),
            pl.BlockSpec((ts, dm), lambda n, base: (n, 0)),
            pl.BlockSpec((ts, N_EXPERTS), lambda n, base: (n, 0)),
        ],
        out_specs=pl.BlockSpec((ts, dm), lambda n, base: (n, 0)),
        scratch_shapes=[pltpu.VMEM((2, N_EXPERTS, SLOT_WIN, dm), BF16),
                        pltpu.VMEM((SLOT_WIN, dm), BF16),
                        pltpu.SemaphoreType.DMA((2,)),
                        pltpu.SemaphoreType.DMA(())],
    )
    return pl.pallas_call(
        functools.partial(_combine_body, nt=nt, cap=cap, n_steps=n_steps),
        grid_spec=grid_spec,
        out_shape=jax.ShapeDtypeStruct((t, dm), F32),
        compiler_params=_params("arbitrary"),
        name="moe_combine",
    )(base, yg, h, slot_tok)


def _moe(h, gain, w_router_t, w_gate, w_up, w_down, layer, bsz, seq):
    cap = CAPACITY_FACTOR * seq // N_EXPERTS
    xn, logits_t = _router_logits(h, gain, w_router_t)
    gate_t, pos_t, sel_t = _select_tokens(logits_t, bsz, seq)

    slot_t = jnp.where(sel_t > 0, pos_t, -1)
    ts = MOE_TILE
    nt = seq // ts
    starts = pos_t.reshape(N_EXPERTS, bsz, nt, ts)[..., 0]
    base = jnp.concatenate([jnp.transpose(starts, (1, 0, 2)),
                            jnp.full((bsz, N_EXPERTS, 1), cap, I32)], axis=-1).reshape(-1)
    gate_tok = gate_t.T
    g_hi = gate_tok.astype(BF16)
    r1 = gate_tok - g_hi.astype(F32)
    g_mid = r1.astype(BF16)
    g_lo = (r1 - g_mid.astype(F32)).astype(BF16)
    pieces = jnp.stack([g_hi, g_mid, g_lo], axis=-1).reshape(-1, 3 * N_EXPERTS)
    pieces = jnp.pad(pieces, ((0, 0), (0, 128 - 3 * N_EXPERTS)))

    xg = _dispatch(base, xn, slot_t, bsz, seq, cap, cw=512, out_dtype=BF16, name="moe_dispatch")
    gate_slots = _dispatch(base, pieces, slot_t, bsz, seq, cap, cw=128, out_dtype=F32, name="moe_gate_dispatch")
    yg = _expert_ffn(xg, w_gate, w_up, w_down, layer, gate_slots)
    return _combine(base, yg, h, slot_t.T, bsz, seq)


def _ple_body(h_ref, g_ref, wg_ref, p_ref, wp_ref, hres_ref, o_ref, xn_ref):
    @pl.when(pl.program_id(1) == 0)
    def _():
        xn_ref[...] = _rms(h_ref[...], g_ref[...]).astype(BF16)

    gate = jax.nn.sigmoid(jnp.dot(xn_ref[...], wg_ref[...], preferred_element_type=F32))
    ple = jnp.dot(p_ref[...].astype(BF16), wp_ref[...], preferred_element_type=F32)
    o_ref[...] = hres_ref[...] + gate * ple


def _ple(h, gain, w_gate, p, w_proj, layer):
    t, dm = h.shape
    tm, tn = 1024, 512
    pd = p.shape[2]
    return pl.pallas_call(
        _ple_body,
        grid=(t // tm, dm // tn),
        in_specs=[
            pl.BlockSpec((tm, dm), lambda i, j: (i, 0)),
            pl.BlockSpec((1, dm), lambda i, j: (0, 0)),
            pl.BlockSpec((None, dm, tn), lambda i, j: (layer, 0, j)),
            pl.BlockSpec((None, tm, pd), lambda i, j: (layer, i, 0)),
            pl.BlockSpec((None, pd, tn), lambda i, j: (layer, 0, j)),
            pl.BlockSpec((tm, tn), lambda i, j: (i, j)),
        ],
        out_specs=pl.BlockSpec((tm, tn), lambda i, j: (i, j)),
        out_shape=jax.ShapeDtypeStruct((t, dm), F32),
        scratch_shapes=[pltpu.VMEM((tm, dm), BF16)],
        compiler_params=_params("parallel", "arbitrary"),
        name="ple_gate",
    )(h, gain, w_gate, p, w_proj, h)


def _rotary_lane_order(a):
    blocks = HEAD_DIM // ROT_HALF
    order = list(range(blocks))
    order[1], order[blocks // 2] = order[blocks // 2], order[1]
    shaped = a.reshape(a.shape[:-1] + (a.shape[-1] // HEAD_DIM, blocks, ROT_HALF))
    return shaped[..., jnp.array(order), :].reshape(a.shape)


def _rope_tables(positions):
    inv_freq = jnp.power(ROPE_THETA, -jnp.arange(ROT_HALF, dtype=F32) * 2.0 / (2 * ROT_HALF))
    ang = positions.astype(F32)[..., None] * inv_freq
    cos, sin = jnp.cos(ang), jnp.sin(ang)
    gap = HEAD_DIM // 2 - ROT_HALF
    ones = jnp.ones(ang.shape[:-1] + (gap,), F32)
    zeros = jnp.zeros(ang.shape[:-1] + (gap,), F32)
    cos_t = jnp.concatenate([cos, ones, cos, ones], axis=-1)
    sin_t = jnp.concatenate([-sin, zeros, sin, zeros], axis=-1)
    return cos_t.reshape(-1, HEAD_DIM), sin_t.reshape(-1, HEAD_DIM)


def kernel(x, p, positions, norm_mix, w_in, q_norm, k_norm, w_attn_br, ssm_a_re, ssm_a_im, ssm_log_dt,
           ssm_b_re, ssm_b_im, ssm_c_re, ssm_c_im, ssm_d, w_ssm_br, w_out, norm_ffn, w_router,
           w_exp_gate, w_exp_up, w_exp_down, norm_ple, w_ple_gate, w_ple_proj):
    bsz, seq, dm = x.shape
    depth = w_in.shape[0]
    t = bsz * seq
    n_attn = 3 * HEADS_PER_GROUP * len(DILATIONS) * HEAD_DIM
    ssm_width = ssm_d.shape[1]
    nc = seq // SSM_CHUNK
    n_steps = int(math.log2(nc))
    assert seq % PERM_TILE == 0 and nc == 1 << n_steps

    cos_t, sin_t = _rope_tables(positions)
    w_in_b, w_attn_b, w_ssm_b, w_out_b = (w.astype(BF16) for w in (w_in, w_attn_br, w_ssm_br, w_out))
    w_eg_b, w_eu_b, w_ed_b = (w.astype(BF16) for w in (w_exp_gate, w_exp_up, w_exp_down))
    w_pg_b, w_pp_b = w_ple_gate.astype(BF16), w_ple_proj.astype(BF16)
    w_qk_b = _rotary_lane_order(w_in_b[:, :, :2 * n_attn // 3])
    comp, tabs = _ssm_fold(ssm_a_re, ssm_a_im, ssm_log_dt, ssm_b_re, ssm_b_im, ssm_c_re, ssm_c_im, n_steps)
    w_slab = _ssm_expand(comp)
    p_rows = p.reshape(depth, t, p.shape[-1])
    h = x.reshape(t, dm)

    for l in range(depth):
        gain = norm_mix[l][None]
        qk_gain = _rotary_lane_order(jnp.stack([q_norm[l], k_norm[l]]))[:, None, :]

        u, xn = _norm_matmul(h, gain, w_in_b, l, n_attn, ssm_width, out_dtype=F32, name="u_proj")
        gates = _sigmoid_matmul(xn, w_in_b, l, n_attn + ssm_width, 2 * dm, name="gate_proj")
        outs, lses = zip(*[_attention(_qkv_proj(xn, w_qk_b, w_in_b, l, gi, qk_gain, cos_t, sin_t), gi, bsz, seq)
                           for gi in range(len(DILATIONS))])
        gated_attn = _attn_merge_proj(outs, lses, w_attn_b, l, gates)

        y = _ssm_scan(u, w_slab, tabs, l, bsz, seq)
        merged = _ssm_glu_merge(y, u, ssm_d[l][None], w_ssm_b, l, gated_attn, gates)
        h = _resid_matmul(merged, w_out_b, l, h)

        h = _moe(h, norm_ffn[l][None], w_router[l].T, w_eg_b, w_eu_b, w_ed_b, l, bsz, seq)
        h = _ple(h, norm_ple[l][None], w_pg_b, p_rows, w_pp_b, l)
    return h.reshape(bsz, seq, dm)
```

```python
import functools
import math

import jax
import jax.numpy as jnp
from jax import lax
from jax.experimental import pallas as pl
from jax.experimental.pallas import tpu as pltpu

F32 = jnp.float32
BF16 = jnp.bfloat16
I32 = jnp.int32

NORM_EPS = 1e-6
MASK_VALUE = -1e30
ROPE_THETA = 500000.0

LANES = 128
HEAD_DIM = 128
HEADS_PER_GROUP = 4
GROUP_WIDTH = HEADS_PER_GROUP * HEAD_DIM
DILATIONS = (1, 4, 16)
N_SIDE = 64
ROT_HALF = 16

SSM_GROUP = 16
SSM_STATE = 64
SSM_CHUNK = 16

N_EXPERTS = 16
CAPACITY_FACTOR = 2
SLOT_WIN = 128
SLOT_ALIGN = 16
MOE_TILE = 512
MOE_STACK = 4

PERM_TILE = 1024
BF16_LHS_ROWS = 2048
VMEM_LIMIT = 56 * 1024 * 1024


def _params(*sem):
    return pltpu.CompilerParams(dimension_semantics=sem, vmem_limit_bytes=VMEM_LIMIT)


def _rms(x, gain):
    var = jnp.mean(x * x, axis=-1, keepdims=True)
    return x * lax.rsqrt(var + NORM_EPS) * gain


def _qkv_body(xn_ref, wq_ref, wk_ref, wv_ref, qkg_ref, cos_ref, sin_ref, o_ref, slab_ref, *, d):
    tm = xn_ref.shape[0]
    n = tm // d
    xn = xn_ref[...]
    cos = cos_ref[...]
    sin = sin_ref[...]
    for part, w_ref in enumerate((wq_ref, wk_ref, wv_ref)):
        acc = jnp.dot(xn, w_ref[...], preferred_element_type=F32)
        for hs in range(HEADS_PER_GROUP):
            a = acc[:, hs * HEAD_DIM:(hs + 1) * HEAD_DIM]
            if part < 2:
                a = _rms(a, qkg_ref[part])
                a = a * cos + pltpu.roll(a, HEAD_DIM // 2, 1) * sin
            cols = slice(part * GROUP_WIDTH + hs * HEAD_DIM, part * GROUP_WIDTH + (hs + 1) * HEAD_DIM)
            if d == 1:
                o_ref[:, cols] = a.astype(BF16)
            else:
                slab_ref[part, hs] = a
                for r in range(d):
                    o_ref[r * n:(r + 1) * n, cols] = slab_ref[part, hs, pl.ds(r, n, stride=d), :].astype(BF16)


def _qkv_proj(xn, w_qk, w_in, layer, gi, qk_gain, cos, sin):
    t, dm = xn.shape
    tm, tn = PERM_TILE, GROUP_WIDTH
    ng = len(DILATIONS)

    def weight(part):
        return pl.BlockSpec((None, dm, tn), lambda i: (layer, 0, part * ng + gi))

    return pl.pallas_call(
        functools.partial(_qkv_body, d=DILATIONS[gi]),
        grid=(t // tm,),
        in_specs=[
            pl.BlockSpec((tm, dm), lambda i: (i, 0)),
            weight(0), weight(1), weight(2),
            pl.BlockSpec((2, 1, HEAD_DIM), lambda i: (0, 0, 0)),
            pl.BlockSpec((tm, HEAD_DIM), lambda i: (i, 0)),
            pl.BlockSpec((tm, HEAD_DIM), lambda i: (i, 0)),
        ],
        out_specs=pl.BlockSpec((tm, 3 * tn), lambda i: (i, 0)),
        out_shape=jax.ShapeDtypeStruct((t, 3 * tn), BF16),
        scratch_shapes=[pltpu.VMEM((3, HEADS_PER_GROUP, tm, HEAD_DIM), F32)],
        compiler_params=_params("parallel"),
        name=f"qkv_proj_d{DILATIONS[gi]}",
    )(xn, w_qk, w_qk, w_in, qk_gain, cos, sin)


def _norm_mm_body(h_ref, g_ref, w_ref, o_ref, xn_out_ref, xn_ref):
    @pl.when(pl.program_id(1) == 0)
    def _():
        xn = _rms(h_ref[...], g_ref[...]).astype(BF16)
        xn_ref[...] = xn
        xn_out_ref[...] = xn

    o_ref[...] = jnp.dot(xn_ref[...], w_ref[...], preferred_element_type=F32).astype(o_ref.dtype)


def _norm_matmul(h, gain, w, layer, col0, ncols, *, out_dtype, tn=512, name):
    t, dm = h.shape
    tm = 1024
    nj = ncols // tn
    blk0 = col0 // tn
    return pl.pallas_call(
        _norm_mm_body,
        grid=(t // tm, nj),
        in_specs=[
            pl.BlockSpec((tm, dm), lambda i, j: (i, 0)),
            pl.BlockSpec((1, dm), lambda i, j: (0, 0)),
            pl.BlockSpec((None, dm, tn), lambda i, j: (layer, 0, blk0 + j)),
        ],
        out_specs=[pl.BlockSpec((tm, tn), lambda i, j: (i, j)),
                   pl.BlockSpec((tm, dm), lambda i, j: (i, 0))],
        out_shape=[jax.ShapeDtypeStruct((t, nj * tn), out_dtype), jax.ShapeDtypeStruct((t, dm), BF16)],
        scratch_shapes=[pltpu.VMEM((tm, dm), BF16)],
        compiler_params=_params("parallel", "arbitrary"),
        name=name,
    )(h, gain, w)


def _act_mm_body(x_ref, w_ref, o_ref):
    o_ref[...] = jax.nn.sigmoid(jnp.dot(x_ref[...], w_ref[...], preferred_element_type=F32)).astype(o_ref.dtype)


def _sigmoid_matmul(x, w, layer, col0, ncols, *, tn=512, name):
    t, dm = x.shape
    tm = BF16_LHS_ROWS
    nj = ncols // tn
    blk0 = col0 // tn
    return pl.pallas_call(
        _act_mm_body,
        grid=(t // tm, nj),
        in_specs=[
            pl.BlockSpec((tm, dm), lambda i, j: (i, 0)),
            pl.BlockSpec((None, dm, tn), lambda i, j: (layer, 0, blk0 + j)),
        ],
        out_specs=pl.BlockSpec((tm, tn), lambda i, j: (i, j)),
        out_shape=jax.ShapeDtypeStruct((t, nj * tn), BF16),
        compiler_params=_params("parallel", "arbitrary"),
        name=name,
    )(x, w)


def _attn_body(q_ref, kp_ref, km_ref, kn_ref, vp_ref, vm_ref, vn_ref, o_ref, l_ref,
               kw_ref, vw_ref, os_ref, ls_ref, *, tq, sub_len):
    c = pl.program_id(2)
    kw_ref[0:N_SIDE] = kp_ref[...]
    kw_ref[N_SIDE:N_SIDE + tq] = km_ref[...].reshape(tq, GROUP_WIDTH)
    kw_ref[N_SIDE + tq:] = kn_ref[...]
    vw_ref[0:N_SIDE] = vp_ref[...]
    vw_ref[N_SIDE:N_SIDE + tq] = vm_ref[...].reshape(tq, GROUP_WIDTH)
    vw_ref[N_SIDE + tq:] = vn_ref[...]
    q = q_ref[...].reshape(tq, GROUP_WIDTH)

    sb = min(128, tq)
    nk = sb + 2 * N_SIDE
    scale = HEAD_DIM ** -0.5
    row = lax.broadcasted_iota(I32, (sb, nk), 0)
    col = lax.broadcasted_iota(I32, (sb, nk), 1)
    band = jnp.abs(col - row - N_SIDE) <= N_SIDE
    for i in range(tq // sb):
        kpos = c * tq + (i * sb - N_SIDE) + col
        valid = band & (kpos >= 0) & (kpos < sub_len)
        for hs in range(HEADS_PER_GROUP):
            lanes = slice(hs * HEAD_DIM, (hs + 1) * HEAD_DIM)
            qh = q[i * sb:(i + 1) * sb, lanes]
            kh = kw_ref[i * sb:i * sb + nk, lanes]
            vh = vw_ref[i * sb:i * sb + nk, lanes]
            s = lax.dot_general(qh, kh, (((1,), (1,)), ((), ())), preferred_element_type=F32) * scale
            s = jnp.where(valid, s, MASK_VALUE)
            m = jnp.max(s, axis=-1, keepdims=True)
            e = jnp.exp(s - m)
            den = jnp.sum(e, axis=-1, keepdims=True)
            o = jnp.dot((e / den).astype(BF16), vh, preferred_element_type=F32)
            os_ref[i * sb:(i + 1) * sb, lanes] = o
            ls_ref[i * sb:(i + 1) * sb, lanes] = jnp.broadcast_to(m + jnp.log(den), (sb, HEAD_DIM))
    o_ref[...] = os_ref[...].reshape(o_ref.shape)
    l_ref[...] = ls_ref[...].reshape(l_ref.shape)


def _attention(qkv, gi, bsz, seq):
    d = DILATIONS[gi]
    t = bsz * seq
    sub_len = seq // d
    nbt = PERM_TILE // (N_SIDE * d)
    ntile = seq // PERM_TILE
    tq = min(512, sub_len)
    nbq = tq // N_SIDE
    nblk = sub_len // N_SIDE
    view = (bsz, ntile, d, nbt, N_SIDE, qkv.shape[1])
    oview = (bsz, ntile, d, nbt, N_SIDE, GROUP_WIDTH)
    gw = GROUP_WIDTH

    if nbt >= nbq:
        per = nbt // nbq
        main_shape = (None, None, None, nbq, N_SIDE, gw)

        def main_idx(col):
            return lambda b, r, c: (b, c // per, r, c % per, 0, col)
    else:
        main_shape = (None, nbq // nbt, None, nbt, N_SIDE, gw)

        def main_idx(col):
            return lambda b, r, c: (b, c, r, 0, 0, col)

    halo_shape = (None, None, None, None, N_SIDE, gw)

    def prev_idx(col):
        def f(b, r, c):
            n = jnp.maximum(c * nbq - 1, 0)
            return (b, n // nbt, r, n % nbt, 0, col)
        return f

    def next_idx(col):
        def f(b, r, c):
            n = jnp.minimum((c + 1) * nbq, nblk - 1)
            return (b, n // nbt, r, n % nbt, 0, col)
        return f

    qc, kc, vc = 0, 1, 2
    x = qkv.reshape(view)
    o, l = pl.pallas_call(
        functools.partial(_attn_body, tq=tq, sub_len=sub_len),
        grid=(bsz, d, sub_len // tq),
        in_specs=[
            pl.BlockSpec(main_shape, main_idx(qc)),
            pl.BlockSpec(halo_shape, prev_idx(kc)),
            pl.BlockSpec(main_shape, main_idx(kc)),
            pl.BlockSpec(halo_shape, next_idx(kc)),
            pl.BlockSpec(halo_shape, prev_idx(vc)),
            pl.BlockSpec(main_shape, main_idx(vc)),
            pl.BlockSpec(halo_shape, next_idx(vc)),
        ],
        out_specs=[pl.BlockSpec(main_shape, main_idx(0)), pl.BlockSpec(main_shape, main_idx(0))],
        out_shape=[jax.ShapeDtypeStruct(oview, F32), jax.ShapeDtypeStruct(oview, F32)],
        scratch_shapes=[pltpu.VMEM((tq + 2 * N_SIDE, gw), BF16), pltpu.VMEM((tq + 2 * N_SIDE, gw), BF16),
                        pltpu.VMEM((tq, gw), F32), pltpu.VMEM((tq, gw), F32)],
        compiler_params=_params("parallel", "parallel", "arbitrary"),
        name=f"dilated_attn_d{d}",
    )(x, x, x, x, x, x, x)
    return o.reshape(t, gw), l.reshape(t, gw)


def _attn_merge_body(o0, l0, o1, l1, o2, l2, w_ref, g_ref, out_ref, comb_ref, so_ref, sl_ref):
    tm = out_ref.shape[0]

    @pl.when(pl.program_id(1) == 0)
    def _():
        for gi, (o_ref, l_ref) in enumerate(((o0, l0), (o1, l1), (o2, l2))):
            d = DILATIONS[gi]
            n = tm // d
            for hs in range(HEADS_PER_GROUP):
                lanes = slice(hs * HEAD_DIM, (hs + 1) * HEAD_DIM)
                for r in range(d):
                    ov = o_ref[r * n:(r + 1) * n, lanes]
                    lv = l_ref[r * n:(r + 1) * n, lanes]
                    if d == 1:
                        so_ref[gi, hs] = ov
                        sl_ref[gi, hs] = lv
                    else:
                        so_ref[gi, hs, pl.ds(r, n, stride=d), :] = ov
                        sl_ref[gi, hs, pl.ds(r, n, stride=d), :] = lv
        for hs in range(HEADS_PER_GROUP):
            ls = [sl_ref[gi, hs] for gi in range(3)]
            mx = jnp.maximum(jnp.maximum(ls[0], ls[1]), ls[2])
            ws = [jnp.exp(l - mx) for l in ls]
            num = ws[0] * so_ref[0, hs] + ws[1] * so_ref[1, hs] + ws[2] * so_ref[2, hs]
            comb = num / (ws[0] + ws[1] + ws[2])
            comb_ref[:, hs * HEAD_DIM:(hs + 1) * HEAD_DIM] = comb.astype(BF16)

    acc = jnp.dot(comb_ref[...], w_ref[...], preferred_element_type=F32)
    out_ref[...] = (acc * g_ref[...].astype(F32)).astype(out_ref.dtype)


def _attn_merge_proj(outs, lses, w_attn, layer, gates):
    t = outs[0].shape[0]
    dm = w_attn.shape[2]
    tm, tn = PERM_TILE, 512
    row = pl.BlockSpec((tm, GROUP_WIDTH), lambda i, j: (i, 0))
    return pl.pallas_call(
        _attn_merge_body,
        grid=(t // tm, dm // tn),
        in_specs=[row, row, row, row, row, row,
                  pl.BlockSpec((None, GROUP_WIDTH, tn), lambda i, j: (layer, 0, j)),
                  pl.BlockSpec((tm, tn), lambda i, j: (i, j))],
        out_specs=pl.BlockSpec((tm, tn), lambda i, j: (i, j)),
        out_shape=jax.ShapeDtypeStruct((t, dm), BF16),
        scratch_shapes=[pltpu.VMEM((tm, GROUP_WIDTH), BF16),
                        pltpu.VMEM((3, HEADS_PER_GROUP, tm, HEAD_DIM), F32),
                        pltpu.VMEM((3, HEADS_PER_GROUP, tm, HEAD_DIM), F32)],
        compiler_params=_params("parallel", "arbitrary"),
        name="attn_merge_proj",
    )(outs[0], lses[0], outs[1], lses[1], outs[2], lses[2], w_attn, gates)


def _ssm_fold_body(sp_re, sp_im, si_re, si_im, sot_re, sot_im, bb_re, bb_im, c_re, c_im, ct_re, ct_im,
                   rep_ref, tile_ref, o_ref):
    lc, hg, p = SSM_CHUNK, SSM_GROUP, SSM_STATE
    gps = LANES // hg
    kw = lc * hg
    hi_prec = lax.Precision.HIGHEST
    nt = (((1,), (1,)), ((), ()))
    for g in range(gps):
        strip = jnp.zeros((hg, 2 * kw), F32)
        injections = []
        for d in range(2):
            br, bi = bb_re[d, g], bb_im[d, g]
            cr, ci = c_re[d, g], c_im[d, g]
            pr, pi = sp_re[d, g][:, None, :], sp_im[d, g][:, None, :]
            y_re = (pr * cr[None] - pi * ci[None]).reshape(2 * kw, p)
            y_im = (pr * ci[None] + pi * cr[None]).reshape(2 * kw, p)
            strip = strip + lax.dot_general(jnp.concatenate([br, -bi], axis=1), jnp.concatenate([y_re, y_im], axis=1),
                                            nt, precision=hi_prec, preferred_element_type=F32)
            qr, qi = si_re[d, g][:, None, :], si_im[d, g][:, None, :]
            injections += [(qr * br[None] - qi * bi[None]).reshape(kw, p),
                           (qr * bi[None] + qi * br[None]).reshape(kw, p)]
            ar = jnp.dot(sot_re[d, g], rep_ref[...], precision=hi_prec, preferred_element_type=F32)
            ai = jnp.dot(sot_im[d, g], rep_ref[...], precision=hi_prec, preferred_element_type=F32)
            xr = jnp.dot(ct_re[d, g], tile_ref[...], precision=hi_prec, preferred_element_type=F32)
            xi = jnp.dot(ct_im[d, g], tile_ref[...], precision=hi_prec, preferred_element_type=F32)
            row0 = ((d * (gps // 2) + g // 2) * 2) * LANES + (g % 2) * p
            o_ref[2, row0:row0 + p, :] = (ar * xr - ai * xi).astype(o_ref.dtype)
            o_ref[2, row0 + LANES:row0 + LANES + p, :] = (-(ar * xi + ai * xr)).astype(o_ref.dtype)
        state_in = jnp.concatenate(injections, axis=1)
        for s in range(lc):
            rows = slice(s * LANES + g * hg, s * LANES + (g + 1) * hg)
            o_ref[0, rows, :] = strip[:, (lc - 1 - s) * hg:(lc - 1 - s) * hg + kw].astype(o_ref.dtype)
            o_ref[1, rows, :] = state_in[s * hg:(s + 1) * hg, :].astype(o_ref.dtype)


def _ssm_fold(a_re, a_im, log_dt, b_re, b_im, c_re, c_im, n_steps):
    f32 = lambda v: v.astype(F32)
    a_re, a_im, b_re, b_im, c_re, c_im = map(f32, (a_re, a_im, b_re, b_im, c_re, c_im))
    depth, _, g, p = a_re.shape
    lc, hg = SSM_CHUNK, SSM_GROUP
    gps = LANES // hg
    ns, npair = g // gps, gps // 2
    kw = lc * hg
    side = lc * LANES
    dt = jnp.exp(f32(log_dt))[..., None]

    def lam_pow(n):
        n = n.astype(F32)[:, None]
        mag = jnp.exp((a_re * dt)[..., None, :] * n)
        ang = (a_im * dt)[..., None, :] * n
        return mag * jnp.cos(ang), mag * jnp.sin(ang)

    pw_re, pw_im = lam_pow(jnp.arange(lc + 1))
    lb_re, lb_im = pw_re[..., 1, :], pw_im[..., 1, :]
    den = a_re * a_re + a_im * a_im
    nr, ni = lb_re - 1.0, lb_im
    coef_re = ((nr * a_re + ni * a_im) / den)[..., None, :]
    coef_im = ((ni * a_re - nr * a_im) / den)[..., None, :]
    bt_re, bt_im = jnp.swapaxes(b_re, -1, -2), jnp.swapaxes(b_im, -1, -2)
    bb_re = coef_re * bt_re - coef_im * bt_im
    bb_im = coef_re * bt_im + coef_im * bt_re

    def per_dir(pw, fwd, bwd):
        return jnp.stack([pw[:, 0][..., fwd, :], pw[:, 1][..., bwd, :]], axis=1)

    steps = jnp.arange(lc)
    slot = jnp.arange(2 * lc)
    f_on = ((slot >= lc - 1) & (slot < 2 * lc - 1)).astype(F32)[:, None]
    b_on = (slot <= lc - 1).astype(F32)[:, None]
    f_idx = jnp.clip(slot - (lc - 1), 0, lc)
    b_idx = jnp.clip(lc - 1 - slot, 0, lc)
    strip_pw = lambda pw: jnp.stack([pw[:, 0][..., f_idx, :] * f_on, pw[:, 1][..., b_idx, :] * b_on], axis=1)
    sp_re, sp_im = strip_pw(pw_re), strip_pw(pw_im)
    si_re, si_im = per_dir(pw_re, lc - 1 - steps, steps), per_dir(pw_im, lc - 1 - steps, steps)
    so_re, so_im = per_dir(pw_re, steps + 1, lc - steps), per_dir(pw_im, steps + 1, lc - steps)
    sot_re, sot_im = jnp.swapaxes(so_re, -1, -2), jnp.swapaxes(so_im, -1, -2)
    ct_re, ct_im = jnp.swapaxes(c_re, -1, -2), jnp.swapaxes(c_im, -1, -2)
    col = jnp.arange(kw)[None, :]
    rep = (jnp.arange(lc)[:, None] == col // hg).astype(F32)
    tile = (jnp.arange(hg)[:, None] == col % hg).astype(F32)

    def rows_spec(rows):
        return pl.BlockSpec((None, 2, gps, rows, p), lambda l, s: (l, 0, s, 0, 0))

    def cols_spec(cols):
        return pl.BlockSpec((None, 2, gps, p, cols), lambda l, s: (l, 0, s, 0, 0))

    const = pl.BlockSpec((lc, kw), lambda l, s: (0, 0))
    comp = pl.pallas_call(
        _ssm_fold_body,
        grid=(depth, ns),
        in_specs=[rows_spec(2 * lc), rows_spec(2 * lc), rows_spec(lc), rows_spec(lc), cols_spec(lc), cols_spec(lc),
                  rows_spec(hg), rows_spec(hg), rows_spec(hg), rows_spec(hg), cols_spec(hg), cols_spec(hg),
                  const, const],
        out_specs=pl.BlockSpec((None, None, 3, side, kw), lambda l, s: (l, s, 0, 0, 0)),
        out_shape=jax.ShapeDtypeStruct((depth, ns, 3, side, kw), BF16),
        compiler_params=_params("parallel", "parallel"),
        name="s5_fold_params",
    )(sp_re, sp_im, si_re, si_im, sot_re, sot_im, bb_re, bb_im, c_re, c_im, ct_re, ct_im, rep, tile)

    sc_re, sc_im = lam_pow(lc * (2 ** jnp.arange(n_steps)))
    tab = jnp.stack([sc_re, sc_im], axis=-2).reshape(depth, 2, ns, npair, 2, n_steps, 2, p)
    tab = jnp.transpose(tab, (0, 2, 1, 3, 5, 6, 4, 7)).reshape(depth, ns, 2 * npair * n_steps * 2, 2 * p)
    return comp, tab


def _ssm_expand_constants():
    lc, hg, p = SSM_CHUNK, SSM_GROUP, SSM_STATE
    r = jnp.arange(lc * hg)[:, None]
    c = jnp.arange(lc * LANES)[None, :]
    e_time = (r // hg == c // LANES) & (r % hg == c % hg)
    e_state = (r // (2 * p) == c // (lc * LANES // 2)) & ((r // p) % 2 == (c // LANES) % 2) & (r % p == c % p)
    rr = jnp.arange(lc * LANES)[:, None]
    lane_group = lambda i: (i % LANES) // hg
    state_group = lambda i: 2 * ((i // (2 * LANES)) % (LANES // hg // 2)) + (i % LANES) // p
    masks = [lane_group(rr) == lane_group(c), lane_group(rr) == state_group(c), state_group(rr) == lane_group(c)]
    return (jnp.stack([e_time, e_state, e_time]).astype(BF16), jnp.stack(masks).astype(BF16))


def _ssm_expand_body(c_ref, e_ref, m_ref, o_ref):
    rows = 512
    for r in range(0, o_ref.shape[0], rows):
        acc = jnp.dot(c_ref[r:r + rows], e_ref[...], preferred_element_type=F32)
        o_ref[r:r + rows] = jnp.where(m_ref[r:r + rows] > 0, acc, 0.0).astype(o_ref.dtype)


def _ssm_expand(comp):
    depth, ns, three, side, kw = comp.shape
    e, mask = _ssm_expand_constants()
    return pl.pallas_call(
        _ssm_expand_body,
        grid=(three, depth * ns),
        in_specs=[
            pl.BlockSpec((None, None, None, side, kw), lambda m, n: (n // ns, n % ns, m, 0, 0)),
            pl.BlockSpec((None, kw, side), lambda m, n: (m, 0, 0)),
            pl.BlockSpec((None, side, side), lambda m, n: (m, 0, 0)),
        ],
        out_specs=pl.BlockSpec((None, None, None, side, side), lambda m, n: (n // ns, n % ns, m, 0, 0)),
        out_shape=jax.ShapeDtypeStruct((depth, ns, three, side, side), BF16),
        compiler_params=_params("arbitrary", "arbitrary"),
        name="s5_expand_weights",
    )(comp, e, mask)


def _ssm_body(u_ref, toe_ref, sin_ref, sout_ref, tab_ref, y_ref, lhs_ref, inj_ref, st_ref, intra_ref, *, n_steps):
    seq = u_ref.shape[0]
    lc = SSM_CHUNK
    nc = seq // lc
    n_slabs = inj_ref.shape[0]
    per_dir = n_slabs // 4
    row = lax.broadcasted_iota(I32, (nc, LANES), 0)

    for s in range(lc):
        lhs_ref[:, s * LANES:(s + 1) * LANES] = u_ref[pl.ds(s, nc, stride=lc), :].astype(BF16)
    lhs = lhs_ref[...]

    for j in range(0, n_slabs, 2):
        r = jnp.dot(lhs, sin_ref[:, j * LANES:(j + 2) * LANES], preferred_element_type=F32)
        inj_ref[j] = r[:, :LANES]
        inj_ref[j + 1] = r[:, LANES:]

    def shifted(x, sh, direction):
        if sh % 8 == 0:
            zeros = jnp.zeros((sh, LANES), x.dtype)
            if direction == 0:
                return jnp.concatenate([zeros, x[:nc - sh]], axis=0)
            return jnp.concatenate([x[sh:], zeros], axis=0)
        if direction == 0:
            return jnp.where(row >= sh, pltpu.roll(x, sh, 0), 0.0)
        return jnp.where(row < nc - sh, pltpu.roll(x, nc - sh, 0), 0.0)

    for jq in range(2 * per_dir):
        direction = jq // per_dir
        xr = inj_ref[2 * jq]
        xi = inj_ref[2 * jq + 1]
        for k in range(n_steps):
            base = (jq * n_steps + k) * 2
            ar = tab_ref[base:base + 1, :]
            ai = tab_ref[base + 1:base + 2, :]
            sr = shifted(xr, 1 << k, direction)
            si = shifted(xi, 1 << k, direction)
            xr, xi = xr + ar * sr - ai * si, xi + ar * si + ai * sr
        st_ref[2 * jq] = shifted(xr, 1, direction).astype(BF16)
        st_ref[2 * jq + 1] = shifted(xi, 1, direction).astype(BF16)
        cols = slice(2 * jq * LANES, (2 * jq + 2) * LANES)
        intra_ref[jq] = jnp.dot(lhs, toe_ref[:, cols], preferred_element_type=F32)

    states = jnp.concatenate([st_ref[j] for j in range(n_slabs)], axis=1)
    for t in range(0, lc, 2):
        cols = slice(t * LANES, (t + 2) * LANES)
        r = intra_ref[t // 2] + jnp.dot(states, sout_ref[:, cols], preferred_element_type=F32)
        y_ref[pl.ds(t, nc, stride=lc), :] = r[:, :LANES]
        y_ref[pl.ds(t + 1, nc, stride=lc), :] = r[:, LANES:]


def _ssm_scan(u, w_slab, tab, layer, bsz, seq):
    t, width = u.shape
    ns = width // LANES
    nc = seq // SSM_CHUNK
    n_steps = int(math.log2(nc))
    side = SSM_CHUNK * LANES
    n_slabs = side // LANES
    once = pl.Buffered(1)

    def weight(m):
        return pl.BlockSpec((None, None, None, side, side), lambda g, b: (layer, g, m, 0, 0), pipeline_mode=once)

    return pl.pallas_call(
        functools.partial(_ssm_body, n_steps=n_steps),
        grid=(ns, bsz),
        in_specs=[
            pl.BlockSpec((seq, LANES), lambda g, b: (b, g)),
            weight(0), weight(1), weight(2),
            pl.BlockSpec((None, None, tab.shape[-2], LANES), lambda g, b: (layer, g, 0, 0)),
        ],
        out_specs=pl.BlockSpec((seq, LANES), lambda g, b: (b, g)),
        out_shape=jax.ShapeDtypeStruct((t, width), F32),
        scratch_shapes=[pltpu.VMEM((nc, side), BF16),
                        pltpu.VMEM((n_slabs, nc, LANES), F32),
                        pltpu.VMEM((n_slabs, nc, LANES), BF16),
                        pltpu.VMEM((n_slabs // 2, nc, 2 * LANES), F32)],
        compiler_params=_params("arbitrary", "arbitrary"),
        name="s5_chunk_scan",
    )(u, w_slab, w_slab, w_slab, tab)


def _ssm_glu_body(y_ref, u_ref, d_ref, wa_ref, wb_ref, ga_ref, gs_ref, o_ref, act_ref):
    @pl.when(pl.program_id(1) == 0)
    def _():
        act_ref[...] = jax.nn.gelu(y_ref[...] + d_ref[...] * u_ref[...]).astype(BF16)

    act = act_ref[...]
    za = jnp.dot(act, wa_ref[...], preferred_element_type=F32)
    zb = jnp.dot(act, wb_ref[...], preferred_element_type=F32)
    s_branch = za * jax.nn.sigmoid(zb)
    o_ref[...] = (ga_ref[...].astype(F32) + gs_ref[...].astype(F32) * s_branch).astype(o_ref.dtype)


def _ssm_glu_merge(y, u, d_skip, w_ssm, layer, gated_attn, gates):
    t, width = y.shape
    dm = gated_attn.shape[1]
    tm, tn = 1024, 512
    nj = dm // tn
    return pl.pallas_call(
        _ssm_glu_body,
        grid=(t // tm, nj),
        in_specs=[
            pl.BlockSpec((tm, width), lambda i, j: (i, 0)),
            pl.BlockSpec((tm, width), lambda i, j: (i, 0)),
            pl.BlockSpec((1, width), lambda i, j: (0, 0)),
            pl.BlockSpec((None, width, tn), lambda i, j: (layer, 0, j)),
            pl.BlockSpec((None, width, tn), lambda i, j: (layer, 0, j + nj)),
            pl.BlockSpec((tm, tn), lambda i, j: (i, j)),
            pl.BlockSpec((tm, tn), lambda i, j: (i, j + nj)),
        ],
        out_specs=pl.BlockSpec((tm, tn), lambda i, j: (i, j)),
        out_shape=jax.ShapeDtypeStruct((t, dm), BF16),
        scratch_shapes=[pltpu.VMEM((tm, width), BF16)],
        compiler_params=_params("parallel", "arbitrary"),
        name="s5_glu_merge",
    )(y, u, d_skip, w_ssm, w_ssm, gated_attn, gates)


def _resid_mm_body(x_ref, w_ref, h_ref, o_ref):
    o_ref[...] = h_ref[...] + jnp.dot(x_ref[...], w_ref[...], preferred_element_type=F32)


def _resid_matmul(x, w, layer, h):
    t, k = x.shape
    dm = w.shape[2]
    tm, tn = BF16_LHS_ROWS, 512
    return pl.pallas_call(
        _resid_mm_body,
        grid=(t // tm, dm // tn),
        in_specs=[
            pl.BlockSpec((tm, k), lambda i, j: (i, 0)),
            pl.BlockSpec((None, k, tn), lambda i, j: (layer, 0, j)),
            pl.BlockSpec((tm, tn), lambda i, j: (i, j)),
        ],
        out_specs=pl.BlockSpec((tm, tn), lambda i, j: (i, j)),
        out_shape=jax.ShapeDtypeStruct((t, dm), F32),
        compiler_params=_params("parallel", "arbitrary"),
        name="out_proj",
    )(x, w, h)


def _router_body(h_ref, g_ref, wr_ref, xn_ref, lg_ref):
    xn = _rms(h_ref[...], g_ref[...])
    xn_ref[...] = xn.astype(BF16)
    lg_ref[...] = lax.dot_general(wr_ref[...], xn, (((1,), (1,)), ((), ())),
                                  precision=lax.Precision.HIGHEST, preferred_element_type=F32)


def _router_logits(h, gain, w_router_t):
    t, dm = h.shape
    tm = 512
    return pl.pallas_call(
        _router_body,
        grid=(t // tm,),
        in_specs=[
            pl.BlockSpec((tm, dm), lambda i: (i, 0)),
            pl.BlockSpec((1, dm), lambda i: (0, 0)),
            pl.BlockSpec((N_EXPERTS, dm), lambda i: (0, 0)),
        ],
        out_specs=[pl.BlockSpec((tm, dm), lambda i: (i, 0)),
                   pl.BlockSpec((N_EXPERTS, tm), lambda i: (0, i))],
        out_shape=[jax.ShapeDtypeStruct((t, dm), BF16), jax.ShapeDtypeStruct((N_EXPERTS, t), F32)],
        compiler_params=_params("parallel"),
        name="router_logits",
    )(h, gain, w_router_t)


def _select_body(lg_ref, gate_ref, pos_ref, sel_ref, *, cap):
    lg = lg_ref[...]
    ne, s = lg.shape
    m = jnp.max(lg, axis=0, keepdims=True)
    e = jnp.exp(lg - m)
    aff = e / jnp.sum(e, axis=0, keepdims=True)
    bits = lax.bitcast_convert_type(aff, I32)

    def count(ind):
        return jnp.sum(ind, axis=1, keepdims=True)

    def value_bit(k, thr):
        cand = thr | jnp.left_shift(jnp.int32(1), 30 - k)
        return jnp.where(count(jnp.where(bits >= cand, 1.0, 0.0)) >= cap, cand, thr)

    thr = lax.fori_loop(0, 31, value_bit, jnp.zeros((ne, 1), I32))
    above = jnp.where(bits > thr, 1.0, 0.0)
    tie = jnp.where(bits == thr, 1.0, 0.0)
    need = cap - count(above)
    idx = lax.broadcasted_iota(I32, (ne, s), 1)
    n_bits = int(math.log2(s)) + 1

    def index_bit(k, bound):
        cand = bound | jnp.left_shift(jnp.int32(1), n_bits - 1 - k)
        below = count(jnp.where(idx < cand, tie, 0.0))
        return jnp.where(below < need, cand, bound)

    bound = lax.fori_loop(0, n_bits, index_bit, jnp.zeros((ne, 1), I32))
    sel = above + jnp.where(idx <= bound, tie, 0.0)
    gate_ref[...] = sel * aff
    sel_ref[...] = sel.astype(I32)

    blk = 256
    tri = jnp.where(lax.broadcasted_iota(I32, (blk, blk), 0) <= lax.broadcasted_iota(I32, (blk, blk), 1),
                    1.0, 0.0).astype(BF16)
    carry = jnp.zeros((ne, 1), F32)
    for j in range(s // blk):
        seg = sel[:, j * blk:(j + 1) * blk]
        inc = jnp.dot(seg.astype(BF16), tri, preferred_element_type=F32)
        pos_ref[:, j * blk:(j + 1) * blk] = (inc - seg + carry).astype(I32)
        carry = carry + inc[:, blk - 1:blk]


def _select_tokens(logits_t, bsz, seq):
    cap = CAPACITY_FACTOR * seq // N_EXPERTS
    t = bsz * seq
    spec = pl.BlockSpec((N_EXPERTS, seq), lambda b: (0, b))
    return pl.pallas_call(
        functools.partial(_select_body, cap=cap),
        grid=(bsz,),
        in_specs=[spec],
        out_specs=[spec, spec, spec],
        out_shape=[jax.ShapeDtypeStruct((N_EXPERTS, t), F32), jax.ShapeDtypeStruct((N_EXPERTS, t), I32),
                   jax.ShapeDtypeStruct((N_EXPERTS, t), I32)],
        compiler_params=_params("parallel"),
        name="expert_choice_select",
    )(logits_t)


def _slot_window(base_ref, b, e, i, nt, cap):
    k = (b * N_EXPERTS + e) * (nt + 1) + i
    lo = base_ref[k]
    start = jnp.minimum((lo // SLOT_ALIGN) * SLOT_ALIGN, cap - SLOT_WIN)
    return pl.multiple_of(start, SLOT_ALIGN), base_ref[k + 1]


def _n_extra_windows(start, hi):
    return (jnp.maximum(hi - (start + SLOT_WIN), 0) + SLOT_WIN - 1) // SLOT_WIN


def _extra_window(start, k, cap):
    first = start + SLOT_WIN * k
    return first, pl.multiple_of(jnp.minimum(first, cap - SLOT_WIN), SLOT_ALIGN)


def _dispatch_body(base_ref, x_ref, slot_ref, out_ref, *, nt, cap):
    b = pl.program_id(0)
    i = pl.program_id(2)
    ts = x_ref.shape[0]

    @pl.when(i == 0)
    def _():
        out_ref[...] = jnp.zeros_like(out_ref)

    x = x_ref[...]
    wio = lax.broadcasted_iota(I32, (SLOT_WIN, ts), 0)
    wins = [_slot_window(base_ref, b, e, i, nt, cap) for e in range(N_EXPERTS)]
    onehot = jnp.concatenate(
        [jnp.where(slot_ref[e:e + 1, :] - wins[e][0] == wio, 1.0, 0.0).astype(x.dtype) for e in range(N_EXPERTS)],
        axis=0)
    res = jnp.dot(onehot, x, preferred_element_type=F32)
    for e in range(N_EXPERTS):
        out_ref[e, pl.ds(wins[e][0], SLOT_WIN), :] += res[e * SLOT_WIN:(e + 1) * SLOT_WIN].astype(out_ref.dtype)

    for e in range(N_EXPERTS):
        start, hi = wins[e]

        def extra(k, carry, e=e, start=start):
            first, st = _extra_window(start, k, cap)
            srow = slot_ref[e:e + 1, :]
            srow = jnp.where(srow >= first, srow, -1)
            oh = jnp.where(srow - st == wio, 1.0, 0.0).astype(x.dtype)
            out_ref[e, pl.ds(st, SLOT_WIN), :] += jnp.dot(oh, x, preferred_element_type=F32).astype(out_ref.dtype)
            return carry

        lax.fori_loop(1, 1 + _n_extra_windows(start, hi), extra, 0)


def _dispatch(base, x, slot_t, bsz, seq, cap, *, cw, out_dtype, name):
    t, width = x.shape
    ts = MOE_TILE
    nt = seq // ts
    grid_spec = pltpu.PrefetchScalarGridSpec(
        num_scalar_prefetch=1,
        grid=(bsz, width // cw, nt),
        in_specs=[
            pl.BlockSpec((ts, cw), lambda b, c, i, base: (b * nt + i, c)),
            pl.BlockSpec((N_EXPERTS, ts), lambda b, c, i, base: (0, b * nt + i)),
        ],
        out_specs=pl.BlockSpec((None, N_EXPERTS, cap, cw), lambda b, c, i, base: (b, 0, 0, c)),
    )
    return pl.pallas_call(
        functools.partial(_dispatch_body, nt=nt, cap=cap),
        grid_spec=grid_spec,
        out_shape=jax.ShapeDtypeStruct((bsz, N_EXPERTS, cap, width), out_dtype),
        compiler_params=_params("parallel", "parallel", "arbitrary"),
        name=name,
    )(base, x, slot_t)


def _expert_body(x_ref, wg_ref, wu_ref, wd_ref, gs_ref, y_ref, acc_ref):
    e = pl.program_id(0)
    f = pl.program_id(2)
    x = x_ref[...]
    hid = (jax.nn.silu(jnp.dot(x, wg_ref[...], preferred_element_type=F32))
           * jnp.dot(x, wu_ref[...], preferred_element_type=F32)).astype(BF16)
    part = jnp.dot(hid, wd_ref[...], preferred_element_type=F32)

    @pl.when(f == 0)
    def _():
        acc_ref[...] = part

    @pl.when(f > 0)
    def _():
        acc_ref[...] += part

    @pl.when(f == pl.num_programs(2) - 1)
    def _():
        pieces = gs_ref[...]
        lane = lax.broadcasted_iota(I32, pieces.shape, 1)
        mine = (lane >= 3 * e) & (lane < 3 * e + 3)
        gate = jnp.sum(jnp.where(mine, pieces, 0.0), axis=1, keepdims=True)
        y_ref[...] = (acc_ref[...] * gate).astype(y_ref.dtype)


def _expert_ffn(xg, w_gate, w_up, w_down, layer, gate_slots):
    bsz, ne, cap, dm = xg.shape
    ff = w_gate.shape[3]
    tf = 512
    return pl.pallas_call(
        _expert_body,
        grid=(ne, bsz, ff // tf),
        in_specs=[
            pl.BlockSpec((None, None, cap, dm), lambda e, b, f: (b, e, 0, 0)),
            pl.BlockSpec((None, None, dm, tf), lambda e, b, f: (layer, e, 0, f)),
            pl.BlockSpec((None, None, dm, tf), lambda e, b, f: (layer, e, 0, f)),
            pl.BlockSpec((None, None, tf, dm), lambda e, b, f: (layer, e, f, 0)),
            pl.BlockSpec((None, None, cap, 128), lambda e, b, f: (b, e, 0, 0)),
        ],
        out_specs=pl.BlockSpec((None, None, cap, dm), lambda e, b, f: (b, e, 0, 0)),
        out_shape=jax.ShapeDtypeStruct((bsz, ne, cap, dm), BF16),
        scratch_shapes=[pltpu.VMEM((cap, dm), F32)],
        compiler_params=_params("parallel", "parallel", "arbitrary"),
        name="expert_swiglu",
    )(xg, w_gate, w_up, w_down, gate_slots)


def _combine_body(base_ref, yg_hbm, h_ref, slot_ref, out_ref, wbuf, xbuf, wsem, xsem, *, nt, cap, n_steps):
    n = pl.program_id(0)
    ts = h_ref.shape[0]

    def window_copies(step, half):
        b = step // nt
        i = step % nt
        copies = []
        for e in range(N_EXPERTS):
            start, _ = _slot_window(base_ref, b, e, i, nt, cap)
            copies.append(pltpu.make_async_copy(yg_hbm.at[b, e, pl.ds(start, SLOT_WIN), :],
                                                wbuf.at[half, e], wsem.at[half]))
        return copies

    @pl.when(n == 0)
    def _():
        for c in window_copies(0, 0):
            c.start()

    @pl.when(n + 1 < n_steps)
    def _():
        for c in window_copies(n + 1, (n + 1) % 2):
            c.start()

    b = n // nt
    i = n % nt
    half = n % 2
    lane = lax.broadcasted_iota(I32, (ts, SLOT_WIN), 1)
    wins = [_slot_window(base_ref, b, e, i, nt, cap) for e in range(N_EXPERTS)]
    for c in window_copies(n, half):
        c.wait()
    acc = h_ref[...]
    for e0 in range(0, N_EXPERTS, MOE_STACK):
        onehot = jnp.concatenate(
            [jnp.where(slot_ref[:, e:e + 1] - wins[e][0] == lane, 1.0, 0.0).astype(BF16)
             for e in range(e0, e0 + MOE_STACK)], axis=1)
        rows = wbuf[half, e0:e0 + MOE_STACK].reshape(MOE_STACK * SLOT_WIN, wbuf.shape[-1])
        acc = acc + jnp.dot(onehot, rows, preferred_element_type=F32)
    out_ref[...] = acc

    for e in range(N_EXPERTS):
        start, hi = wins[e]

        def extra(k, carry, e=e, start=start):
            first, st = _extra_window(start, k, cap)
            copy = pltpu.make_async_copy(yg_hbm.at[b, e, pl.ds(st, SLOT_WIN), :], xbuf, xsem)
            copy.start()
            scol = slot_ref[:, e:e + 1]
            scol = jnp.where(scol >= first, scol, -1)
            oh = jnp.where(scol - st == lane, 1.0, 0.0).astype(BF16)
            copy.wait()
            out_ref[...] += jnp.dot(oh, xbuf[...], preferred_element_type=F32)
            return carry

        lax.fori_loop(1, 1 + _n_extra_windows(start, hi), extra, 0)


def _combine(base, yg, h, slot_tok, bsz, seq):
    t, dm = h.shape
    cap = yg.shape[2]
    ts = MOE_TILE
    nt = seq // ts
    n_steps = bsz * nt
    grid_spec = pltpu.PrefetchScalarGridSpec(
        num_scalar_prefetch=1,
        grid=(n_steps,),
        in_specs=[
            pl.BlockSpec(memory_space=pl.ANY),
            pl.BlockSpec((ts, dm), lambda n, base: (n, 0)),
            pl.BlockSpec((ts, N_EXPERTS), lambda n, base: (n, 0)),
        ],
        out_specs=pl.BlockSpec((ts, dm), lambda n, base: (n, 0)),
        scratch_shapes=[pltpu.VMEM((2, N_EXPERTS, SLOT_WIN, dm), BF16),
                        pltpu.VMEM((SLOT_WIN, dm), BF16),
                        pltpu.SemaphoreType.DMA((2,)),
                        pltpu.SemaphoreType.DMA(())],
    )
    return pl.pallas_call(
        functools.partial(_combine_body, nt=nt, cap=cap, n_steps=n_steps),
        grid_spec=grid_spec,
        out_shape=jax.ShapeDtypeStruct((t, dm), F32),
        compiler_params=_params("arbitrary"),
        name="moe_combine",
    )(base, yg, h, slot_tok)


def _moe(h, gain, w_router_t, w_gate, w_up, w_down, layer, bsz, seq):
    cap = CAPACITY_FACTOR * seq // N_EXPERTS
    xn, logits_t = _router_logits(h, gain, w_router_t)
    gate_t, pos_t, sel_t = _select_tokens(logits_t, bsz, seq)

    slot_t = jnp.where(sel_t > 0, pos_t, -1)
    ts = MOE_TILE
    nt = seq // ts
    starts = pos_t.reshape(N_EXPERTS, bsz, nt, ts)[..., 0]
    base = jnp.concatenate([jnp.transpose(starts, (1, 0, 2)),
                            jnp.full((bsz, N_EXPERTS, 1), cap, I32)], axis=-1).reshape(-1)
    gate_tok = gate_t.T
    g_hi = gate_tok.astype(BF16)
    r1 = gate_tok - g_hi.astype(F32)
    g_mid = r1.astype(BF16)
    g_lo = (r1 - g_mid.astype(F32)).astype(BF16)
    pieces = jnp.stack([g_hi, g_mid, g_lo], axis=-1).reshape(-1, 3 * N_EXPERTS)
    pieces = jnp.pad(pieces, ((0, 0), (0, 128 - 3 * N_EXPERTS)))

    xg = _dispatch(base, xn, slot_t, bsz, seq, cap, cw=512, out_dtype=BF16, name="moe_dispatch")
    gate_slots = _dispatch(base, pieces, slot_t, bsz, seq, cap, cw=128, out_dtype=F32, name="moe_gate_dispatch")
    yg = _expert_ffn(xg, w_gate, w_up, w_down, layer, gate_slots)
    return _combine(base, yg, h, slot_t.T, bsz, seq)


def _ple_body(h_ref, g_ref, wg_ref, p_ref, wp_ref, hres_ref, o_ref, xn_ref):
    @pl.when(pl.program_id(1) == 0)
    def _():
        xn_ref[...] = _rms(h_ref[...], g_ref[...]).astype(BF16)

    gate = jax.nn.sigmoid(jnp.dot(xn_ref[...], wg_ref[...], preferred_element_type=F32))
    ple = jnp.dot(p_ref[...].astype(BF16), wp_ref[...], preferred_element_type=F32)
    o_ref[...] = hres_ref[...] + gate * ple


def _ple(h, gain, w_gate, p, w_proj, layer):
    t, dm = h.shape
    tm, tn = 1024, 512
    pd = p.shape[2]
    return pl.pallas_call(
        _ple_body,
        grid=(t // tm, dm // tn),
        in_specs=[
            pl.BlockSpec((tm, dm), lambda i, j: (i, 0)),
            pl.BlockSpec((1, dm), lambda i, j: (0, 0)),
            pl.BlockSpec((None, dm, tn), lambda i, j: (layer, 0, j)),
            pl.BlockSpec((None, tm, pd), lambda i, j: (layer, i, 0)),
            pl.BlockSpec((None, pd, tn), lambda i, j: (layer, 0, j)),
            pl.BlockSpec((tm, tn), lambda i, j: (i, j)),
        ],
        out_specs=pl.BlockSpec((tm, tn), lambda i, j: (i, j)),
        out_shape=jax.ShapeDtypeStruct((t, dm), F32),
        scratch_shapes=[pltpu.VMEM((tm, dm), BF16)],
        compiler_params=_params("parallel", "arbitrary"),
        name="ple_gate",
    )(h, gain, w_gate, p, w_proj, h)


def _rotary_lane_order(a):
    blocks = HEAD_DIM // ROT_HALF
    order = list(range(blocks))
    order[1], order[blocks // 2] = order[blocks // 2], order[1]
    shaped = a.reshape(a.shape[:-1] + (a.shape[-1] // HEAD_DIM, blocks, ROT_HALF))
    return shaped[..., jnp.array(order), :].reshape(a.shape)


def _rope_tables(positions):
    inv_freq = jnp.power(ROPE_THETA, -jnp.arange(ROT_HALF, dtype=F32) * 2.0 / (2 * ROT_HALF))
    ang = positions.astype(F32)[..., None] * inv_freq
    cos, sin = jnp.cos(ang), jnp.sin(ang)
    gap = HEAD_DIM // 2 - ROT_HALF
    ones = jnp.ones(ang.shape[:-1] + (gap,), F32)
    zeros = jnp.zeros(ang.shape[:-1] + (gap,), F32)
    cos_t = jnp.concatenate([cos, ones, cos, ones], axis=-1)
    sin_t = jnp.concatenate([-sin, zeros, sin, zeros], axis=-1)
    return cos_t.reshape(-1, HEAD_DIM), sin_t.reshape(-1, HEAD_DIM)


def kernel(x, p, positions, norm_mix, w_in, q_norm, k_norm, w_attn_br, ssm_a_re, ssm_a_im, ssm_log_dt,
           ssm_b_re, ssm_b_im, ssm_c_re, ssm_c_im, ssm_d, w_ssm_br, w_out, norm_ffn, w_router,
           w_exp_gate, w_exp_up, w_exp_down, norm_ple, w_ple_gate, w_ple_proj):
    bsz, seq, dm = x.shape
    depth = w_in.shape[0]
    t = bsz * seq
    n_attn = 3 * HEADS_PER_GROUP * len(DILATIONS) * HEAD_DIM
    ssm_width = ssm_d.shape[1]
    nc = seq // SSM_CHUNK
    n_steps = int(math.log2(nc))
    assert seq % PERM_TILE == 0 and nc == 1 << n_steps

    cos_t, sin_t = _rope_tables(positions)
    w_in_b, w_attn_b, w_ssm_b, w_out_b = (w.astype(BF16) for w in (w_in, w_attn_br, w_ssm_br, w_out))
    w_eg_b, w_eu_b, w_ed_b = (w.astype(BF16) for w in (w_exp_gate, w_exp_up, w_exp_down))
    w_pg_b, w_pp_b = w_ple_gate.astype(BF16), w_ple_proj.astype(BF16)
    w_qk_b = _rotary_lane_order(w_in_b[:, :, :2 * n_attn // 3])
    comp, tabs = _ssm_fold(ssm_a_re, ssm_a_im, ssm_log_dt, ssm_b_re, ssm_b_im, ssm_c_re, ssm_c_im, n_steps)
    w_slab = _ssm_expand(comp)
    p_rows = p.reshape(depth, t, p.shape[-1])
    h = x.reshape(t, dm)

    for l in range(depth):
        gain = norm_mix[l][None]
        qk_gain = _rotary_lane_order(jnp.stack([q_norm[l], k_norm[l]]))[:, None, :]

        u, xn = _norm_matmul(h, gain, w_in_b, l, n_attn, ssm_width, out_dtype=F32, name="u_proj")
        gates = _sigmoid_matmul(xn, w_in_b, l, n_attn + ssm_width, 2 * dm, name="gate_proj")
        outs, lses = zip(*[_attention(_qkv_proj(xn, w_qk_b, w_in_b, l, gi, qk_gain, cos_t, sin_t), gi, bsz, seq)
                           for gi in range(len(DILATIONS))])
        gated_attn = _attn_merge_proj(outs, lses, w_attn_b, l, gates)

        y = _ssm_scan(u, w_slab, tabs, l, bsz, seq)
        merged = _ssm_glu_merge(y, u, ssm_d[l][None], w_ssm_b, l, gated_attn, gates)
        h = _resid_matmul(merged, w_out_b, l, h)

        h = _moe(h, norm_ffn[l][None], w_router[l].T, w_eg_b, w_eu_b, w_ed_b, l, bsz, seq)
        h = _ple(h, norm_ple[l][None], w_pg_b, p_rows, w_pp_b, l)
    return h.reshape(bsz, seq, dm)
```

```python
import functools
import math

import jax
import jax.numpy as jnp
from jax import lax
from jax.experimental import pallas as pl
from jax.experimental.pallas import tpu as pltpu

F32 = jnp.float32
BF16 = jnp.bfloat16
I32 = jnp.int32

NORM_EPS = 1e-6
MASK_VALUE = -1e30
ROPE_THETA = 500000.0

LANES = 128
HEAD_DIM = 128
HEADS_PER_GROUP = 4
GROUP_WIDTH = HEADS_PER_GROUP * HEAD_DIM
DILATIONS = (1, 4, 16)
N_SIDE = 64
ROT_HALF = 16

SSM_GROUP = 16
SSM_STATE = 64
SSM_CHUNK = 16

N_EXPERTS = 16
CAPACITY_FACTOR = 2
SLOT_WIN = 128
SLOT_ALIGN = 16
MOE_TILE = 512
MOE_STACK = 4

PERM_TILE = 1024
BF16_LHS_ROWS = 2048
VMEM_LIMIT = 56 * 1024 * 1024


def _params(*sem):
    return pltpu.CompilerParams(dimension_semantics=sem, vmem_limit_bytes=VMEM_LIMIT)


def _rms(x, gain):
    var = jnp.mean(x * x, axis=-1, keepdims=True)
    return x * lax.rsqrt(var + NORM_EPS) * gain


def _qkv_body(xn_ref, wq_ref, wk_ref, wv_ref, qkg_ref, cos_ref, sin_ref, o_ref, slab_ref, *, d):
    tm = xn_ref.shape[0]
    n = tm // d
    xn = xn_ref[...]
    cos = cos_ref[...]
    sin = sin_ref[...]
    for part, w_ref in enumerate((wq_ref, wk_ref, wv_ref)):
        acc = jnp.dot(xn, w_ref[...], preferred_element_type=F32)
        for hs in range(HEADS_PER_GROUP):
            a = acc[:, hs * HEAD_DIM:(hs + 1) * HEAD_DIM]
            if part < 2:
                a = _rms(a, qkg_ref[part])
                a = a * cos + pltpu.roll(a, HEAD_DIM // 2, 1) * sin
            cols = slice(part * GROUP_WIDTH + hs * HEAD_DIM, part * GROUP_WIDTH + (hs + 1) * HEAD_DIM)
            if d == 1:
                o_ref[:, cols] = a.astype(BF16)
            else:
                slab_ref[part, hs] = a
                for r in range(d):
                    o_ref[r * n:(r + 1) * n, cols] = slab_ref[part, hs, pl.ds(r, n, stride=d), :].astype(BF16)


def _qkv_proj(xn, w_qk, w_in, layer, gi, qk_gain, cos, sin):
    t, dm = xn.shape
    tm, tn = PERM_TILE, GROUP_WIDTH
    ng = len(DILATIONS)

    def weight(part):
        return pl.BlockSpec((None, dm, tn), lambda i: (layer, 0, part * ng + gi))

    return pl.pallas_call(
        functools.partial(_qkv_body, d=DILATIONS[gi]),
        grid=(t // tm,),
        in_specs=[
            pl.BlockSpec((tm, dm), lambda i: (i, 0)),
            weight(0), weight(1), weight(2),
            pl.BlockSpec((2, 1, HEAD_DIM), lambda i: (0, 0, 0)),
            pl.BlockSpec((tm, HEAD_DIM), lambda i: (i, 0)),
            pl.BlockSpec((tm, HEAD_DIM), lambda i: (i, 0)),
        ],
        out_specs=pl.BlockSpec((tm, 3 * tn), lambda i: (i, 0)),
        out_shape=jax.ShapeDtypeStruct((t, 3 * tn), BF16),
        scratch_shapes=[pltpu.VMEM((3, HEADS_PER_GROUP, tm, HEAD_DIM), F32)],
        compiler_params=_params("parallel"),
        name=f"qkv_proj_d{DILATIONS[gi]}",
    )(xn, w_qk, w_qk, w_in, qk_gain, cos, sin)


def _norm_mm_body(h_ref, g_ref, w_ref, o_ref, xn_out_ref, xn_ref):
    @pl.when(pl.program_id(1) == 0)
    def _():
        xn = _rms(h_ref[...], g_ref[...]).astype(BF16)
        xn_ref[...] = xn
        xn_out_ref[...] = xn

    o_ref[...] = jnp.dot(xn_ref[...], w_ref[...], preferred_element_type=F32).astype(o_ref.dtype)


def _norm_matmul(h, gain, w, layer, col0, ncols, *, out_dtype, tn=512, name):
    t, dm = h.shape
    tm = 1024
    nj = ncols // tn
    blk0 = col0 // tn
    return pl.pallas_call(
        _norm_mm_body,
        grid=(t // tm, nj),
        in_specs=[
            pl.BlockSpec((tm, dm), lambda i, j: (i, 0)),
            pl.BlockSpec((1, dm), lambda i, j: (0, 0)),
            pl.BlockSpec((None, dm, tn), lambda i, j: (layer, 0, blk0 + j)),
        ],
        out_specs=[pl.BlockSpec((tm, tn), lambda i, j: (i, j)),
                   pl.BlockSpec((tm, dm), lambda i, j: (i, 0))],
        out_shape=[jax.ShapeDtypeStruct((t, nj * tn), out_dtype), jax.ShapeDtypeStruct((t, dm), BF16)],
        scratch_shapes=[pltpu.VMEM((tm, dm), BF16)],
        compiler_params=_params("parallel", "arbitrary"),
        name=name,
    )(h, gain, w)


def _act_mm_body(x_ref, w_ref, o_ref):
    o_ref[...] = jax.nn.sigmoid(jnp.dot(x_ref[...], w_ref[...], preferred_element_type=F32)).astype(o_ref.dtype)


def _sigmoid_matmul(x, w, layer, col0, ncols, *, tn=512, name):
    t, dm = x.shape
    tm = BF16_LHS_ROWS
    nj = ncols // tn
    blk0 = col0 // tn
    return pl.pallas_call(
        _act_mm_body,
        grid=(t // tm, nj),
        in_specs=[
            pl.BlockSpec((tm, dm), lambda i, j: (i, 0)),
            pl.BlockSpec((None, dm, tn), lambda i, j: (layer, 0, blk0 + j)),
        ],
        out_specs=pl.BlockSpec((tm, tn), lambda i, j: (i, j)),
        out_shape=jax.ShapeDtypeStruct((t, nj * tn), BF16),
        compiler_params=_params("parallel", "arbitrary"),
        name=name,
    )(x, w)


def _attn_body(q_ref, kp_ref, km_ref, kn_ref, vp_ref, vm_ref, vn_ref, o_ref, l_ref,
               kw_ref, vw_ref, os_ref, ls_ref, *, tq, sub_len):
    c = pl.program_id(2)
    kw_ref[0:N_SIDE] = kp_ref[...]
    kw_ref[N_SIDE:N_SIDE + tq] = km_ref[...].reshape(tq, GROUP_WIDTH)
    kw_ref[N_SIDE + tq:] = kn_ref[...]
    vw_ref[0:N_SIDE] = vp_ref[...]
    vw_ref[N_SIDE:N_SIDE + tq] = vm_ref[...].reshape(tq, GROUP_WIDTH)
    vw_ref[N_SIDE + tq:] = vn_ref[...]
    q = q_ref[...].reshape(tq, GROUP_WIDTH)

    sb = min(128, tq)
    nk = sb + 2 * N_SIDE
    scale = HEAD_DIM ** -0.5
    row = lax.broadcasted_iota(I32, (sb, nk), 0)
    col = lax.broadcasted_iota(I32, (sb, nk), 1)
    band = jnp.abs(col - row - N_SIDE) <= N_SIDE
    for i in range(tq // sb):
        kpos = c * tq + (i * sb - N_SIDE) + col
        valid = band & (kpos >= 0) & (kpos < sub_len)
        for hs in range(HEADS_PER_GROUP):
            lanes = slice(hs * HEAD_DIM, (hs + 1) * HEAD_DIM)
            qh = q[i * sb:(i + 1) * sb, lanes]
            kh = kw_ref[i * sb:i * sb + nk, lanes]
            vh = vw_ref[i * sb:i * sb + nk, lanes]
            s = lax.dot_general(qh, kh, (((1,), (1,)), ((), ())), preferred_element_type=F32) * scale
            s = jnp.where(valid, s, MASK_VALUE)
            m = jnp.max(s, axis=-1, keepdims=True)
            e = jnp.exp(s - m)
            den = jnp.sum(e, axis=-1, keepdims=True)
            o = jnp.dot((e / den).astype(BF16), vh, preferred_element_type=F32)
            os_ref[i * sb:(i + 1) * sb, lanes] = o
            ls_ref[i * sb:(i + 1) * sb, lanes] = jnp.broadcast_to(m + jnp.log(den), (sb, HEAD_DIM))
    o_ref[...] = os_ref[...].reshape(o_ref.shape)
    l_ref[...] = ls_ref[...].reshape(l_ref.shape)


def _attention(qkv, gi, bsz, seq):
    d = DILATIONS[gi]
    t = bsz * seq
    sub_len = seq // d
    nbt = PERM_TILE // (N_SIDE * d)
    ntile = seq // PERM_TILE
    tq = min(512, sub_len)
    nbq = tq // N_SIDE
    nblk = sub_len // N_SIDE
    view = (bsz, ntile, d, nbt, N_SIDE, qkv.shape[1])
    oview = (bsz, ntile, d, nbt, N_SIDE, GROUP_WIDTH)
    gw = GROUP_WIDTH

    if nbt >= nbq:
        per = nbt // nbq
        main_shape = (None, None, None, nbq, N_SIDE, gw)

        def main_idx(col):
            return lambda b, r, c: (b, c // per, r, c % per, 0, col)
    else:
        main_shape = (None, nbq // nbt, None, nbt, N_SIDE, gw)

        def main_idx(col):
            return lambda b, r, c: (b, c, r, 0, 0, col)

    halo_shape = (None, None, None, None, N_SIDE, gw)

    def prev_idx(col):
        def f(b, r, c):
            n = jnp.maximum(c * nbq - 1, 0)
            return (b, n // nbt, r, n % nbt, 0, col)
        return f

    def next_idx(col):
        def f(b, r, c):
            n = jnp.minimum((c + 1) * nbq, nblk - 1)
            return (b, n // nbt, r, n % nbt, 0, col)
        return f

    qc, kc, vc = 0, 1, 2
    x = qkv.reshape(view)
    o, l = pl.pallas_call(
        functools.partial(_attn_body, tq=tq, sub_len=sub_len),
        grid=(bsz, d, sub_len // tq),
        in_specs=[
            pl.BlockSpec(main_shape, main_idx(qc)),
            pl.BlockSpec(halo_shape, prev_idx(kc)),
            pl.BlockSpec(main_shape, main_idx(kc)),
            pl.BlockSpec(halo_shape, next_idx(kc)),
            pl.BlockSpec(halo_shape, prev_idx(vc)),
            pl.BlockSpec(main_shape, main_idx(vc)),
            pl.BlockSpec(halo_shape, next_idx(vc)),
        ],
        out_specs=[pl.BlockSpec(main_shape, main_idx(0)), pl.BlockSpec(main_shape, main_idx(0))],
        out_shape=[jax.ShapeDtypeStruct(oview, F32), jax.ShapeDtypeStruct(oview, F32)],
        scratch_shapes=[pltpu.VMEM((tq + 2 * N_SIDE, gw), BF16), pltpu.VMEM((tq + 2 * N_SIDE, gw), BF16),
                        pltpu.VMEM((tq, gw), F32), pltpu.VMEM((tq, gw), F32)],
        compiler_params=_params("parallel", "parallel", "arbitrary"),
        name=f"dilated_attn_d{d}",
    )(x, x, x, x, x, x, x)
    return o.reshape(t, gw), l.reshape(t, gw)


def _attn_merge_body(o0, l0, o1, l1, o2, l2, w_ref, g_ref, out_ref, comb_ref, so_ref, sl_ref):
    tm = out_ref.shape[0]

    @pl.when(pl.program_id(1) == 0)
    def _():
        for gi, (o_ref, l_ref) in enumerate(((o0, l0), (o1, l1), (o2, l2))):
            d = DILATIONS[gi]
            n = tm // d
            for hs in range(HEADS_PER_GROUP):
                lanes = slice(hs * HEAD_DIM, (hs + 1) * HEAD_DIM)
                for r in range(d):
                    ov = o_ref[r * n:(r + 1) * n, lanes]
                    lv = l_ref[r * n:(r + 1) * n, lanes]
                    if d == 1:
                        so_ref[gi, hs] = ov
                        sl_ref[gi, hs] = lv
                    else:
                        so_ref[gi, hs, pl.ds(r, n, stride=d), :] = ov
                        sl_ref[gi, hs, pl.ds(r, n, stride=d), :] = lv
        for hs in range(HEADS_PER_GROUP):
            ls = [sl_ref[gi, hs] for gi in range(3)]
            mx = jnp.maximum(jnp.maximum(ls[0], ls[1]), ls[2])
            ws = [jnp.exp(l - mx) for l in ls]
            num = ws[0] * so_ref[0, hs] + ws[1] * so_ref[1, hs] + ws[2] * so_ref[2, hs]
            comb = num / (ws[0] + ws[1] + ws[2])
            comb_ref[:, hs * HEAD_DIM:(hs + 1) * HEAD_DIM] = comb.astype(BF16)

    acc = jnp.dot(comb_ref[...], w_ref[...], preferred_element_type=F32)
    out_ref[...] = (acc * g_ref[...].astype(F32)).astype(out_ref.dtype)


def _attn_merge_proj(outs, lses, w_attn, layer, gates):
    t = outs[0].shape[0]
    dm = w_attn.shape[2]
    tm, tn = PERM_TILE, 512
    row = pl.BlockSpec((tm, GROUP_WIDTH), lambda i, j: (i, 0))
    return pl.pallas_call(
        _attn_merge_body,
        grid=(t // tm, dm // tn),
        in_specs=[row, row, row, row, row, row,
                  pl.BlockSpec((None, GROUP_WIDTH, tn), lambda i, j: (layer, 0, j)),
                  pl.BlockSpec((tm, tn), lambda i, j: (i, j))],
        out_specs=pl.BlockSpec((tm, tn), lambda i, j: (i, j)),
        out_shape=jax.ShapeDtypeStruct((t, dm), BF16),
        scratch_shapes=[pltpu.VMEM((tm, GROUP_WIDTH), BF16),
                        pltpu.VMEM((3, HEADS_PER_GROUP, tm, HEAD_DIM), F32),
                        pltpu.VMEM((3, HEADS_PER_GROUP, tm, HEAD_DIM), F32)],
        compiler_params=_params("parallel", "arbitrary"),
        name="attn_merge_proj",
    )(outs[0], lses[0], outs[1], lses[1], outs[2], lses[2], w_attn, gates)


def _ssm_fold_body(sp_re, sp_im, si_re, si_im, sot_re, sot_im, bb_re, bb_im, c_re, c_im, ct_re, ct_im,
                   rep_ref, tile_ref, o_ref):
    lc, hg, p = SSM_CHUNK, SSM_GROUP, SSM_STATE
    gps = LANES // hg
    kw = lc * hg
    hi_prec = lax.Precision.HIGHEST
    nt = (((1,), (1,)), ((), ()))
    for g in range(gps):
        strip = jnp.zeros((hg, 2 * kw), F32)
        injections = []
        for d in range(2):
            br, bi = bb_re[d, g], bb_im[d, g]
            cr, ci = c_re[d, g], c_im[d, g]
            pr, pi = sp_re[d, g][:, None, :], sp_im[d, g][:, None, :]
            y_re = (pr * cr[None] - pi * ci[None]).reshape(2 * kw, p)
            y_im = (pr * ci[None] + pi * cr[None]).reshape(2 * kw, p)
            strip = strip + lax.dot_general(jnp.concatenate([br, -bi], axis=1), jnp.concatenate([y_re, y_im], axis=1),
                                            nt, precision=hi_prec, preferred_element_type=F32)
            qr, qi = si_re[d, g][:, None, :], si_im[d, g][:, None, :]
            injections += [(qr * br[None] - qi * bi[None]).reshape(kw, p),
                           (qr * bi[None] + qi * br[None]).reshape(kw, p)]
            ar = jnp.dot(sot_re[d, g], rep_ref[...], precision=hi_prec, preferred_element_type=F32)
            ai = jnp.dot(sot_im[d, g], rep_ref[...], precision=hi_prec, preferred_element_type=F32)
            xr = jnp.dot(ct_re[d, g], tile_ref[...], precision=hi_prec, preferred_element_type=F32)
            xi = jnp.dot(ct_im[d, g], tile_ref[...], precision=hi_prec, preferred_element_type=F32)
            row0 = ((d * (gps // 2) + g // 2) * 2) * LANES + (g % 2) * p
            o_ref[2, row0:row0 + p, :] = (ar * xr - ai * xi).astype(o_ref.dtype)
            o_ref[2, row0 + LANES:row0 + LANES + p, :] = (-(ar * xi + ai * xr)).astype(o_ref.dtype)
        state_in = jnp.concatenate(injections, axis=1)
        for s in range(lc):
            rows = slice(s * LANES + g * hg, s * LANES + (g + 1) * hg)
            o_ref[0, rows, :] = strip[:, (lc - 1 - s) * hg:(lc - 1 - s) * hg + kw].astype(o_ref.dtype)
            o_ref[1, rows, :] = state_in[s * hg:(s + 1) * hg, :].astype(o_ref.dtype)


def _ssm_fold(a_re, a_im, log_dt, b_re, b_im, c_re, c_im, n_steps):
    f32 = lambda v: v.astype(F32)
    a_re, a_im, b_re, b_im, c_re, c_im = map(f32, (a_re, a_im, b_re, b_im, c_re, c_im))
    depth, _, g, p = a_re.shape
    lc, hg = SSM_CHUNK, SSM_GROUP
    gps = LANES // hg
    ns, npair = g // gps, gps // 2
    kw = lc * hg
    side = lc * LANES
    dt = jnp.exp(f32(log_dt))[..., None]

    def lam_pow(n):
        n = n.astype(F32)[:, None]
        mag = jnp.exp((a_re * dt)[..., None, :] * n)
        ang = (a_im * dt)[..., None, :] * n
        return mag * jnp.cos(ang), mag * jnp.sin(ang)

    pw_re, pw_im = lam_pow(jnp.arange(lc + 1))
    lb_re, lb_im = pw_re[..., 1, :], pw_im[..., 1, :]
    den = a_re * a_re + a_im * a_im
    nr, ni = lb_re - 1.0, lb_im
    coef_re = ((nr * a_re + ni * a_im) / den)[..., None, :]
    coef_im = ((ni * a_re - nr * a_im) / den)[..., None, :]
    bt_re, bt_im = jnp.swapaxes(b_re, -1, -2), jnp.swapaxes(b_im, -1, -2)
    bb_re = coef_re * bt_re - coef_im * bt_im
    bb_im = coef_re * bt_im + coef_im * bt_re

    def per_dir(pw, fwd, bwd):
        return jnp.stack([pw[:, 0][..., fwd, :], pw[:, 1][..., bwd, :]], axis=1)

    steps = jnp.arange(lc)
    slot = jnp.arange(2 * lc)
    f_on = ((slot >= lc - 1) & (slot < 2 * lc - 1)).astype(F32)[:, None]
    b_on = (slot <= lc - 1).astype(F32)[:, None]
    f_idx = jnp.clip(slot - (lc - 1), 0, lc)
    b_idx = jnp.clip(lc - 1 - slot, 0, lc)
    strip_pw = lambda pw: jnp.stack([pw[:, 0][..., f_idx, :] * f_on, pw[:, 1][..., b_idx, :] * b_on], axis=1)
    sp_re, sp_im = strip_pw(pw_re), strip_pw(pw_im)
    si_re, si_im = per_dir(pw_re, lc - 1 - steps, steps), per_dir(pw_im, lc - 1 - steps, steps)
    so_re, so_im = per_dir(pw_re, steps + 1, lc - steps), per_dir(pw_im, steps + 1, lc - steps)
    sot_re, sot_im = jnp.swapaxes(so_re, -1, -2), jnp.swapaxes(so_im, -1, -2)
    ct_re, ct_im = jnp.swapaxes(c_re, -1, -2), jnp.swapaxes(c_im, -1, -2)
    col = jnp.arange(kw)[None, :]
    rep = (jnp.arange(lc)[:, None] == col // hg).astype(F32)
    tile = (jnp.arange(hg)[:, None] == col % hg).astype(F32)

    def rows_spec(rows):
        return pl.BlockSpec((None, 2, gps, rows, p), lambda l, s: (l, 0, s, 0, 0))

    def cols_spec(cols):
        return pl.BlockSpec((None, 2, gps, p, cols), lambda l, s: (l, 0, s, 0, 0))

    const = pl.BlockSpec((lc, kw), lambda l, s: (0, 0))
    comp = pl.pallas_call(
        _ssm_fold_body,
        grid=(depth, ns),
        in_specs=[rows_spec(2 * lc), rows_spec(2 * lc), rows_spec(lc), rows_spec(lc), cols_spec(lc), cols_spec(lc),
                  rows_spec(hg), rows_spec(hg), rows_spec(hg), rows_spec(hg), cols_spec(hg), cols_spec(hg),
                  const, const],
        out_specs=pl.BlockSpec((None, None, 3, side, kw), lambda l, s: (l, s, 0, 0, 0)),
        out_shape=jax.ShapeDtypeStruct((depth, ns, 3, side, kw), BF16),
        compiler_params=_params("parallel", "parallel"),
        name="s5_fold_params",
    )(sp_re, sp_im, si_re, si_im, sot_re, sot_im, bb_re, bb_im, c_re, c_im, ct_re, ct_im, rep, tile)

    sc_re, sc_im = lam_pow(lc * (2 ** jnp.arange(n_steps)))
    tab = jnp.stack([sc_re, sc_im], axis=-2).reshape(depth, 2, ns, npair, 2, n_steps, 2, p)
    tab = jnp.transpose(tab, (0, 2, 1, 3, 5, 6, 4, 7)).reshape(depth, ns, 2 * npair * n_steps * 2, 2 * p)
    return comp, tab


def _ssm_expand_constants():
    lc, hg, p = SSM_CHUNK, SSM_GROUP, SSM_STATE
    r = jnp.arange(lc * hg)[:, None]
    c = jnp.arange(lc * LANES)[None, :]
    e_time = (r // hg == c // LANES) & (r % hg == c % hg)
    e_state = (r // (2 * p) == c // (lc * LANES // 2)) & ((r // p) % 2 == (c // LANES) % 2) & (r % p == c % p)
    rr = jnp.arange(lc * LANES)[:, None]
    lane_group = lambda i: (i % LANES) // hg
    state_group = lambda i: 2 * ((i // (2 * LANES)) % (LANES // hg // 2)) + (i % LANES) // p
    masks = [lane_group(rr) == lane_group(c), lane_group(rr) == state_group(c), state_group(rr) == lane_group(c)]
    return (jnp.stack([e_time, e_state, e_time]).astype(BF16), jnp.stack(masks).astype(BF16))


def _ssm_expand_body(c_ref, e_ref, m_ref, o_ref):
    rows = 512
    for r in range(0, o_ref.shape[0], rows):
        acc = jnp.dot(c_ref[r:r + rows], e_ref[...], preferred_element_type=F32)
        o_ref[r:r + rows] = jnp.where(m_ref[r:r + rows] > 0, acc, 0.0).astype(o_ref.dtype)


def _ssm_expand(comp):
    depth, ns, three, side, kw = comp.shape
    e, mask = _ssm_expand_constants()
    return pl.pallas_call(
        _ssm_expand_body,
        grid=(three, depth * ns),
        in_specs=[
            pl.BlockSpec((None, None, None, side, kw), lambda m, n: (n // ns, n % ns, m, 0, 0)),
            pl.BlockSpec((None, kw, side), lambda m, n: (m, 0, 0)),
            pl.BlockSpec((None, side, side), lambda m, n: (m, 0, 0)),
        ],
        out_specs=pl.BlockSpec((None, None, None, side, side), lambda m, n: (n // ns, n % ns, m, 0, 0)),
        out_shape=jax.ShapeDtypeStruct((depth, ns, three, side, side), BF16),
        compiler_params=_params("arbitrary", "arbitrary"),
        name="s5_expand_weights",
    )(comp, e, mask)


def _ssm_body(u_ref, toe_ref, sin_ref, sout_ref, tab_ref, y_ref, lhs_ref, inj_ref, st_ref, intra_ref, *, n_steps):
    seq = u_ref.shape[0]
    lc = SSM_CHUNK
    nc = seq // lc
    n_slabs = inj_ref.shape[0]
    per_dir = n_slabs // 4
    row = lax.broadcasted_iota(I32, (nc, LANES), 0)

    for s in range(lc):
        lhs_ref[:, s * LANES:(s + 1) * LANES] = u_ref[pl.ds(s, nc, stride=lc), :].astype(BF16)
    lhs = lhs_ref[...]

    for j in range(0, n_slabs, 2):
        r = jnp.dot(lhs, sin_ref[:, j * LANES:(j + 2) * LANES], preferred_element_type=F32)
        inj_ref[j] = r[:, :LANES]
        inj_ref[j + 1] = r[:, LANES:]

    def shifted(x, sh, direction):
        if sh % 8 == 0:
            zeros = jnp.zeros((sh, LANES), x.dtype)
            if direction == 0:
                return jnp.concatenate([zeros, x[:nc - sh]], axis=0)
            return jnp.concatenate([x[sh:], zeros], axis=0)
        if direction == 0:
            return jnp.where(row >= sh, pltpu.roll(x, sh, 0), 0.0)
        return jnp.where(row < nc - sh, pltpu.roll(x, nc - sh, 0), 0.0)

    for jq in range(2 * per_dir):
        direction = jq // per_dir
        xr = inj_ref[2 * jq]
        xi = inj_ref[2 * jq + 1]
        for k in range(n_steps):
            base = (jq * n_steps + k) * 2
            ar = tab_ref[base:base + 1, :]
            ai = tab_ref[base + 1:base + 2, :]
            sr = shifted(xr, 1 << k, direction)
            si = shifted(xi, 1 << k, direction)
            xr, xi = xr + ar * sr - ai * si, xi + ar * si + ai * sr
        st_ref[2 * jq] = shifted(xr, 1, direction).astype(BF16)
        st_ref[2 * jq + 1] = shifted(xi, 1, direction).astype(BF16)
        cols = slice(2 * jq * LANES, (2 * jq + 2) * LANES)
        intra_ref[jq] = jnp.dot(lhs, toe_ref[:, cols], preferred_element_type=F32)

    states = jnp.concatenate([st_ref[j] for j in range(n_slabs)], axis=1)
    for t in range(0, lc, 2):
        cols = slice(t * LANES, (t + 2) * LANES)
        r = intra_ref[t // 2] + jnp.dot(states, sout_ref[:, cols], preferred_element_type=F32)
        y_ref[pl.ds(t, nc, stride=lc), :] = r[:, :LANES]
        y_ref[pl.ds(t + 1, nc, stride=lc), :] = r[:, LANES:]


def _ssm_scan(u, w_slab, tab, layer, bsz, seq):
    t, width = u.shape
    ns = width // LANES
    nc = seq // SSM_CHUNK
    n_steps = int(math.log2(nc))
    side = SSM_CHUNK * LANES
    n_slabs = side // LANES
    once = pl.Buffered(1)

    def weight(m):
        return pl.BlockSpec((None, None, None, side, side), lambda g, b: (layer, g, m, 0, 0), pipeline_mode=once)

    return pl.pallas_call(
        functools.partial(_ssm_body, n_steps=n_steps),
        grid=(ns, bsz),
        in_specs=[
            pl.BlockSpec((seq, LANES), lambda g, b: (b, g)),
            weight(0), weight(1), weight(2),
            pl.BlockSpec((None, None, tab.shape[-2], LANES), lambda g, b: (layer, g, 0, 0)),
        ],
        out_specs=pl.BlockSpec((seq, LANES), lambda g, b: (b, g)),
        out_shape=jax.ShapeDtypeStruct((t, width), F32),
        scratch_shapes=[pltpu.VMEM((nc, side), BF16),
                        pltpu.VMEM((n_slabs, nc, LANES), F32),
                        pltpu.VMEM((n_slabs, nc, LANES), BF16),
                        pltpu.VMEM((n_slabs // 2, nc, 2 * LANES), F32)],
        compiler_params=_params("arbitrary", "arbitrary"),
        name="s5_chunk_scan",
    )(u, w_slab, w_slab, w_slab, tab)


def _ssm_glu_body(y_ref, u_ref, d_ref, wa_ref, wb_ref, ga_ref, gs_ref, o_ref, act_ref):
    @pl.when(pl.program_id(1) == 0)
    def _():
        act_ref[...] = jax.nn.gelu(y_ref[...] + d_ref[...] * u_ref[...]).astype(BF16)

    act = act_ref[...]
    za = jnp.dot(act, wa_ref[...], preferred_element_type=F32)
    zb = jnp.dot(act, wb_ref[...], preferred_element_type=F32)
    s_branch = za * jax.nn.sigmoid(zb)
    o_ref[...] = (ga_ref[...].astype(F32) + gs_ref[...].astype(F32) * s_branch).astype(o_ref.dtype)


def _ssm_glu_merge(y, u, d_skip, w_ssm, layer, gated_attn, gates):
    t, width = y.shape
    dm = gated_attn.shape[1]
    tm, tn = 1024, 512
    nj = dm // tn
    return pl.pallas_call(
        _ssm_glu_body,
        grid=(t // tm, nj),
        in_specs=[
            pl.BlockSpec((tm, width), lambda i, j: (i, 0)),
            pl.BlockSpec((tm, width), lambda i, j: (i, 0)),
            pl.BlockSpec((1, width), lambda i, j: (0, 0)),
            pl.BlockSpec((None, width, tn), lambda i, j: (layer, 0, j)),
            pl.BlockSpec((None, width, tn), lambda i, j: (layer, 0, j + nj)),
            pl.BlockSpec((tm, tn), lambda i, j: (i, j)),
            pl.BlockSpec((tm, tn), lambda i, j: (i, j + nj)),
        ],
        out_specs=pl.BlockSpec((tm, tn), lambda i, j: (i, j)),
        out_shape=jax.ShapeDtypeStruct((t, dm), BF16),
        scratch_shapes=[pltpu.VMEM((tm, width), BF16)],
        compiler_params=_params("parallel", "arbitrary"),
        name="s5_glu_merge",
    )(y, u, d_skip, w_ssm, w_ssm, gated_attn, gates)


def _resid_mm_body(x_ref, w_ref, h_ref, o_ref):
    x = x_ref[...]
    tn = 512
    for c in range(0, o_ref.shape[1], tn):
        o_ref[:, c:c + tn] = h_ref[:, c:c + tn] + jnp.dot(x, w_ref[:, c:c + tn], preferred_element_type=F32)


def _resid_matmul(x, w, layer, h):
    t, k = x.shape
    dm = w.shape[2]
    tm = 1024
    return pl.pallas_call(
        _resid_mm_body,
        grid=(t // tm,),
        in_specs=[
            pl.BlockSpec((tm, k), lambda i: (i, 0)),
            pl.BlockSpec((None, k, dm), lambda i: (layer, 0, 0), pipeline_mode=pl.Buffered(1)),
            pl.BlockSpec((tm, dm), lambda i: (i, 0)),
        ],
        out_specs=pl.BlockSpec((tm, dm), lambda i: (i, 0)),
        out_shape=jax.ShapeDtypeStruct((t, dm), F32),
        compiler_params=_params("parallel"),
        name="out_proj",
    )(x, w, h)


def _router_body(h_ref, g_ref, wr_ref, xn_ref, lg_ref):
    xn = _rms(h_ref[...], g_ref[...])
    xn_ref[...] = xn.astype(BF16)
    lg_ref[...] = lax.dot_general(wr_ref[...], xn, (((1,), (1,)), ((), ())),
                                  precision=lax.Precision.HIGHEST, preferred_element_type=F32)


def _router_logits(h, gain, w_router_t):
    t, dm = h.shape
    tm = 512
    return pl.pallas_call(
        _router_body,
        grid=(t // tm,),
        in_specs=[
            pl.BlockSpec((tm, dm), lambda i: (i, 0)),
            pl.BlockSpec((1, dm), lambda i: (0, 0)),
            pl.BlockSpec((N_EXPERTS, dm), lambda i: (0, 0)),
        ],
        out_specs=[pl.BlockSpec((tm, dm), lambda i: (i, 0)),
                   pl.BlockSpec((N_EXPERTS, tm), lambda i: (0, i))],
        out_shape=[jax.ShapeDtypeStruct((t, dm), BF16), jax.ShapeDtypeStruct((N_EXPERTS, t), F32)],
        compiler_params=_params("parallel"),
        name="router_logits",
    )(h, gain, w_router_t)


def _select_body(lg_ref, gate_ref, pos_ref, sel_ref, *, cap):
    lg = lg_ref[...]
    ne, s = lg.shape
    m = jnp.max(lg, axis=0, keepdims=True)
    e = jnp.exp(lg - m)
    aff = e / jnp.sum(e, axis=0, keepdims=True)
    bits = lax.bitcast_convert_type(aff, I32)

    def count(ind):
        return jnp.sum(ind, axis=1, keepdims=True)

    def value_bit(k, thr):
        cand = thr | jnp.left_shift(jnp.int32(1), 30 - k)
        return jnp.where(count(jnp.where(bits >= cand, 1.0, 0.0)) >= cap, cand, thr)

    thr = lax.fori_loop(0, 31, value_bit, jnp.zeros((ne, 1), I32))
    above = jnp.where(bits > thr, 1.0, 0.0)
    tie = jnp.where(bits == thr, 1.0, 0.0)
    need = cap - count(above)
    idx = lax.broadcasted_iota(I32, (ne, s), 1)
    n_bits = int(math.log2(s)) + 1

    def index_bit(k, bound):
        cand = bound | jnp.left_shift(jnp.int32(1), n_bits - 1 - k)
        below = count(jnp.where(idx < cand, tie, 0.0))
        return jnp.where(below < need, cand, bound)

    bound = lax.fori_loop(0, n_bits, index_bit, jnp.zeros((ne, 1), I32))
    sel = above + jnp.where(idx <= bound, tie, 0.0)
    gate_ref[...] = sel * aff
    sel_ref[...] = sel.astype(I32)

    blk = 256
    tri = jnp.where(lax.broadcasted_iota(I32, (blk, blk), 0) <= lax.broadcasted_iota(I32, (blk, blk), 1),
                    1.0, 0.0).astype(BF16)
    carry = jnp.zeros((ne, 1), F32)
    for j in range(s // blk):
        seg = sel[:, j * blk:(j + 1) * blk]
        inc = jnp.dot(seg.astype(BF16), tri, preferred_element_type=F32)
        pos_ref[:, j * blk:(j + 1) * blk] = (inc - seg + carry).astype(I32)
        carry = carry + inc[:, blk - 1:blk]


def _select_tokens(logits_t, bsz, seq):
    cap = CAPACITY_FACTOR * seq // N_EXPERTS
    t = bsz * seq
    spec = pl.BlockSpec((N_EXPERTS, seq), lambda b: (0, b))
    return pl.pallas_call(
        functools.partial(_select_body, cap=cap),
        grid=(bsz,),
        in_specs=[spec],
        out_specs=[spec, spec, spec],
        out_shape=[jax.ShapeDtypeStruct((N_EXPERTS, t), F32), jax.ShapeDtypeStruct((N_EXPERTS, t), I32),
                   jax.ShapeDtypeStruct((N_EXPERTS, t), I32)],
        compiler_params=_params("parallel"),
        name="expert_choice_select",
    )(logits_t)


def _slot_window(base_ref, b, e, i, nt, cap):
    k = (b * N_EXPERTS + e) * (nt + 1) + i
    lo = base_ref[k]
    start = jnp.minimum((lo // SLOT_ALIGN) * SLOT_ALIGN, cap - SLOT_WIN)
    return pl.multiple_of(start, SLOT_ALIGN), base_ref[k + 1]


def _n_extra_windows(start, hi):
    return (jnp.maximum(hi - (start + SLOT_WIN), 0) + SLOT_WIN - 1) // SLOT_WIN


def _extra_window(start, k, cap):
    first = start + SLOT_WIN * k
    return first, pl.multiple_of(jnp.minimum(first, cap - SLOT_WIN), SLOT_ALIGN)


def _dispatch_body(base_ref, x_ref, slot_ref, out_ref, *, nt, cap):
    b = pl.program_id(0)
    i = pl.program_id(2)
    ts = x_ref.shape[0]

    @pl.when(i == 0)
    def _():
        out_ref[...] = jnp.zeros_like(out_ref)

    x = x_ref[...]
    wio = lax.broadcasted_iota(I32, (SLOT_WIN, ts), 0)
    wins = [_slot_window(base_ref, b, e, i, nt, cap) for e in range(N_EXPERTS)]
    onehot = jnp.concatenate(
        [jnp.where(slot_ref[e:e + 1, :] - wins[e][0] == wio, 1.0, 0.0).astype(x.dtype) for e in range(N_EXPERTS)],
        axis=0)
    res = jnp.dot(onehot, x, preferred_element_type=F32)
    for e in range(N_EXPERTS):
        out_ref[e, pl.ds(wins[e][0], SLOT_WIN), :] += res[e * SLOT_WIN:(e + 1) * SLOT_WIN].astype(out_ref.dtype)

    for e in range(N_EXPERTS):
        start, hi = wins[e]

        def extra(k, carry, e=e, start=start):
            first, st = _extra_window(start, k, cap)
            srow = slot_ref[e:e + 1, :]
            srow = jnp.where(srow >= first, srow, -1)
            oh = jnp.where(srow - st == wio, 1.0, 0.0).astype(x.dtype)
            out_ref[e, pl.ds(st, SLOT_WIN), :] += jnp.dot(oh, x, preferred_element_type=F32).astype(out_ref.dtype)
            return carry

        lax.fori_loop(1, 1 + _n_extra_windows(start, hi), extra, 0)


def _dispatch(base, x, slot_t, bsz, seq, cap, *, cw, out_dtype, name):
    t, width = x.shape
    ts = MOE_TILE
    nt = seq // ts
    grid_spec = pltpu.PrefetchScalarGridSpec(
        num_scalar_prefetch=1,
        grid=(bsz, width // cw, nt),
        in_specs=[
            pl.BlockSpec((ts, cw), lambda b, c, i, base: (b * nt + i, c)),
            pl.BlockSpec((N_EXPERTS, ts), lambda b, c, i, base: (0, b * nt + i)),
        ],
        out_specs=pl.BlockSpec((None, N_EXPERTS, cap, cw), lambda b, c, i, base: (b, 0, 0, c)),
    )
    return pl.pallas_call(
        functools.partial(_dispatch_body, nt=nt, cap=cap),
        grid_spec=grid_spec,
        out_shape=jax.ShapeDtypeStruct((bsz, N_EXPERTS, cap, width), out_dtype),
        compiler_params=_params("parallel", "parallel", "arbitrary"),
        name=name,
    )(base, x, slot_t)


def _expert_body(x_ref, wg_ref, wu_ref, wd_ref, gs_ref, y_ref, acc_ref):
    e = pl.program_id(0)
    f = pl.program_id(2)
    x = x_ref[...]
    hid = (jax.nn.silu(jnp.dot(x, wg_ref[...], preferred_element_type=F32))
           * jnp.dot(x, wu_ref[...], preferred_element_type=F32)).astype(BF16)
    part = jnp.dot(hid, wd_ref[...], preferred_element_type=F32)

    @pl.when(f == 0)
    def _():
        acc_ref[...] = part

    @pl.when(f > 0)
    def _():
        acc_ref[...] += part

    @pl.when(f == pl.num_programs(2) - 1)
    def _():
        pieces = gs_ref[...]
        lane = lax.broadcasted_iota(I32, pieces.shape, 1)
        mine = (lane >= 3 * e) & (lane < 3 * e + 3)
        gate = jnp.sum(jnp.where(mine, pieces, 0.0), axis=1, keepdims=True)
        y_ref[...] = (acc_ref[...] * gate).astype(y_ref.dtype)


def _expert_ffn(xg, w_gate, w_up, w_down, layer, gate_slots):
    bsz, ne, cap, dm = xg.shape
    ff = w_gate.shape[3]
    tf = 512
    return pl.pallas_call(
        _expert_body,
        grid=(ne, bsz, ff // tf),
        in_specs=[
            pl.BlockSpec((None, None, cap, dm), lambda e, b, f: (b, e, 0, 0)),
            pl.BlockSpec((None, None, dm, tf), lambda e, b, f: (layer, e, 0, f)),
            pl.BlockSpec((None, None, dm, tf), lambda e, b, f: (layer, e, 0, f)),
            pl.BlockSpec((None, None, tf, dm), lambda e, b, f: (layer, e, f, 0)),
            pl.BlockSpec((None, None, cap, 128), lambda e, b, f: (b, e, 0, 0)),
        ],
        out_specs=pl.BlockSpec((None, None, cap, dm), lambda e, b, f: (b, e, 0, 0)),
        out_shape=jax.ShapeDtypeStruct((bsz, ne, cap, dm), BF16),
        scratch_shapes=[pltpu.VMEM((cap, dm), F32)],
        compiler_params=_params("parallel", "parallel", "arbitrary"),
        name="expert_swiglu",
    )(xg, w_gate, w_up, w_down, gate_slots)


def _combine_body(base_ref, yg_hbm, h_ref, slot_ref, out_ref, wbuf, xbuf, wsem, xsem, *, nt, cap, n_steps):
    n = pl.program_id(0)
    ts = h_ref.shape[0]

    def window_copies(step, half):
        b = step // nt
        i = step % nt
        copies = []
        for e in range(N_EXPERTS):
            start, _ = _slot_window(base_ref, b, e, i, nt, cap)
            copies.append(pltpu.make_async_copy(yg_hbm.at[b, e, pl.ds(start, SLOT_WIN), :],
                                                wbuf.at[half, e], wsem.at[half]))
        return copies

    @pl.when(n == 0)
    def _():
        for c in window_copies(0, 0):
            c.start()

    @pl.when(n + 1 < n_steps)
    def _():
        for c in window_copies(n + 1, (n + 1) % 2):
            c.start()

    b = n // nt
    i = n % nt
    half = n % 2
    lane = lax.broadcasted_iota(I32, (ts, SLOT_WIN), 1)
    wins = [_slot_window(base_ref, b, e, i, nt, cap) for e in range(N_EXPERTS)]
    for c in window_copies(n, half):
        c.wait()
    acc = h_ref[...]
    for e0 in range(0, N_EXPERTS, MOE_STACK):
        onehot = jnp.concatenate(
            [jnp.where(slot_ref[:, e:e + 1] - wins[e][0] == lane, 1.0, 0.0).astype(BF16)
             for e in range(e0, e0 + MOE_STACK)], axis=1)
        rows = wbuf[half, e0:e0 + MOE_STACK].reshape(MOE_STACK * SLOT_WIN, wbuf.shape[-1])
        acc = acc + jnp.dot(onehot, rows, preferred_element_type=F32)
    out_ref[...] = acc

    for e in range(N_EXPERTS):
        start, hi = wins[e]

        def extra(k, carry, e=e, start=start):
            first, st = _extra_window(start, k, cap)
            copy = pltpu.make_async_copy(yg_hbm.at[b, e, pl.ds(st, SLOT_WIN), :], xbuf, xsem)
            copy.start()
            scol = slot_ref[:, e:e + 1]
            scol = jnp.where(scol >= first, scol, -1)
            oh = jnp.where(scol - st == lane, 1.0, 0.0).astype(BF16)
            copy.wait()
            out_ref[...] += jnp.dot(oh, xbuf[...], preferred_element_type=F32)
            return carry

        lax.fori_loop(1, 1 + _n_extra_windows(start, hi), extra, 0)


def _combine(base, yg, h, slot_tok, bsz, seq):
    t, dm = h.shape
    cap = yg.shape[2]
    ts = MOE_TILE
    nt = seq // ts
    n_steps = bsz * nt
    grid_spec = pltpu.PrefetchScalarGridSpec(
        num_scalar_prefetch=1,
        grid=(n_steps,),
        in_specs=[
            pl.BlockSpec(memory_space=pl.ANY),
            pl.BlockSpec((ts, dm), lambda n, base: (n, 0)),
            pl.BlockSpec((ts, N_EXPERTS), lambda n, base: (n, 0)),
        ],
        out_specs=pl.BlockSpec((ts, dm), lambda n, base: (n, 0)),
        scratch_shapes=[pltpu.VMEM((2, N_EXPERTS, SLOT_WIN, dm), BF16),
                        pltpu.VMEM((SLOT_WIN, dm), BF16),
                        pltpu.SemaphoreType.DMA((2,)),
                        pltpu.SemaphoreType.DMA(())],
    )
    return pl.pallas_call(
        functools.partial(_combine_body, nt=nt, cap=cap, n_steps=n_steps),
        grid_spec=grid_spec,
        out_shape=jax.ShapeDtypeStruct((t, dm), F32),
        compiler_params=_params("arbitrary"),
        name="moe_combine",
    )(base, yg, h, slot_tok)


def _moe(h, gain, w_router_t, w_gate, w_up, w_down, layer, bsz, seq):
    cap = CAPACITY_FACTOR * seq // N_EXPERTS
    xn, logits_t = _router_logits(h, gain, w_router_t)
    gate_t, pos_t, sel_t = _select_tokens(logits_t, bsz, seq)

    slot_t = jnp.where(sel_t > 0, pos_t, -1)
    ts = MOE_TILE
    nt = seq // ts
    starts = pos_t.reshape(N_EXPERTS, bsz, nt, ts)[..., 0]
    base = jnp.concatenate([jnp.transpose(starts, (1, 0, 2)),
                            jnp.full((bsz, N_EXPERTS, 1), cap, I32)], axis=-1).reshape(-1)
    gate_tok = gate_t.T
    g_hi = gate_tok.astype(BF16)
    r1 = gate_tok - g_hi.astype(F32)
    g_mid = r1.astype(BF16)
    g_lo = (r1 - g_mid.astype(F32)).astype(BF16)
    pieces = jnp.stack([g_hi, g_mid, g_lo], axis=-1).reshape(-1, 3 * N_EXPERTS)
    pieces = jnp.pad(pieces, ((0, 0), (0, 128 - 3 * N_EXPERTS)))

    xg = _dispatch(base, xn, slot_t, bsz, seq, cap, cw=512, out_dtype=BF16, name="moe_dispatch")
    gate_slots = _dispatch(base, pieces, slot_t, bsz, seq, cap, cw=128, out_dtype=F32, name="moe_gate_dispatch")
    yg = _expert_ffn(xg, w_gate, w_up, w_down, layer, gate_slots)
    return _combine(base, yg, h, slot_t.T, bsz, seq)


def _ple_body(h_ref, g_ref, wg_ref, p_ref, wp_ref, hres_ref, o_ref, xn_ref):
    @pl.when(pl.program_id(1) == 0)
    def _():
        xn_ref[...] = _rms(h_ref[...], g_ref[...]).astype(BF16)

    gate = jax.nn.sigmoid(jnp.dot(xn_ref[...], wg_ref[...], preferred_element_type=F32))
    ple = jnp.dot(p_ref[...].astype(BF16), wp_ref[...], preferred_element_type=F32)
    o_ref[...] = hres_ref[...] + gate * ple


def _ple(h, gain, w_gate, p, w_proj, layer):
    t, dm = h.shape
    tm, tn = 1024, 512
    pd = p.shape[2]
    return pl.pallas_call(
        _ple_body,
        grid=(t // tm, dm // tn),
        in_specs=[
            pl.BlockSpec((tm, dm), lambda i, j: (i, 0)),
            pl.BlockSpec((1, dm), lambda i, j: (0, 0)),
            pl.BlockSpec((None, dm, tn), lambda i, j: (layer, 0, j)),
            pl.BlockSpec((None, tm, pd), lambda i, j: (layer, i, 0)),
            pl.BlockSpec((None, pd, tn), lambda i, j: (layer, 0, j)),
            pl.BlockSpec((tm, tn), lambda i, j: (i, j)),
        ],
        out_specs=pl.BlockSpec((tm, tn), lambda i, j: (i, j)),
        out_shape=jax.ShapeDtypeStruct((t, dm), F32),
        scratch_shapes=[pltpu.VMEM((tm, dm), BF16)],
        compiler_params=_params("parallel", "arbitrary"),
        name="ple_gate",
    )(h, gain, w_gate, p, w_proj, h)


def _rotary_lane_order(a):
    blocks = HEAD_DIM // ROT_HALF
    order = list(range(blocks))
    order[1], order[blocks // 2] = order[blocks // 2], order[1]
    shaped = a.reshape(a.shape[:-1] + (a.shape[-1] // HEAD_DIM, blocks, ROT_HALF))
    return shaped[..., jnp.array(order), :].reshape(a.shape)


def _rope_tables(positions):
    inv_freq = jnp.power(ROPE_THETA, -jnp.arange(ROT_HALF, dtype=F32) * 2.0 / (2 * ROT_HALF))
    ang = positions.astype(F32)[..., None] * inv_freq
    cos, sin = jnp.cos(ang), jnp.sin(ang)
    gap = HEAD_DIM // 2 - ROT_HALF
    ones = jnp.ones(ang.shape[:-1] + (gap,), F32)
    zeros = jnp.zeros(ang.shape[:-1] + (gap,), F32)
    cos_t = jnp.concatenate([cos, ones, cos, ones], axis=-1)
    sin_t = jnp.concatenate([-sin, zeros, sin, zeros], axis=-1)
    return cos_t.reshape(-1, HEAD_DIM), sin_t.reshape(-1, HEAD_DIM)


def kernel(x, p, positions, norm_mix, w_in, q_norm, k_norm, w_attn_br, ssm_a_re, ssm_a_im, ssm_log_dt,
           ssm_b_re, ssm_b_im, ssm_c_re, ssm_c_im, ssm_d, w_ssm_br, w_out, norm_ffn, w_router,
           w_exp_gate, w_exp_up, w_exp_down, norm_ple, w_ple_gate, w_ple_proj):
    bsz, seq, dm = x.shape
    depth = w_in.shape[0]
    t = bsz * seq
    n_attn = 3 * HEADS_PER_GROUP * len(DILATIONS) * HEAD_DIM
    ssm_width = ssm_d.shape[1]
    nc = seq // SSM_CHUNK
    n_steps = int(math.log2(nc))
    assert seq % PERM_TILE == 0 and nc == 1 << n_steps

    cos_t, sin_t = _rope_tables(positions)
    w_in_b, w_attn_b, w_ssm_b, w_out_b = (w.astype(BF16) for w in (w_in, w_attn_br, w_ssm_br, w_out))
    w_eg_b, w_eu_b, w_ed_b = (w.astype(BF16) for w in (w_exp_gate, w_exp_up, w_exp_down))
    w_pg_b, w_pp_b = w_ple_gate.astype(BF16), w_ple_proj.astype(BF16)
    w_qk_b = _rotary_lane_order(w_in_b[:, :, :2 * n_attn // 3])
    comp, tabs = _ssm_fold(ssm_a_re, ssm_a_im, ssm_log_dt, ssm_b_re, ssm_b_im, ssm_c_re, ssm_c_im, n_steps)
    w_slab = _ssm_expand(comp)
    p_rows = p.reshape(depth, t, p.shape[-1])
    h = x.reshape(t, dm)

    for l in range(depth):
        gain = norm_mix[l][None]
        qk_gain = _rotary_lane_order(jnp.stack([q_norm[l], k_norm[l]]))[:, None, :]

        u, xn = _norm_matmul(h, gain, w_in_b, l, n_attn, ssm_width, out_dtype=F32, name="u_proj")
        gates = _sigmoid_matmul(xn, w_in_b, l, n_attn + ssm_width, 2 * dm, name="gate_proj")
        outs, lses = zip(*[_attention(_qkv_proj(xn, w_qk_b, w_in_b, l, gi, qk_gain, cos_t, sin_t), gi, bsz, seq)
                           for gi in range(len(DILATIONS))])
        gated_attn = _attn_merge_proj(outs, lses, w_attn_b, l, gates)

        y = _ssm_scan(u, w_slab, tabs, l, bsz, seq)
        merged = _ssm_glu_merge(y, u, ssm_d[l][None], w_ssm_b, l, gated_attn, gates)
        h = _resid_matmul(merged, w_out_b, l, h)

        h = _moe(h, norm_ffn[l][None], w_router[l].T, w_eg_b, w_eu_b, w_ed_b, l, bsz, seq)
        h = _ple(h, norm_ple[l][None], w_pg_b, p_rows, w_pp_b, l)
    return h.reshape(bsz, seq, dm)
```
